```python
import math
import jax, jax.numpy as jnp
from jax import lax
import numpy as np

D_MODEL = 1024
BATCH = 16
SEQ = 4096
DEPTH = 2
DEC_BATCH = 16
DEC_SEQ = 32
PAST_LEN = 4096

CHUNK = 64
HEAD_DIM = 64
A_HEADS = 4
A_V = 2 * HEAD_DIM
A_WIDTH = A_HEADS * A_V
B_HEADS = 8
B_WIDTH = B_HEADS * HEAD_DIM
BAND_CHUNKS = 8
BAND_PAST = BAND_CHUNKS * CHUNK
BAND_KEYS = BAND_PAST + CHUNK
REL_CLIP = 128
T5_BUCKETS = 32
T5_MAX_DIST = 128
Q_BLOCK = 128
AB_IN = 3 * A_WIDTH + 3 * B_WIDTH
AB_OUT = A_WIDTH + B_WIDTH
AB_SPLITS = (A_WIDTH, 2 * A_WIDTH, 3 * A_WIDTH, 3 * A_WIDTH + B_WIDTH, 3 * A_WIDTH + 2 * B_WIDTH)
C_HEADS = 4
C_INNER = D_MODEL
C_HEAD_DIM = C_INNER // C_HEADS
C_CONV = 4
C_IN = 2 * C_INNER + 2 * C_HEADS
MLSTM_CHUNK = CHUNK
N_EXPERTS = 64
TOP_K = 8
D_EXPERT = 256
D_SHARED = 256
ROUTE_SCALE = 2.5
MOE_BLOCK = 64
N_AB = (DEPTH + 1) // 2
N_C = DEPTH // 2
EPS = 1e-6

kernel_name = 'hybrid_stream_diffattn_band_mlstm_moe_step'


def rms_norm(x, g):
    xf = x.astype(jnp.float32)
    y = xf * lax.rsqrt(jnp.mean(xf * xf, axis=-1, keepdims=True) + EPS)
    return (y * g.astype(jnp.float32)).astype(x.dtype)


def modulation(c, w_mod, b_mod):
    m = jax.nn.silu(c) @ w_mod + b_mod
    return jnp.split(m[:, None, :], 6, axis=-1)


def t5_bucket(rel):
    half = T5_BUCKETS // 2
    exact = half // 2
    n = jnp.abs(rel)
    large = exact + (jnp.log(jnp.maximum(n, 1).astype(jnp.float32) / exact)
                     / math.log(T5_MAX_DIST / exact) * (half - exact)).astype(jnp.int32)
    large = jnp.minimum(large, half - 1)
    return jnp.where(rel > 0, half, 0) + jnp.where(n < exact, n, large)


def diff_lambda(lam_p, lam_init):
    lp = lam_p.astype(jnp.float32)
    return jnp.exp(jnp.sum(lp[0] * lp[1])) - jnp.exp(jnp.sum(lp[2] * lp[3])) + lam_init


def diff_attn_core(q, k, v, qpos, kpos, lam, t5_bias):
    s = jnp.einsum('bqhcd,bkhcd->cbhqk', q, k, preferred_element_type=jnp.float32) * HEAD_DIM ** -0.5
    rel = kpos[None, :] - qpos[:, None]
    bias = jnp.transpose(t5_bias[t5_bucket(rel)], (2, 0, 1)).astype(jnp.float32)
    visible = (kpos[None, :] // CHUNK) <= (qpos[:, None] // CHUNK)
    p = jax.nn.softmax(jnp.where(visible, s + bias, -jnp.inf), axis=-1)
    a = p[0] - lam * p[1]
    return jnp.einsum('bhqk,bkhe->bqhe', a, v.astype(jnp.float32))


def diff_attn_prompt(q, k, v, lam, t5_bias):
    Bn, S = q.shape[:2]
    nb = S // Q_BLOCK
    kpos = jnp.arange(S)
    qb = jnp.moveaxis(q.reshape((Bn, nb, Q_BLOCK) + q.shape[2:]), 1, 0)

    def block(args):
        qi, i = args
        return diff_attn_core(qi, k, v, i * Q_BLOCK + jnp.arange(Q_BLOCK), kpos, lam, t5_bias)

    o = lax.map(block, (qb, jnp.arange(nb)))
    return jnp.moveaxis(o, 0, 1).reshape(Bn, S, A_HEADS, A_V)


def band_core(q, k, v, qpos, kpos, rel_bias, valid):
    s = jnp.einsum('bqhd,bkhd->bhqk', q, k, preferred_element_type=jnp.float32) * HEAD_DIM ** -0.5
    rel = jnp.clip(kpos[None, :] - qpos[:, None], -REL_CLIP, REL_CLIP) + REL_CLIP
    bias = jnp.transpose(rel_bias[rel], (2, 0, 1)).astype(jnp.float32)
    p = jax.nn.softmax(jnp.where(valid, s + bias, -jnp.inf), axis=-1)
    return jnp.einsum('bhqk,bkhd->bqhd', p, v.astype(jnp.float32))


def band_attn_prompt(q, k, v, rel_bias):
    Bn, S = q.shape[:2]
    nc = S // CHUNK
    pad = ((0, 0), (BAND_PAST, 0), (0, 0), (0, 0))
    kp = jnp.pad(k, pad)
    vp = jnp.pad(v, pad)
    qc = jnp.moveaxis(q.reshape(Bn, nc, CHUNK, B_HEADS, HEAD_DIM), 1, 0)

    def chunk(args):
        qi, ci = args
        start = ci * CHUNK
        kb = lax.dynamic_slice_in_dim(kp, start, BAND_KEYS, axis=1)
        vb = lax.dynamic_slice_in_dim(vp, start, BAND_KEYS, axis=1)
        kpos = start - BAND_PAST + jnp.arange(BAND_KEYS)
        return band_core(qi, kb, vb, start + jnp.arange(CHUNK), kpos, rel_bias, kpos >= 0)

    o = lax.map(chunk, (qc, jnp.arange(nc)))
    return jnp.moveaxis(o, 0, 1).reshape(Bn, S, B_HEADS, HEAD_DIM)


def ab_project(h, w_in, qk_g):
    Bn, T = h.shape[:2]
    qa, ka, va, qb, kb, vb = jnp.split(h @ w_in, AB_SPLITS, axis=-1)
    qa = rms_norm(qa.reshape(Bn, T, A_HEADS, 2, HEAD_DIM), qk_g[0])
    ka = rms_norm(ka.reshape(Bn, T, A_HEADS, 2, HEAD_DIM), qk_g[1])
    va = va.reshape(Bn, T, A_HEADS, A_V)
    qb = rms_norm(qb.reshape(Bn, T, B_HEADS, HEAD_DIM), qk_g[2])
    kb = rms_norm(kb.reshape(Bn, T, B_HEADS, HEAD_DIM), qk_g[3])
    vb = vb.reshape(Bn, T, B_HEADS, HEAD_DIM)
    return qa, ka, va, qb, kb, vb


def ab_merge(oa, ob, head_g, w_out, lam_init, dtype):
    Bn, T = oa.shape[:2]
    oa = rms_norm(oa, head_g) * (1.0 - lam_init)
    o = jnp.concatenate([oa.reshape(Bn, T, A_WIDTH), ob.reshape(Bn, T, B_WIDTH)], axis=-1).astype(dtype)
    return o @ w_out


def causal_conv(xpad, w, b):
    T = xpad.shape[1] - (C_CONV - 1)
    return sum(xpad[:, j:j + T] * w[j] for j in range(C_CONV)) + b


def mlstm_scan(q, k, v, ig, lf, mem0, norm0, max0, blk):
    Bn, T = q.shape[:2]
    nc = T // blk

    def to_blocks(a):
        return jnp.moveaxis(a.astype(jnp.float32).reshape((Bn, nc, blk) + a.shape[2:]), 1, 0)

    causal = jnp.tril(jnp.ones((blk, blk), bool))[None, :, :, None]

    def step(carry, inp):
        mem, nrm_, mx = carry
        qc, kc, vc, ic, fc = inp
        b = jnp.cumsum(fc, axis=1)
        b_last = b[:, -1]
        logw = jnp.where(causal, b[:, :, None] - b[:, None] + ic[:, None], -jnp.inf)
        g = b + mx[:, None]
        m_t = jnp.maximum(g, jnp.max(logw, axis=2))
        w = jnp.exp(logw - m_t[:, :, None])
        inter = jnp.exp(g - m_t)
        a = w * jnp.einsum('bthd,bshd->btsh', qc, kc)
        num = jnp.einsum('btsh,bshe->bthe', a, vc) + inter[..., None] * jnp.einsum('bthd,bhde->bthe', qc, mem)
        den = jnp.sum(a, axis=2) + inter * jnp.einsum('bthd,bhd->bth', qc, nrm_)
        h = num / jnp.maximum(jnp.abs(den), jnp.exp(-m_t))[..., None]
        logs = b_last[:, None] - b + ic
        m_new = jnp.maximum(b_last + mx, jnp.max(logs, axis=1))
        decay = jnp.exp(b_last + mx - m_new)
        ws = jnp.exp(logs - m_new[:, None])
        mem = decay[..., None, None] * mem + jnp.einsum('bsh,bshd,bshe->bhde', ws, kc, vc)
        nrm_ = decay[..., None] * nrm_ + jnp.einsum('bsh,bshd->bhd', ws, kc)
        return (mem, nrm_, m_new), h

    init = (mem0.astype(jnp.float32), norm0.astype(jnp.float32), max0.astype(jnp.float32))
    (mem, nrm_, mx), h = lax.scan(step, init, tuple(map(to_blocks, (q, k, v, ig, lf))))
    return jnp.moveaxis(h, 0, 1).reshape(Bn, T, C_HEADS, C_HEAD_DIM), mem, nrm_, mx


def mlstm_mixer(h, conv_prev, mem0, norm0, max0, blk, w_in, b_if, conv_w, conv_b, w_qkv, head_g, skip, w_out):
    Bn, T = h.shape[:2]
    xc, o_pre, gate_pre = jnp.split(h @ w_in, (C_INNER, 2 * C_INNER), axis=-1)
    xpad = jnp.concatenate([conv_prev.astype(xc.dtype), xc], axis=1)
    xconv = jax.nn.silu(causal_conv(xpad, conv_w, conv_b))
    xh = xconv.reshape(Bn, T, C_HEADS, C_HEAD_DIM)
    q = jnp.einsum('bthd,hde->bthe', xh, w_qkv[0])
    k = jnp.einsum('bthd,hde->bthe', xh, w_qkv[1]) * C_HEAD_DIM ** -0.5
    v = jnp.einsum('bthd,hde->bthe', xc.reshape(Bn, T, C_HEADS, C_HEAD_DIM), w_qkv[2])
    gates = gate_pre.astype(jnp.float32).reshape(Bn, T, 2, C_HEADS) + b_if.astype(jnp.float32)
    ig = gates[:, :, 0]
    lf = jax.nn.log_sigmoid(gates[:, :, 1])
    hc, mem, nrm_, mx = mlstm_scan(q, k, v, ig, lf, mem0, norm0, max0, blk)
    hc = hc.reshape(Bn, T, C_INNER) * jax.nn.sigmoid(o_pre.astype(jnp.float32))
    hc = rms_norm(hc.reshape(Bn, T, C_HEADS, C_HEAD_DIM), head_g.reshape(C_HEADS, C_HEAD_DIM)).reshape(Bn, T, C_INNER)
    out = (hc + skip * xconv).astype(h.dtype) @ w_out
    return out, mem, nrm_, mx, xpad[:, -(C_CONV - 1):]


def moe_dispatch(x, idx, gate, w1, w3, w2):
    N = x.shape[0]
    A = N * TOP_K
    n_blocks = (A + N_EXPERTS * (MOE_BLOCK - 1) + MOE_BLOCK - 1) // MOE_BLOCK
    cap = n_blocks * MOE_BLOCK
    flat_e = idx.reshape(-1)
    flat_tok = jnp.arange(A, dtype=jnp.int32) // TOP_K
    flat_g = gate.reshape(-1)
    order = jnp.argsort(flat_e)
    se = flat_e[order]
    counts = jnp.bincount(flat_e, length=N_EXPERTS)
    padded = (counts + MOE_BLOCK - 1) // MOE_BLOCK * MOE_BLOCK
    start = jnp.cumsum(counts) - counts
    pend = jnp.cumsum(padded)
    pstart = pend - padded
    dest = pstart[se] + jnp.arange(A) - start[se]
    buf_tok = jnp.full((cap,), N, jnp.int32).at[dest].set(flat_tok[order])
    buf_g = jnp.zeros((cap,), flat_g.dtype).at[dest].set(flat_g[order])
    block_e = jnp.minimum(jnp.searchsorted(pend, jnp.arange(n_blocks) * MOE_BLOCK, side='right'), N_EXPERTS - 1)
    x_pad = jnp.concatenate([x, jnp.zeros((1, x.shape[1]), x.dtype)], axis=0)

    def block(args):
        tok, g, e = args
        xb = x_pad[tok]
        hid = jax.nn.silu(xb @ w1[e]) * (xb @ w3[e])
        return (hid @ w2[e]).astype(jnp.float32) * g[:, None]

    out = lax.map(block, (buf_tok.reshape(n_blocks, MOE_BLOCK), buf_g.reshape(n_blocks, MOE_BLOCK), block_e))
    y = jax.ops.segment_sum(out.reshape(cap, -1), buf_tok, num_segments=N + 1)
    return y[:N]


def moe_tokens(h, rw, rb, w1, w3, w2, s1, s3, s2):
    s = jax.nn.sigmoid(h.astype(jnp.float32) @ rw.astype(jnp.float32))
    _, idx = lax.top_k(s + rb.astype(jnp.float32), TOP_K)
    g = jnp.take_along_axis(s, idx, axis=-1)
    g = g / jnp.sum(g, axis=-1, keepdims=True) * ROUTE_SCALE
    routed = moe_dispatch(h, idx, g, w1, w3, w2)
    shared = (jax.nn.silu(h @ s1) * (h @ s3)) @ s2
    return (routed + shared).astype(h.dtype)


def setup_inputs(seed: int = 0) -> dict:
    key = jax.random.key(seed)
    ks = iter(jax.random.split(key, 64))

    def nrm(shape, scale=1.0):
        return jax.random.normal(next(ks), shape, jnp.float32) * scale

    lb = min(BAND_PAST, PAST_LEN)
    f_bias = jnp.broadcast_to(jnp.linspace(3.0, 6.0, C_HEADS), (N_C, C_HEADS))
    return {
        'x_prompt': nrm((BATCH, SEQ, D_MODEL)),
        'x_sample': nrm((DEC_BATCH, DEC_SEQ, D_MODEL)),
        'c_prompt': nrm((BATCH, D_MODEL)),
        'c_sample': nrm((DEC_BATCH, D_MODEL)),
        'cache_a_k': nrm((N_AB, DEC_BATCH, PAST_LEN, A_HEADS, 2, HEAD_DIM)),
        'cache_a_v': nrm((N_AB, DEC_BATCH, PAST_LEN, A_HEADS, A_V)),
        'cache_b_k': nrm((N_AB, DEC_BATCH, lb, B_HEADS, HEAD_DIM)),
        'cache_b_v': nrm((N_AB, DEC_BATCH, lb, B_HEADS, HEAD_DIM)),
        'state_c_mem': nrm((N_C, DEC_BATCH, C_HEADS, C_HEAD_DIM, C_HEAD_DIM), 0.1),
        'state_c_norm': nrm((N_C, DEC_BATCH, C_HEADS, C_HEAD_DIM), 0.1),
        'state_c_max': nrm((N_C, DEC_BATCH, C_HEADS)),
        'state_c_conv': nrm((N_C, DEC_BATCH, C_CONV - 1, C_INNER)),
        'norm_g': 1.0 + nrm((DEPTH, 2, D_MODEL), 0.02),
        'w_mod': nrm((DEPTH, D_MODEL, 6 * D_MODEL), 0.5 * D_MODEL ** -0.5),
        'b_mod': nrm((DEPTH, 6 * D_MODEL), 0.02),
        't5_bias': nrm((T5_BUCKETS, A_HEADS), 0.5),
        'ab_w_in': nrm((N_AB, D_MODEL, AB_IN), D_MODEL ** -0.5),
        'ab_qk_g': 1.0 + nrm((N_AB, 4, HEAD_DIM), 0.02),
        'ab_lambda': nrm((N_AB, 4, HEAD_DIM), 0.1),
        'ab_head_g': 1.0 + nrm((N_AB, A_V), 0.02),
        'ab_rel_bias': nrm((N_AB, 2 * REL_CLIP + 1, B_HEADS), 0.5),
        'ab_w_out': nrm((N_AB, AB_OUT, D_MODEL), AB_OUT ** -0.5),
        'c_w_in': nrm((N_C, D_MODEL, C_IN), D_MODEL ** -0.5),
        'c_b_if': jnp.stack([nrm((N_C, C_HEADS), 0.1), f_bias + nrm((N_C, C_HEADS), 0.1)], axis=1),
        'c_conv_w': nrm((N_C, C_CONV, C_INNER), C_CONV ** -0.5),
        'c_conv_b': nrm((N_C, C_INNER), 0.02),
        'c_w_qkv': nrm((N_C, 3, C_HEADS, C_HEAD_DIM, C_HEAD_DIM), C_HEAD_DIM ** -0.5),
        'c_head_g': 1.0 + nrm((N_C, C_INNER), 0.02),
        'c_skip': 1.0 + nrm((N_C, C_INNER), 0.02),
        'c_w_out': nrm((N_C, C_INNER, D_MODEL), C_INNER ** -0.5),
        'router_w': nrm((DEPTH, D_MODEL, N_EXPERTS), D_MODEL ** -0.5),
        'router_b': nrm((DEPTH, N_EXPERTS), 0.01),
        'exp_w1': nrm((DEPTH, N_EXPERTS, D_MODEL, D_EXPERT), D_MODEL ** -0.5),
        'exp_w3': nrm((DEPTH, N_EXPERTS, D_MODEL, D_EXPERT), D_MODEL ** -0.5),
        'exp_w2': nrm((DEPTH, N_EXPERTS, D_EXPERT, D_MODEL), D_EXPERT ** -0.5),
        'sh_w1': nrm((DEPTH, D_MODEL, D_SHARED), D_MODEL ** -0.5),
        'sh_w3': nrm((DEPTH, D_MODEL, D_SHARED), D_MODEL ** -0.5),
        'sh_w2': nrm((DEPTH, D_SHARED, D_MODEL), D_SHARED ** -0.5),
    }


def reference(x_prompt, x_sample, c_prompt, c_sample, cache_a_k, cache_a_v, cache_b_k, cache_b_v,
              state_c_mem, state_c_norm, state_c_max, state_c_conv, norm_g, w_mod, b_mod, t5_bias,
              ab_w_in, ab_qk_g, ab_lambda, ab_head_g, ab_rel_bias, ab_w_out, c_w_in, c_b_if, c_conv_w,
              c_conv_b, c_w_qkv, c_head_g, c_skip, c_w_out, router_w, router_b, exp_w1, exp_w3, exp_w2,
              sh_w1, sh_w3, sh_w2):
    Bp, S = x_prompt.shape[:2]
    Bs, T = x_sample.shape[:2]
    past = cache_a_k.shape[2]
    lb = cache_b_k.shape[2]
    pos_s = past + jnp.arange(T)
    kpos_a = jnp.arange(past + T)
    kpos_b = past - lb + jnp.arange(lb + T)
    band_lo = (past // CHUNK - BAND_CHUNKS) * CHUNK
    keep_b = min(BAND_PAST, S)
    akp, avp, aks, avs, bkp, bvp, bks, bvs = [], [], [], [], [], [], [], []
    memp, normp, maxp, convp, mems, norms, maxs, convs = [], [], [], [], [], [], [], []
    xp, xs = x_prompt, x_sample
    for l in range(DEPTH):
        mp = modulation(c_prompt, w_mod[l], b_mod[l])
        ms = modulation(c_sample, w_mod[l], b_mod[l])
        hp = rms_norm(xp, norm_g[l, 0]) * (1.0 + mp[1]) + mp[0]
        hs = rms_norm(xs, norm_g[l, 0]) * (1.0 + ms[1]) + ms[0]
        i = l // 2
        if l % 2 == 0:
            lam_init = 0.8 - 0.6 * math.exp(-0.3 * l)
            lam = diff_lambda(ab_lambda[i], lam_init)
            qa, ka, va, qb, kb, vb = ab_project(hp, ab_w_in[i], ab_qk_g[i])
            oa = diff_attn_prompt(qa, ka, va, lam, t5_bias)
            ob = band_attn_prompt(qb, kb, vb, ab_rel_bias[i])
            mix_p = ab_merge(oa, ob, ab_head_g[i], ab_w_out[i], lam_init, xp.dtype)
            akp.append(ka); avp.append(va)
            bkp.append(kb[:, S - keep_b:]); bvp.append(vb[:, S - keep_b:])
            qa, ka, va, qb, kb, vb = ab_project(hs, ab_w_in[i], ab_qk_g[i])
            ka_all = jnp.concatenate([cache_a_k[i].astype(ka.dtype), ka], axis=1)
            va_all = jnp.concatenate([cache_a_v[i].astype(va.dtype), va], axis=1)
            oa = diff_attn_core(qa, ka_all, va_all, pos_s, kpos_a, lam, t5_bias)
            kb_all = jnp.concatenate([cache_b_k[i].astype(kb.dtype), kb], axis=1)
            vb_all = jnp.concatenate([cache_b_v[i].astype(vb.dtype), vb], axis=1)
            ob = band_core(qb, kb_all, vb_all, pos_s, kpos_b, ab_rel_bias[i], kpos_b >= band_lo)
            mix_s = ab_merge(oa, ob, ab_head_g[i], ab_w_out[i], lam_init, xs.dtype)
            aks.append(ka); avs.append(va); bks.append(kb); bvs.append(vb)
        else:
            zc = jnp.zeros((Bp, C_CONV - 1, C_INNER), hp.dtype)
            zm = jnp.zeros((Bp, C_HEADS, C_HEAD_DIM, C_HEAD_DIM), jnp.float32)
            zn = jnp.zeros((Bp, C_HEADS, C_HEAD_DIM), jnp.float32)
            zx = jnp.zeros((Bp, C_HEADS), jnp.float32)
            mix_p, m1, n1, x1, cv1 = mlstm_mixer(hp, zc, zm, zn, zx, MLSTM_CHUNK, c_w_in[i], c_b_if[i],
                                                 c_conv_w[i], c_conv_b[i], c_w_qkv[i], c_head_g[i], c_skip[i], c_w_out[i])
            mix_s, m2, n2, x2, cv2 = mlstm_mixer(hs, state_c_conv[i], state_c_mem[i], state_c_norm[i], state_c_max[i], T,
                                                 c_w_in[i], c_b_if[i], c_conv_w[i], c_conv_b[i], c_w_qkv[i],
                                                 c_head_g[i], c_skip[i], c_w_out[i])
            memp.append(m1); normp.append(n1); maxp.append(x1); convp.append(cv1)
            mems.append(m2); norms.append(n2); maxs.append(x2); convs.append(cv2)
        xp = xp + mp[2] * mix_p
        xs = xs + ms[2] * mix_s
        hp = rms_norm(xp, norm_g[l, 1]) * (1.0 + mp[4]) + mp[3]
        hs = rms_norm(xs, norm_g[l, 1]) * (1.0 + ms[4]) + ms[3]
        moe_args = (router_w[l], router_b[l], exp_w1[l], exp_w3[l], exp_w2[l], sh_w1[l], sh_w3[l], sh_w2[l])
        ffn_p = lax.map(lambda hb: moe_tokens(hb, *moe_args), hp)
        ffn_s = moe_tokens(hs.reshape(Bs * T, D_MODEL), *moe_args).reshape(Bs, T, D_MODEL)
        xp = xp + mp[5] * ffn_p
        xs = xs + ms[5] * ffn_s
    return (xp, xs,
            jnp.stack(akp), jnp.stack(avp), jnp.stack(aks), jnp.stack(avs),
            jnp.stack(bkp), jnp.stack(bvp), jnp.stack(bks), jnp.stack(bvs),
            jnp.stack(memp), jnp.stack(normp), jnp.stack(maxp), jnp.stack(convp),
            jnp.stack(mems), jnp.stack(norms), jnp.stack(maxs), jnp.stack(convs))
```

```python
import functools
import math

import jax
import jax.numpy as jnp
from jax import lax
from jax.experimental import pallas as pl
from jax.experimental.pallas import tpu as pltpu

F32 = jnp.float32
BF16 = jnp.bfloat16
I32 = jnp.int32

EPS = 1e-6
NEG = -1e30
CHUNK = 64
HEAD_DIM = 64
A_HEADS = 4
B_HEADS = 8
BAND_CHUNKS = 8
BAND_PAST = BAND_CHUNKS * CHUNK
REL_CLIP = 128
T5_BUCKETS = 32
T5_MAX_DIST = 128
C_HEADS = 4
C_HEAD_DIM = 256
C_CONV = 4
N_EXPERTS = 64
TOP_K = 8
ROUTE_SCALE = 2.5

LANES = 128
TM = 512
TQ_DIFF = 256
TQ_BAND = 128
ML_CHUNK = 256
BM = 256
TM_MOVE = 256
VMEM_LIMIT = 56 * 1024 * 1024


def _cparams(sem):
    return pltpu.CompilerParams(dimension_semantics=sem, vmem_limit_bytes=VMEM_LIMIT)


def _dot(a, b):
    return jnp.dot(a, b, preferred_element_type=F32)


def _dot_nt(a, b):
    return lax.dot_general(a, b, (((1,), (1,)), ((), ())), preferred_element_type=F32)


def _dot_tn(a, b):
    return lax.dot_general(a, b, (((0,), (0,)), ((), ())), preferred_element_type=F32)


def _split2(x):
    hi = x.astype(BF16)
    lo = (x - hi.astype(F32)).astype(BF16)
    return hi, lo


def _split3(x):
    p0 = x.astype(BF16)
    r1 = x - p0.astype(F32)
    p1 = r1.astype(BF16)
    p2 = (r1 - p1.astype(F32)).astype(BF16)
    return p0, p1, p2


def _rms(x, g):
    return x * lax.rsqrt(jnp.mean(x * x, axis=-1, keepdims=True) + EPS) * g


def _const_spec(shape):
    nd = len(shape)
    return pl.BlockSpec(shape, lambda *_: (0,) * nd, pipeline_mode=pl.Buffered(1))


def _mod_kernel(c_ref, w_ref, b_ref, o_ref):
    c = c_ref[...]
    a_hi, a_lo = _split2(c * jax.nn.sigmoid(c))
    w_hi, w_lo = _split2(w_ref[...])
    o_ref[...] = _dot(a_hi, w_hi) + _dot(a_hi, w_lo) + _dot(a_lo, w_hi) + b_ref[...]


def _modulation(c_all, w_mod, b_mod):
    depth, d, n6 = w_mod.shape
    nseq = c_all.shape[0]
    tn = 512
    return pl.pallas_call(
        _mod_kernel,
        grid=(depth, n6 // tn),
        in_specs=[
            pl.BlockSpec((nseq, d), lambda l, j: (0, 0)),
            pl.BlockSpec((None, d, tn), lambda l, j: (l, 0, j)),
            pl.BlockSpec((None, 1, tn), lambda l, j: (l, 0, j)),
        ],
        out_specs=pl.BlockSpec((None, nseq, tn), lambda l, j: (l, 0, j)),
        out_shape=jax.ShapeDtypeStruct((depth, nseq, n6), F32),
        compiler_params=_cparams(("parallel", "parallel")),
        name="modulation",
    )(c_all, w_mod, b_mod.reshape(depth, 1, n6))


def _ab_in_kernel(x_ref, g_ref, sh_ref, sc_ref, w_ref, qkg_ref, bd_ref,
                  p16_ref, ka_ref, va_ref, kb_ref, vb_ref):
    h = _rms(x_ref[...], g_ref[...]) * (1.0 + sc_ref[...]) + sh_ref[...]
    y = _dot(h.astype(BF16), w_ref[...])
    bd = bd_ref[...]
    wa = A_HEADS * 2 * HEAD_DIM

    def group_norm(seg, gi):
        hi, lo = _split2(seg * seg)
        ss = _dot(hi, bd) + _dot(lo, bd)
        return seg * lax.rsqrt(ss * (1.0 / HEAD_DIM) + EPS) * qkg_ref[gi:gi + 1, :]

    qa = group_norm(y[:, 0 * wa:1 * wa], 0)
    ka = group_norm(y[:, 1 * wa:2 * wa], 1)
    va = y[:, 2 * wa:3 * wa]
    qb = group_norm(y[:, 3 * wa:4 * wa], 2)
    kb = group_norm(y[:, 4 * wa:5 * wa], 3)
    vb = y[:, 5 * wa:6 * wa]
    scale = HEAD_DIM ** -0.5
    p16_ref[:, 0 * wa:1 * wa] = (qa * scale).astype(BF16)
    p16_ref[:, 1 * wa:2 * wa] = ka.astype(BF16)
    p16_ref[:, 2 * wa:3 * wa] = va.astype(BF16)
    p16_ref[:, 3 * wa:4 * wa] = (qb * scale).astype(BF16)
    p16_ref[:, 4 * wa:5 * wa] = kb.astype(BF16)
    p16_ref[:, 5 * wa:6 * wa] = vb.astype(BF16)
    ka_ref[...] = ka
    va_ref[...] = va
    kb_ref[...] = kb
    vb_ref[...] = vb


def _ab_in_proj(x_all, g, sh_e, sc_e, w16, qkg_t, bd, mod_idx):
    n, d = x_all.shape
    wa = A_HEADS * 2 * HEAD_DIM
    n_in = w16.shape[1]
    tok = lambda i: (i, 0)
    mod = lambda i: (mod_idx(i), 0, 0)
    return pl.pallas_call(
        _ab_in_kernel,
        grid=(n // TM,),
        in_specs=[
            pl.BlockSpec((TM, d), tok),
            _const_spec((1, d)),
            pl.BlockSpec((None, TM, d), mod),
            pl.BlockSpec((None, TM, d), mod),
            _const_spec((d, n_in)),
            _const_spec((4, wa)),
            _const_spec((wa, wa)),
        ],
        out_specs=[pl.BlockSpec((TM, n_in), tok)] + [pl.BlockSpec((TM, wa), tok)] * 4,
        out_shape=[jax.ShapeDtypeStruct((n, n_in), BF16)] + [jax.ShapeDtypeStruct((n, wa), F32)] * 4,
        compiler_params=_cparams(("parallel",)),
        name="ab_in_proj",
    )(x_all, g, sh_e, sc_e, w16, qkg_t, bd)


def _t5_bucket(rel):
    half = T5_BUCKETS // 2
    exact = half // 2
    n = jnp.abs(rel)
    large = exact + (jnp.log(jnp.maximum(n, 1).astype(F32) / exact)
                     / math.log(T5_MAX_DIST / exact) * (half - exact)).astype(I32)
    large = jnp.minimum(large, half - 1)
    return jnp.where(rel > 0, half, 0) + jnp.where(n < exact, n, large)


def _diff_bias_prompt(t5_bias, tq):
    i = jnp.arange(tq)[:, None]
    j = jnp.arange(tq)[None, :]
    diag = jnp.where(((j // CHUNK) <= (i // CHUNK))[..., None], t5_bias[_t5_bucket(j - i)], NEG)
    prev = t5_bias[_t5_bucket(j - i - tq)]
    tiles = jnp.transpose(jnp.stack([prev, diag]), (3, 0, 1, 2)).astype(F32)
    tiles = jnp.concatenate([tiles, tiles], axis=2)
    far = t5_bias[_t5_bucket(jnp.asarray(-T5_MAX_DIST))].astype(F32)
    return tiles, far


def _diff_bias_sample(t5_bias, past, t):
    qpos = past + jnp.arange(t)
    kpos = jnp.arange(past + t)
    rel = kpos[None, :] - qpos[:, None]
    vis = (kpos[None, :] // CHUNK) <= (qpos[:, None] // CHUNK)
    b = jnp.where(vis[..., None], t5_bias[_t5_bucket(rel)], NEG)
    b = jnp.transpose(b, (2, 0, 1)).astype(F32)
    b = jnp.concatenate([b, b], axis=1)
    return b[:, :, :past], b[:, :, past:]


def _band_bias_prompt(rel_bias, tq):
    nvar = BAND_PAST // tq + 1
    win = BAND_PAST + tq
    u = jnp.arange(nvar)[:, None, None]
    i = jnp.arange(tq)[None, :, None]
    j = jnp.arange(win)[None, None, :]
    qp = u * tq + i
    qc = qp // CHUNK
    kc = j // CHUNK
    valid = (kc <= qc) & (kc >= qc - BAND_CHUNKS)
    b = rel_bias[jnp.clip(j - qp, -REL_CLIP, REL_CLIP) + REL_CLIP]
    b = jnp.where(valid[..., None], b, NEG).astype(F32)
    b = jnp.transpose(b, (0, 3, 1, 2))
    return b.reshape(nvar, B_HEADS // 2, 2 * tq, win)


def _band_bias_sample(rel_bias, past, lb, t):
    qpos = past + jnp.arange(t)
    kpos = past - lb + jnp.arange(lb + t)
    band_lo = (past // CHUNK - BAND_CHUNKS) * CHUNK
    rel = jnp.clip(kpos[None, :] - qpos[:, None], -REL_CLIP, REL_CLIP) + REL_CLIP
    b = jnp.where((kpos >= band_lo)[None, :, None], rel_bias[rel], NEG).astype(F32)
    b = jnp.transpose(b, (2, 0, 1)).reshape(B_HEADS // 2, 2 * t, lb + t)
    return b[:, :, :lb], b[:, :, lb:]


def _stack_halves(q):
    lane = lax.broadcasted_iota(I32, q.shape, 1)
    zero = jnp.zeros_like(q)
    return jnp.concatenate([jnp.where(lane < HEAD_DIM, q, zero), jnp.where(lane >= HEAD_DIM, q, zero)], axis=0)


def _diff_finish(o1, o2, lam, hg, out_scale):
    o = o1 - lam * o2
    return (_rms(o, hg) * out_scale).astype(BF16)


def _band_finish(o):
    tq = o.shape[0] // 2
    lane = lax.broadcasted_iota(I32, (tq, o.shape[1]), 1)
    return jnp.where(lane < HEAD_DIM, o[:tq], o[tq:]).astype(BF16)


def _diff_prompt_kernel(far_ref, lam_ref, q_ref, k_ref, v_ref, bias_ref, hg_ref, o_ref,
                        q2_s, m_s, l_s, acc_s, *, tq, out_scale):
    h = pl.program_id(1)
    i = pl.program_id(2)
    q2_s[...] = _stack_halves(q_ref[...])
    m_s[...] = jnp.full(m_s.shape, NEG, F32)
    l_s[...] = jnp.zeros(l_s.shape, F32)
    acc_s[...] = jnp.zeros(acc_s.shape, F32)

    def step(j, bias):
        start = pl.multiple_of(j * tq, tq)
        kb = k_ref[pl.ds(start, tq), :]
        vb = v_ref[pl.ds(start, tq), :]
        s = _dot_nt(q2_s[...], kb) + bias
        m_prev = m_s[...]
        m_new = jnp.maximum(m_prev, jnp.max(s, axis=-1, keepdims=True))
        alpha = jnp.exp(m_prev - m_new)
        p = jnp.exp(s - m_new)
        l_s[...] = alpha * l_s[...] + jnp.sum(p, axis=-1, keepdims=True)
        acc_s[...] = alpha * acc_s[...] + _dot(p.astype(BF16), vb)
        m_s[...] = m_new

    far = far_ref[h]

    def far_step(j, carry):
        step(j, far)
        return carry

    lax.fori_loop(0, jnp.maximum(i - 1, 0), far_step, 0)

    @pl.when(i >= 1)
    def _():
        step(i - 1, bias_ref[0])

    step(i, bias_ref[1])
    o = acc_s[...] / l_s[...]
    o_ref[...] = _diff_finish(o[:tq], o[tq:], lam_ref[0], hg_ref[...], out_scale)


def _diff_attn_prompt(p16, tiles, far, lam, hg, bp, s, out_scale):
    n = p16.shape[0]
    tq = TQ_DIFF
    nq = s // tq
    wa = A_HEADS * LANES
    kern = functools.partial(_diff_prompt_kernel, tq=tq, out_scale=out_scale)
    return pl.pallas_call(
        kern,
        grid=(bp, A_HEADS, nq),
        in_specs=[
            pl.BlockSpec(memory_space=pltpu.SMEM),
            pl.BlockSpec(memory_space=pltpu.SMEM),
            pl.BlockSpec((tq, LANES), lambda b, h, i: (b * nq + i, h)),
            pl.BlockSpec((s, LANES), lambda b, h, i: (b, A_HEADS + h)),
            pl.BlockSpec((s, LANES), lambda b, h, i: (b, 2 * A_HEADS + h)),
            pl.BlockSpec((None, 2, 2 * tq, tq), lambda b, h, i: (h, 0, 0, 0)),
            _const_spec((1, LANES)),
        ],
        out_specs=pl.BlockSpec((tq, LANES), lambda b, h, i: (b * nq + i, h)),
        out_shape=jax.ShapeDtypeStruct((bp * s, wa), BF16),
        scratch_shapes=[
            pltpu.VMEM((2 * tq, LANES), BF16),
            pltpu.VMEM((2 * tq, 1), F32),
            pltpu.VMEM((2 * tq, 1), F32),
            pltpu.VMEM((2 * tq, LANES), F32),
        ],
        compiler_params=_cparams(("parallel", "parallel", "parallel")),
        name="diff_attn_prompt",
    )(far, lam, p16, p16, p16, tiles, hg)


def _band_prompt_kernel(q_ref, k_ref, v_ref, bias_ref, o_ref, *, tq, win):
    t = pl.program_id(2)
    start = pl.multiple_of(jnp.maximum(t * tq - BAND_PAST, 0), tq)
    kb = k_ref[pl.ds(start, win), :]
    vb = v_ref[pl.ds(start, win), :]
    s = _dot_nt(_stack_halves(q_ref[...]), kb) + bias_ref[...]
    m = jnp.max(s, axis=-1, keepdims=True)
    p = jnp.exp(s - m)
    l = jnp.sum(p, axis=-1, keepdims=True)
    o_ref[...] = _band_finish(_dot(p.astype(BF16), vb) / l)


def _band_attn_prompt(p16, bias, bp, s):
    n = p16.shape[0]
    tq = TQ_BAND
    nq = s // tq
    win = BAND_PAST + tq
    nvar = bias.shape[0]
    npair = B_HEADS // 2
    c0 = 3 * A_HEADS
    kern = functools.partial(_band_prompt_kernel, tq=tq, win=win)
    return pl.pallas_call(
        kern,
        grid=(bp, npair, nq),
        in_specs=[
            pl.BlockSpec((tq, LANES), lambda b, p, t: (b * nq + t, c0 + p)),
            pl.BlockSpec((s, LANES), lambda b, p, t: (b, c0 + npair + p)),
            pl.BlockSpec((s, LANES), lambda b, p, t: (b, c0 + 2 * npair + p)),
            pl.BlockSpec((None, None, 2 * tq, win), lambda b, p, t: (jnp.minimum(t, nvar - 1), p, 0, 0)),
        ],
        out_specs=pl.BlockSpec((tq, LANES), lambda b, p, t: (b * nq + t, p)),
        out_shape=jax.ShapeDtypeStruct((bp * s, npair * LANES), BF16),
        compiler_params=_cparams(("parallel", "parallel", "parallel")),
        name="band_attn_prompt",
    )(p16, p16, p16, bias)


def _sample_attn_kernel(lam_ref, q_ref, kc_ref, vc_ref, kn_ref, vn_ref, bc_ref, bn_ref, hg_ref, o_ref,
                        *, diff, out_scale):
    t = q_ref.shape[0]
    q2 = _stack_halves(q_ref[...])
    sc = _dot_nt(q2, kc_ref[...].astype(BF16)) + bc_ref[...]
    sn = _dot_nt(q2, kn_ref[...]) + bn_ref[...]
    m = jnp.maximum(jnp.max(sc, axis=-1, keepdims=True), jnp.max(sn, axis=-1, keepdims=True))
    pc = jnp.exp(sc - m)
    pn = jnp.exp(sn - m)
    l = jnp.sum(pc, axis=-1, keepdims=True) + jnp.sum(pn, axis=-1, keepdims=True)
    o = (_dot(pc.astype(BF16), vc_ref[...].astype(BF16)) + _dot(pn.astype(BF16), vn_ref[...])) / l
    if diff:
        o_ref[...] = _diff_finish(o[:t], o[t:], lam_ref[0], hg_ref[...], out_scale)
    else:
        o_ref[...] = _band_finish(o)


def _sample_attn(p16, cache_k, cache_v, bias_c, bias_n, lam, hg, np_rows, bs, t, diff, out_scale):
    past = cache_k.shape[1]
    ncol = cache_k.shape[2] // LANES
    row0 = np_rows // t
    if diff:
        qc, kc, vc = 0, A_HEADS, 2 * A_HEADS
    else:
        qc, kc, vc = 3 * A_HEADS, 3 * A_HEADS + ncol, 3 * A_HEADS + 2 * ncol
    kern = functools.partial(_sample_attn_kernel, diff=diff, out_scale=out_scale)
    return pl.pallas_call(
        kern,
        grid=(bs, ncol),
        in_specs=[
            pl.BlockSpec(memory_space=pltpu.SMEM),
            pl.BlockSpec((t, LANES), lambda b, h: (row0 + b, qc + h)),
            pl.BlockSpec((None, past, LANES), lambda b, h: (b, 0, h)),
            pl.BlockSpec((None, past, LANES), lambda b, h: (b, 0, h)),
            pl.BlockSpec((t, LANES), lambda b, h: (row0 + b, kc + h)),
            pl.BlockSpec((t, LANES), lambda b, h: (row0 + b, vc + h)),
            pl.BlockSpec((None, 2 * t, past), lambda b, h: (h, 0, 0)),
            pl.BlockSpec((None, 2 * t, t), lambda b, h: (h, 0, 0)),
            _const_spec((1, LANES)),
        ],
        out_specs=pl.BlockSpec((None, t, LANES), lambda b, h: (b, 0, h)),
        out_shape=jax.ShapeDtypeStruct((bs, t, ncol * LANES), BF16),
        compiler_params=_cparams(("parallel", "parallel")),
        name="diff_attn_sample" if diff else "band_attn_sample",
    )(lam, p16, cache_k, cache_v, p16, p16, bias_c, bias_n, hg)


def _out_proj_kernel(a0_ref, a1_ref, w_ref, x_ref, gt_ref, g2_ref, sh_ref, sc_ref, rwh_ref, rwl_ref,
                     xo_ref, h2_ref, lg_ref):
    half = a0_ref.shape[1]
    mix = _dot(a0_ref[...], w_ref[0:half, :]) + _dot(a1_ref[...], w_ref[half:2 * half, :])
    x = x_ref[...] + gt_ref[...] * mix
    xo_ref[...] = x
    h2 = _rms(x, g2_ref[...]) * (1.0 + sc_ref[...]) + sh_ref[...]
    h2_ref[...] = h2
    h_hi, h_lo = _split2(h2)
    rw_hi = rwh_ref[...]
    lg_ref[...] = _dot_nt(rw_hi, h_hi) + _dot_nt(rw_hi, h_lo) + _dot_nt(rwl_ref[...], h_hi)


def _out_proj(a0, c0, a1, c1, w16, x_all, gt_e, g2, sh_e, sc_e, rw_hi, rw_lo, mod_idx):
    n, d = x_all.shape
    half = d // 2
    ne = rw_hi.shape[0]
    tok = lambda i: (i, 0)
    mod = lambda i: (mod_idx(i), 0, 0)
    return pl.pallas_call(
        _out_proj_kernel,
        grid=(n // TM,),
        in_specs=[
            pl.BlockSpec((TM, half), lambda i: (i, c0)),
            pl.BlockSpec((TM, half), lambda i: (i, c1)),
            _const_spec((d, d)),
            pl.BlockSpec((TM, d), tok),
            pl.BlockSpec((None, TM, d), mod),
            _const_spec((1, d)),
            pl.BlockSpec((None, TM, d), mod),
            pl.BlockSpec((None, TM, d), mod),
            _const_spec((ne, d)),
            _const_spec((ne, d)),
        ],
        out_specs=[pl.BlockSpec((TM, d), tok), pl.BlockSpec((TM, d), tok), pl.BlockSpec((ne, TM), lambda i: (0, i))],
        out_shape=[jax.ShapeDtypeStruct((n, d), F32), jax.ShapeDtypeStruct((n, d), F32),
                   jax.ShapeDtypeStruct((ne, n), F32)],
        compiler_params=_cparams(("parallel",)),
        name="out_proj",
    )(a0, a1, w16, x_all, gt_e, g2, sh_e, sc_e, rw_hi, rw_lo)


def _route_kernel(lg_ref, rb_ref, tri_ref, idx_ref, gate_ref, rank_ref, cnt_ref, carry_s):
    @pl.when(pl.program_id(0) == 0)
    def _():
        carry_s[...] = jnp.zeros(carry_s.shape, F32)

    s = jax.nn.sigmoid(lg_ref[...])
    sb = s + rb_ref[...]
    row = lax.broadcasted_iota(I32, s.shape, 0).astype(F32)
    picks = []
    sel = jnp.zeros(s.shape, F32)
    for _ in range(TOP_K):
        m = jnp.max(sb, axis=0, keepdims=True)
        ik = jnp.min(jnp.where(sb == m, row, float(N_EXPERTS)), axis=0, keepdims=True)
        oh = row == ik
        picks.append((ik, oh, jnp.sum(jnp.where(oh, s, 0.0), axis=0, keepdims=True)))
        sel = sel + oh.astype(F32)
        sb = jnp.where(oh, -jnp.inf, sb)
    before = _dot(sel.astype(BF16), tri_ref[...]) + carry_s[...]
    gsum = functools.reduce(lambda a, b: a + b, [g for _, _, g in picks])
    for k, (ik, oh, g) in enumerate(picks):
        idx_ref[k:k + 1, :] = ik.astype(I32)
        gate_ref[k:k + 1, :] = g / gsum * ROUTE_SCALE
        rank_ref[k:k + 1, :] = jnp.sum(jnp.where(oh, before, 0.0), axis=0, keepdims=True).astype(I32)
    carry_s[...] = carry_s[...] + jnp.sum(sel, axis=1, keepdims=True)
    cnt_ref[...] = carry_s[...]


def _route(lg_t, rb, tri):
    ne, n = lg_t.shape
    tm = tri.shape[0]
    tokk = lambda i: (0, i)
    return pl.pallas_call(
        _route_kernel,
        grid=(n // tm,),
        in_specs=[pl.BlockSpec((ne, tm), tokk), _const_spec((ne, 1)), _const_spec((tm, tm))],
        out_specs=[pl.BlockSpec((TOP_K, tm), tokk)] * 3 + [_const_spec((ne, 1))],
        out_shape=[jax.ShapeDtypeStruct((TOP_K, n), I32), jax.ShapeDtypeStruct((TOP_K, n), F32),
                   jax.ShapeDtypeStruct((TOP_K, n), I32), jax.ShapeDtypeStruct((ne, 1), F32)],
        scratch_shapes=[pltpu.VMEM((ne, 1), F32)],
        compiler_params=_cparams(("arbitrary",)),
        name="moe_route",
    )(lg_t, rb, tri)


def _dispatch_kernel(pad_ref, dest_ref, h_ref, xs_ref, zero_s, sem, *, tm):
    i = pl.program_id(0)
    nrow = zero_s.shape[0]

    @pl.when(i == 0)
    def _():
        zero_s[...] = jnp.zeros(zero_s.shape, F32)

        def fill(start):
            cp = pltpu.make_async_copy(zero_s, xs_ref.at[pl.ds(pl.multiple_of(start, 8), nrow), :], sem)
            cp.start()
            cp.wait()

        def fill_pad(e, c):
            fill(pad_ref[e] // 8 * 8)
            return c

        lax.fori_loop(0, N_EXPERTS, fill_pad, 0)
        total = xs_ref.shape[0]
        tail = pad_ref[N_EXPERTS]

        def fill_tail(j, c):
            fill(jnp.minimum(tail + j * nrow, total - nrow))
            return c

        lax.fori_loop(0, (total - tail + nrow - 1) // nrow, fill_tail, 0)

    def issue_row(r, c):
        for k in range(TOP_K):
            d = dest_ref[r * TOP_K + k]
            pltpu.make_async_copy(h_ref.at[pl.ds(r, 1), :], xs_ref.at[pl.ds(d, 1), :], sem).start()
        return c

    lax.fori_loop(0, tm, issue_row, 0)
    pltpu.make_async_copy(xs_ref.at[pl.ds(0, tm * TOP_K), :], xs_ref.at[pl.ds(0, tm * TOP_K), :], sem).wait()


def _dispatch(pad_start, dest_flat, h2, cap):
    n, d = h2.shape
    tm = TM_MOVE
    kern = functools.partial(_dispatch_kernel, tm=tm)
    return pl.pallas_call(
        kern,
        grid_spec=pltpu.PrefetchScalarGridSpec(
            num_scalar_prefetch=1,
            grid=(n // tm,),
            in_specs=[
                pl.BlockSpec((tm * TOP_K,), lambda i, ps: (i,), memory_space=pltpu.SMEM),
                pl.BlockSpec((tm, d), lambda i, ps: (i, 0)),
            ],
            out_specs=pl.BlockSpec(memory_space=pl.ANY),
            scratch_shapes=[pltpu.VMEM((BM + 8, d), F32), pltpu.SemaphoreType.DMA(())],
        ),
        out_shape=jax.ShapeDtypeStruct((cap + BM + 8, d), F32),
        compiler_params=_cparams(("arbitrary",)),
        name="moe_dispatch",
    )(pad_start, dest_flat, h2)


def _experts_kernel(exp_ref, nused_ref, x_ref, w1_ref, w3_ref, w2_ref, y_ref):
    used = pl.program_id(0) < nused_ref[0]

    @pl.when(used)
    def _():
        xb = x_ref[...].astype(BF16)
        a = _dot(xb, w1_ref[...])
        hid = a * jax.nn.sigmoid(a) * _dot(xb, w3_ref[...])
        y_ref[...] = _dot(hid.astype(BF16), w2_ref[...])

    @pl.when(jnp.logical_not(used))
    def _():
        y_ref[...] = jnp.zeros(y_ref.shape, F32)


def _experts(blk_e, n_used, xs, w1, w3, w2):
    d = xs.shape[1]
    de = w1.shape[2]
    nb = blk_e.shape[0]
    return pl.pallas_call(
        _experts_kernel,
        grid_spec=pltpu.PrefetchScalarGridSpec(
            num_scalar_prefetch=2,
            grid=(nb,),
            in_specs=[
                pl.BlockSpec((BM, d), lambda i, e, u: (i, 0)),
                pl.BlockSpec((None, d, de), lambda i, e, u: (e[i], 0, 0)),
                pl.BlockSpec((None, d, de), lambda i, e, u: (e[i], 0, 0)),
                pl.BlockSpec((None, de, d), lambda i, e, u: (e[i], 0, 0)),
            ],
            out_specs=pl.BlockSpec((BM, d), lambda i, e, u: (i, 0)),
        ),
        out_shape=jax.ShapeDtypeStruct((nb * BM, d), F32),
        compiler_params=_cparams(("arbitrary",)),
        name="moe_experts",
    )(blk_e, n_used, xs, w1, w3, w2)


def _combine_kernel(dest_ref, ys_ref, g_ref, h_ref, x_ref, gt_ref, s1_ref, s3_ref, s2_ref, xo_ref,
                    buf_s, sem, *, tm):
    def issue_row(r, c):
        for k in range(TOP_K):
            d = dest_ref[r * TOP_K + k]
            pltpu.make_async_copy(ys_ref.at[pl.ds(d, 1), :], buf_s.at[k, pl.ds(r, 1), :], sem).start()
        return c

    lax.fori_loop(0, tm, issue_row, 0)
    hb = h_ref[...].astype(BF16)
    a = _dot(hb, s1_ref[...])
    shared = _dot((a * jax.nn.sigmoid(a) * _dot(hb, s3_ref[...])).astype(BF16), s2_ref[...])
    pltpu.make_async_copy(ys_ref.at[pl.ds(0, tm * TOP_K), :], ys_ref.at[pl.ds(0, tm * TOP_K), :], sem).wait()
    g = g_ref[...]
    routed = g[:, 0:1] * buf_s[0]
    for k in range(1, TOP_K):
        routed = routed + g[:, k:k + 1] * buf_s[k]
    xo_ref[...] = x_ref[...] + gt_ref[...] * (routed + shared)


def _combine(dest_flat, ys, gates, h2, x_all, gt_e, s1, s3, s2, mod_idx):
    n, d = x_all.shape
    ds_ = s1.shape[1]
    tm = TM_MOVE
    kern = functools.partial(_combine_kernel, tm=tm)
    tok = lambda i: (i, 0)
    return pl.pallas_call(
        kern,
        grid=(n // tm,),
        in_specs=[
            pl.BlockSpec((tm * TOP_K,), lambda i: (i,), memory_space=pltpu.SMEM),
            pl.BlockSpec(memory_space=pl.ANY),
            pl.BlockSpec((tm, TOP_K), tok),
            pl.BlockSpec((tm, d), tok),
            pl.BlockSpec((tm, d), tok),
            pl.BlockSpec((None, tm, d), lambda i: (mod_idx(i), 0, 0)),
            _const_spec((d, ds_)),
            _const_spec((d, ds_)),
            _const_spec((ds_, d)),
        ],
        out_specs=pl.BlockSpec((tm, d), tok),
        out_shape=jax.ShapeDtypeStruct((n, d), F32),
        scratch_shapes=[pltpu.VMEM((TOP_K, tm, d), F32), pltpu.SemaphoreType.DMA(())],
        compiler_params=_cparams(("arbitrary",)),
        name="moe_combine",
    )(dest_flat, ys, gates, h2, x_all, gt_e, s1, s3, s2)


def _moe(lg_t, h2, x_all, gt_e, rb, w1, w3, w2, s1, s3, s2, tri, mod_idx_move):
    n, d = h2.shape
    idx_t, gate_t, rank_t, cnt = _route(lg_t, rb.reshape(N_EXPERTS, 1).astype(F32), tri)
    counts = cnt[:, 0].astype(I32)
    padded = (counts + BM - 1) // BM * BM
    pend = jnp.cumsum(padded)
    pstart = pend - padded
    nb = (n * TOP_K + N_EXPERTS * (BM - 1) + BM - 1) // BM
    n_used = pend[-1] // BM
    blk_e = jnp.minimum(jnp.searchsorted(pend, jnp.arange(nb, dtype=I32) * BM, side='right'),
                        N_EXPERTS - 1).astype(I32)
    dest_t = jnp.sum(jnp.where(idx_t[None] == jnp.arange(N_EXPERTS, dtype=I32)[:, None, None],
                               pstart[:, None, None], 0), axis=0) + rank_t
    dest_flat = dest_t.T.reshape(-1)
    fill = jnp.concatenate([pstart + counts, pend[-1:]]).astype(I32)
    xs = _dispatch(fill, dest_flat, h2, nb * BM)
    ys = _experts(blk_e, n_used.reshape(1).astype(I32), xs, w1, w3, w2)
    return _combine(dest_flat, ys, gate_t.T, h2, x_all, gt_e, s1, s3, s2, mod_idx_move)


def _mlstm_pre_kernel(x_ref, g_ref, sh_ref, sc_ref, wxo_ref, wgh_ref, wgl_ref, bif_ref, cw_ref, cb_ref,
                      wq_ref, wk_ref, wv_ref, halo_ref,
                      q_ref, k_ref, v_ref, o_ref, xcv_ref, gates_ref, tail_ref, xcs_ref,
                      xpad_s, *, npt, tps, t_s):
    i = pl.program_id(0)
    tm, d = x_ref.shape
    h = _rms(x_ref[...], g_ref[...]) * (1.0 + sc_ref[...]) + sh_ref[...]
    h_hi, h_lo = _split2(h)
    y = _dot(h_hi, wxo_ref[...])
    xc = y[:, :d]
    o_ref[...] = y[:, d:].astype(BF16)
    wgh = wgh_ref[...]
    gp = _dot(h_hi, wgh) + _dot(h_hi, wgl_ref[...]) + _dot(h_lo, wgh) + bif_ref[...]
    lane = lax.broadcasted_iota(I32, gp.shape, 1)
    log_sig = jnp.minimum(gp, 0.0) - jnp.log(1.0 + jnp.exp(-jnp.abs(gp)))
    gates_ref[...] = jnp.where(lane >= C_HEADS, log_sig, gp)

    is_sample = i >= npt

    @pl.when(jnp.logical_or(i % tps == 0, is_sample))
    def _():
        xpad_s[0:8, :] = jnp.zeros((8, d), F32)

    xpad_s[8:, :] = xc
    row = (lax.broadcasted_iota(I32, (tm, 1), 0) & (t_s - 1)) + jnp.where(is_sample, 0, C_CONV)
    acc = xc * cw_ref[C_CONV - 1:C_CONV, :] + cb_ref[...]
    for j in range(1, C_CONV):
        prev = xpad_s[8 - j:8 - j + tm, :]
        prev = jnp.where(row < j, halo_ref[j - 1], prev)
        acc = acc + prev * cw_ref[C_CONV - 1 - j:C_CONV - j, :]
    xpad_s[0:8, :] = xc[tm - 8:, :]
    xconv = acc * jax.nn.sigmoid(acc)
    xcv16 = xconv.astype(BF16)
    xc16 = xc.astype(BF16)
    xcv_ref[...] = xcv16
    for hh in range(C_HEADS):
        cs = slice(hh * C_HEAD_DIM, (hh + 1) * C_HEAD_DIM)
        q_ref[:, cs] = _dot(xcv16[:, cs], wq_ref[hh]).astype(BF16)
        k_ref[:, cs] = (_dot(xcv16[:, cs], wk_ref[hh]) * C_HEAD_DIM ** -0.5).astype(BF16)
        v_ref[:, cs] = _dot(xc16[:, cs], wv_ref[hh]).astype(BF16)

    @pl.when(jnp.logical_not(is_sample))
    def _():
        tail_ref[...] = xc[tm - 8:, :]

    @pl.when(is_sample)
    def _():
        xcs_ref[...] = xc


def _mlstm_pre(x_all, g, sh_e, sc_e, wxo, wgh, wgl, bif, cw, cb, wq, wk, wv, halo, mod_idx, bp, s, t_s):
    n, d = x_all.shape
    npt = bp * s // TM
    tps = s // TM
    nst = n // TM - npt
    tok = lambda i: (i, 0)
    mod = lambda i: (mod_idx(i), 0, 0)
    kern = functools.partial(_mlstm_pre_kernel, npt=npt, tps=tps, t_s=t_s)
    b16 = jax.ShapeDtypeStruct((n, d), BF16)
    return pl.pallas_call(
        kern,
        grid=(n // TM,),
        in_specs=[
            pl.BlockSpec((TM, d), tok),
            _const_spec((1, d)),
            pl.BlockSpec((None, TM, d), mod),
            pl.BlockSpec((None, TM, d), mod),
            _const_spec((d, 2 * d)),
            _const_spec((d, LANES)),
            _const_spec((d, LANES)),
            _const_spec((1, LANES)),
            _const_spec((C_CONV, d)),
            _const_spec((1, d)),
            _const_spec((C_HEADS, C_HEAD_DIM, C_HEAD_DIM)),
            _const_spec((C_HEADS, C_HEAD_DIM, C_HEAD_DIM)),
            _const_spec((C_HEADS, C_HEAD_DIM, C_HEAD_DIM)),
            pl.BlockSpec((None, C_CONV - 1, TM, d), lambda i: (jnp.maximum(i - npt, 0), 0, 0, 0),
                         pipeline_mode=pl.Buffered(1)),
        ],
        out_specs=[pl.BlockSpec((TM, d), tok)] * 5 + [
            pl.BlockSpec((TM, LANES), tok),
            pl.BlockSpec((None, 8, d), lambda i: (jnp.minimum(i // tps, bp - 1), 0, 0)),
            pl.BlockSpec((TM, d), lambda i: (jnp.maximum(i - npt, 0), 0)),
        ],
        out_shape=[b16] * 5 + [
            jax.ShapeDtypeStruct((n, LANES), F32),
            jax.ShapeDtypeStruct((bp, 8, d), F32),
            jax.ShapeDtypeStruct((nst * TM, d), F32),
        ],
        scratch_shapes=[pltpu.VMEM((TM + 8, d), F32)],
        compiler_params=_cparams(("arbitrary",)),
        name="mlstm_pre",
    )(x_all, g, sh_e, sc_e, wxo, wgh, wgl, bif, cw, cb, wq, wk, wv, halo)


def _mlstm_scan_kernel(q_ref, k_ref, v_ref, o_ref, xcv_ref, gc_ref, gr_ref, tri_ref, trit_ref, hg_ref, sk_ref,
                       mem0_ref, nrm0_ref, mx0_ref,
                       a_ref, memo_ref, nrmo_ref, mxo_ref,
                       mem_s, nrm_s, mx_s, *, nc):
    c = pl.program_id(1)
    ln = q_ref.shape[0]

    @pl.when(c == 0)
    def _():
        mem_s[...] = mem0_ref[...]
        nrm_s[...] = nrm0_ref[...]
        mx_s[...] = mx0_ref[...]

    gc = gc_ref[...]
    gr = gr_ref[...]
    tri = tri_ref[...]
    trit = trit_ref[...]
    bc = functools.reduce(lambda a, b: a + b, [_dot(tri, p) for p in _split3(gc)])
    br = functools.reduce(lambda a, b: a + b, [_dot(p, trit) for p in _split3(gr)])
    causal = lax.broadcasted_iota(I32, (ln, ln), 1) <= lax.broadcasted_iota(I32, (ln, ln), 0)
    for h in range(C_HEADS):
        cs = slice(h * C_HEAD_DIM, (h + 1) * C_HEAD_DIM)
        b_col = bc[:, C_HEADS + h:C_HEADS + h + 1]
        ig_col = gc[:, h:h + 1]
        b_row = br[C_HEADS + h:C_HEADS + h + 1, :]
        ig_row = gr[h:h + 1, :]
        b_last = b_row[:, ln - 1:ln]
        mx = mx_s[h:h + 1, 0:1]
        logw = jnp.where(causal, b_col - b_row + ig_row, NEG)
        g = b_col + mx
        m_t = jnp.maximum(g, jnp.max(logw, axis=1, keepdims=True))
        w = jnp.exp(logw - m_t)
        inter = jnp.exp(g - m_t)
        qh = q_ref[:, cs]
        kh = k_ref[:, cs]
        vh = v_ref[:, cs]
        a = w * _dot_nt(qh, kh)
        mem = mem_s[h]
        nrm = nrm_s[h:h + 1, :]
        num = _dot(a.astype(BF16), vh) + inter * _dot(qh, mem.astype(BF16))
        den = jnp.sum(a, axis=1, keepdims=True) + inter * jnp.sum(qh.astype(F32) * nrm, axis=1, keepdims=True)
        hout = num / jnp.maximum(jnp.abs(den), jnp.exp(-m_t))
        logs = b_last - b_col + ig_col
        m_new = jnp.maximum(b_last + mx, jnp.max(logs, axis=0, keepdims=True))
        decay = jnp.exp(b_last + mx - m_new)
        kw = kh.astype(F32) * jnp.exp(logs - m_new)
        mem_s[h] = decay * mem + _dot_tn(kw.astype(BF16), vh)
        nrm_s[h:h + 1, :] = decay * nrm + jnp.sum(kw, axis=0, keepdims=True)
        mx_s[h:h + 1, :] = jnp.broadcast_to(m_new, (1, mx_s.shape[1]))
        hh = hout * jax.nn.sigmoid(o_ref[:, cs].astype(F32))
        a_ref[:, cs] = (_rms(hh, hg_ref[:, cs]) + sk_ref[:, cs] * xcv_ref[:, cs].astype(F32)).astype(BF16)

    @pl.when(c == nc - 1)
    def _():
        memo_ref[...] = mem_s[...]
        nrmo_ref[...] = nrm_s[...]
        mxo_ref[...] = mx_s[...]


def _mlstm_scan(q, k, v, o, xcv, gates, hg, sk, mem0, nrm0, mx0, row0, nb, nc, ln):
    d = q.shape[1]
    nrow = nb * nc * ln
    gsl = lax.slice_in_dim(gates, row0 * ln, row0 * ln + nrow, axis=0)[:, :16]
    gr = jnp.transpose(gsl.reshape(nb * nc, ln, 16), (0, 2, 1))
    r = jnp.arange(ln)
    tri = (r[None, :] <= r[:, None]).astype(BF16)
    chunk = lambda b, c: (row0 + b * nc + c, 0)
    seq4 = lambda b, c: (b, 0, 0, 0)
    seq3 = lambda b, c: (b, 0, 0)
    kern = functools.partial(_mlstm_scan_kernel, nc=nc)
    return pl.pallas_call(
        kern,
        grid=(nb, nc),
        in_specs=[pl.BlockSpec((ln, d), chunk)] * 5 + [
            pl.BlockSpec((ln, LANES), chunk),
            pl.BlockSpec((None, 16, ln), lambda b, c: (b * nc + c, 0, 0)),
            _const_spec((ln, ln)),
            _const_spec((ln, ln)),
            _const_spec((1, d)),
            _const_spec((1, d)),
            pl.BlockSpec((None, C_HEADS, C_HEAD_DIM, C_HEAD_DIM), seq4),
            pl.BlockSpec((None, 8, C_HEAD_DIM), seq3),
            pl.BlockSpec((None, 8, LANES), seq3),
        ],
        out_specs=[
            pl.BlockSpec((ln, d), lambda b, c: (b * nc + c, 0)),
            pl.BlockSpec((None, C_HEADS, C_HEAD_DIM, C_HEAD_DIM), seq4),
            pl.BlockSpec((None, 8, C_HEAD_DIM), seq3),
            pl.BlockSpec((None, 8, LANES), seq3),
        ],
        out_shape=[
            jax.ShapeDtypeStruct((nrow, d), BF16),
            jax.ShapeDtypeStruct((nb, C_HEADS, C_HEAD_DIM, C_HEAD_DIM), F32),
            jax.ShapeDtypeStruct((nb, 8, C_HEAD_DIM), F32),
            jax.ShapeDtypeStruct((nb, 8, LANES), F32),
        ],
        scratch_shapes=[
            pltpu.VMEM((C_HEADS, C_HEAD_DIM, C_HEAD_DIM), F32),
            pltpu.VMEM((8, C_HEAD_DIM), F32),
            pltpu.VMEM((8, LANES), F32),
        ],
        compiler_params=_cparams(("parallel", "arbitrary")),
        name="mlstm_scan",
    )(q, k, v, o, xcv, gates, gr, tri, tri.T, hg, sk, mem0, nrm0, mx0)


def _pad_heads(a, width):
    nb = a.shape[0]
    if a.ndim == 2:
        a = jnp.broadcast_to(a[:, :, None], (nb, C_HEADS, width))
    return jnp.concatenate([a.astype(F32), jnp.zeros((nb, 8 - C_HEADS, width), F32)], axis=1)


def kernel(x_prompt, x_sample, c_prompt, c_sample, cache_a_k, cache_a_v, cache_b_k, cache_b_v, state_c_mem, state_c_norm, state_c_max, state_c_conv, norm_g, w_mod, b_mod, t5_bias, ab_w_in, ab_qk_g, ab_lambda, ab_head_g, ab_rel_bias, ab_w_out, c_w_in, c_b_if, c_conv_w, c_conv_b, c_w_qkv, c_head_g, c_skip, c_w_out, router_w, router_b, exp_w1, exp_w3, exp_w2, sh_w1, sh_w3, sh_w2):
    bp, s, d = x_prompt.shape
    bs, t = x_sample.shape[:2]
    depth = norm_g.shape[0]
    past = cache_a_k.shape[2]
    lb = cache_b_k.shape[2]
    n_p = bp * s
    n_s = bs * t
    n_all = n_p + n_s
    assert s % TM == 0 and n_s % TM == 0 and TM % t == 0 and t & (t - 1) == 0
    assert s % TQ_DIFF == 0 and s % ML_CHUNK == 0 and s >= BAND_PAST + TQ_BAND and TQ_DIFF >= T5_MAX_DIST
    assert past % CHUNK == 0 and lb == BAND_PAST and t <= CHUNK and t >= C_CONV - 1 and s >= BAND_PAST

    def mod_idx_for(tile):
        rep = TM // tile
        npt = n_p // tile
        tps = s // tile
        return lambda i: jnp.where(i < npt, (i // tps) * rep, bp * rep + (i - npt))

    mod_idx = mod_idx_for(TM)
    mod_idx_move = mod_idx_for(TM_MOVE)

    def expand(vec, tile):
        vp = jnp.broadcast_to(vec[:bp, None, :], (bp, TM, d))
        vs = jnp.repeat(vec[bp:], t, axis=0).reshape(n_s // TM, TM, d)
        return jnp.concatenate([vp, vs], axis=0).reshape(-1, tile, d)

    x_all = jnp.concatenate([x_prompt.reshape(n_p, d), x_sample.reshape(n_s, d)], axis=0)
    c_all = jnp.concatenate([c_prompt, c_sample], axis=0)
    mods = _modulation(c_all, w_mod, b_mod)

    r = jnp.arange(TM)
    tri_route = (r[:, None] < r[None, :]).astype(BF16)
    hd = jnp.arange(A_HEADS * 2 * HEAD_DIM) // HEAD_DIM
    bd = (hd[:, None] == hd[None, :]).astype(BF16)

    leaves = {}
    for l in range(depth):
        m6 = [mods[l][:, j * d:(j + 1) * d] for j in range(6)]
        sh1, sc1, gt1, sh2, sc2 = [expand(v, TM) for v in m6[:5]]
        gt2 = expand(m6[5], TM_MOVE)
        i = l // 2
        if l % 2 == 0:
            lam_init = 0.8 - 0.6 * math.exp(-0.3 * l)
            lp = ab_lambda[i].astype(F32)
            lam = (jnp.exp(jnp.sum(lp[0] * lp[1])) - jnp.exp(jnp.sum(lp[2] * lp[3])) + lam_init).reshape(1)
            qkg_t = jnp.tile(ab_qk_g[i].astype(F32), (1, A_HEADS * 2))
            p16, ka, va, kb, vb = _ab_in_proj(x_all, norm_g[l, 0].reshape(1, d), sh1, sc1,
                                              ab_w_in[i].astype(BF16), qkg_t, bd, mod_idx)
            hg = ab_head_g[i].reshape(1, 2 * HEAD_DIM).astype(F32)
            out_scale = 1.0 - lam_init
            tiles, far = _diff_bias_prompt(t5_bias, TQ_DIFF)
            oa_p = _diff_attn_prompt(p16, tiles, far, lam, hg, bp, s, out_scale)
            ob_p = _band_attn_prompt(p16, _band_bias_prompt(ab_rel_bias[i], TQ_BAND), bp, s)
            dbc, dbn = _diff_bias_sample(t5_bias, past, t)
            oa_s = _sample_attn(p16, cache_a_k[i].reshape(bs, past, -1), cache_a_v[i].reshape(bs, past, -1),
                                dbc, dbn, lam, hg, n_p, bs, t, True, out_scale)
            bbc, bbn = _band_bias_sample(ab_rel_bias[i], past, lb, t)
            ob_s = _sample_attn(p16, cache_b_k[i].reshape(bs, lb, -1), cache_b_v[i].reshape(bs, lb, -1),
                                bbc, bbn, lam, hg, n_p, bs, t, False, out_scale)
            oa = jnp.concatenate([oa_p, oa_s.reshape(n_s, -1)], axis=0)
            ob = jnp.concatenate([ob_p, ob_s.reshape(n_s, -1)], axis=0)
            mix_in = (oa, 0, ob, 0)
            w_out16 = ab_w_out[i].astype(BF16)
            keep = min(BAND_PAST, s)
            leaves.setdefault('akp', []).append(ka[:n_p].reshape(bp, s, A_HEADS, 2, HEAD_DIM))
            leaves.setdefault('avp', []).append(va[:n_p].reshape(bp, s, A_HEADS, 2 * HEAD_DIM))
            leaves.setdefault('aks', []).append(ka[n_p:].reshape(bs, t, A_HEADS, 2, HEAD_DIM))
            leaves.setdefault('avs', []).append(va[n_p:].reshape(bs, t, A_HEADS, 2 * HEAD_DIM))
            leaves.setdefault('bkp', []).append(kb[:n_p].reshape(bp, s, B_HEADS, HEAD_DIM)[:, s - keep:])
            leaves.setdefault('bvp', []).append(vb[:n_p].reshape(bp, s, B_HEADS, HEAD_DIM)[:, s - keep:])
            leaves.setdefault('bks', []).append(kb[n_p:].reshape(bs, t, B_HEADS, HEAD_DIM))
            leaves.setdefault('bvs', []).append(vb[n_p:].reshape(bs, t, B_HEADS, HEAD_DIM))
        else:
            w_in = c_w_in[i]
            wg = jnp.pad(w_in[:, 2 * d:].astype(F32), ((0, 0), (0, LANES - 2 * C_HEADS)))
            wgh = wg.astype(BF16)
            wgl = (wg - wgh.astype(F32)).astype(BF16)
            bif = jnp.pad(c_b_if[i].astype(F32).reshape(1, 2 * C_HEADS), ((0, 0), (0, LANES - 2 * C_HEADS)))
            cprev = state_c_conv[i].astype(F32)
            planes = []
            for j in range(1, C_CONV):
                rows = jnp.concatenate([cprev[:, C_CONV - 1 - j:, :], jnp.zeros((bs, t - j, d), F32)], axis=1)
                planes.append(rows.reshape(n_s // TM, TM, d))
            halo = jnp.stack(planes, axis=1)
            wqkv = c_w_qkv[i].astype(BF16)
            q, k, v, o, xcv, gates, tail, xcs = _mlstm_pre(
                x_all, norm_g[l, 0].reshape(1, d), sh1, sc1, w_in[:, :2 * d].astype(BF16), wgh, wgl, bif,
                c_conv_w[i].astype(F32), c_conv_b[i].reshape(1, d).astype(F32), wqkv[0], wqkv[1], wqkv[2],
                halo, mod_idx, bp, s, t)
            hg = c_head_g[i].reshape(1, d).astype(F32)
            sk = c_skip[i].reshape(1, d).astype(F32)
            zm = jnp.zeros((bp, C_HEADS, C_HEAD_DIM, C_HEAD_DIM), F32)
            a_p, mem_p, nrm_p, mx_p = _mlstm_scan(
                q, k, v, o, xcv, gates, hg, sk, zm, jnp.zeros((bp, 8, C_HEAD_DIM), F32),
                jnp.zeros((bp, 8, LANES), F32), 0, bp, s // ML_CHUNK, ML_CHUNK)
            a_s, mem_s, nrm_s, mx_s = _mlstm_scan(
                q, k, v, o, xcv, gates, hg, sk, state_c_mem[i].astype(F32),
                _pad_heads(state_c_norm[i], C_HEAD_DIM), _pad_heads(state_c_max[i], LANES),
                n_p // t, bs, 1, t)
            a_all = jnp.concatenate([a_p, a_s], axis=0)
            mix_in = (a_all, 0, a_all, 1)
            w_out16 = c_w_out[i].astype(BF16)
            leaves.setdefault('memp', []).append(mem_p)
            leaves.setdefault('normp', []).append(nrm_p[:, :C_HEADS])
            leaves.setdefault('maxp', []).append(mx_p[:, :C_HEADS, 0])
            leaves.setdefault('convp', []).append(tail[:, 8 - (C_CONV - 1):])
            leaves.setdefault('mems', []).append(mem_s)
            leaves.setdefault('norms', []).append(nrm_s[:, :C_HEADS])
            leaves.setdefault('maxs', []).append(mx_s[:, :C_HEADS, 0])
            leaves.setdefault('convs', []).append(xcs.reshape(bs, t, d)[:, t - (C_CONV - 1):])
        rw_t = router_w[l].astype(F32).T
        rw_hi = rw_t.astype(BF16)
        rw_lo = (rw_t - rw_hi.astype(F32)).astype(BF16)
        x_all, h2, lg_t = _out_proj(mix_in[0], mix_in[1], mix_in[2], mix_in[3], w_out16, x_all, gt1,
                                    norm_g[l, 1].reshape(1, d), sh2, sc2, rw_hi, rw_lo, mod_idx)
        x_all = _moe(lg_t, h2, x_all, gt2, router_b[l], exp_w1[l].astype(BF16), exp_w3[l].astype(BF16),
                     exp_w2[l].astype(BF16), sh_w1[l].astype(BF16), sh_w3[l].astype(BF16),
                     sh_w2[l].astype(BF16), tri_route, mod_idx_move)

    order = ['akp', 'avp', 'aks', 'avs', 'bkp', 'bvp', 'bks', 'bvs',
             'memp', 'normp', 'maxp', 'convp', 'mems', 'norms', 'maxs', 'convs']
    return (x_all[:n_p].reshape(bp, s, d), x_all[n_p:].reshape(bs, t, d)) + tuple(
        jnp.stack(leaves[name]) for name in order)
```

```python
import functools
import math

import jax
import jax.numpy as jnp
from jax import lax
from jax.experimental import pallas as pl
from jax.experimental.pallas import tpu as pltpu

F32 = jnp.float32
BF16 = jnp.bfloat16
I32 = jnp.int32

EPS = 1e-6
NEG = -1e30
CHUNK = 64
HEAD_DIM = 64
A_HEADS = 4
B_HEADS = 8
BAND_CHUNKS = 8
BAND_PAST = BAND_CHUNKS * CHUNK
REL_CLIP = 128
T5_BUCKETS = 32
T5_MAX_DIST = 128
C_HEADS = 4
C_HEAD_DIM = 256
C_CONV = 4
N_EXPERTS = 64
TOP_K = 8
ROUTE_SCALE = 2.5

LANES = 128
TM = 512
TQ_DIFF = 256
TQ_BAND = 128
ML_CHUNK = 256
BM = 256
TM_MOVE = 256
VMEM_LIMIT = 56 * 1024 * 1024


def _cparams(sem):
    return pltpu.CompilerParams(dimension_semantics=sem, vmem_limit_bytes=VMEM_LIMIT)


def _dot(a, b):
    return jnp.dot(a, b, preferred_element_type=F32)


def _dot_nt(a, b):
    return lax.dot_general(a, b, (((1,), (1,)), ((), ())), preferred_element_type=F32)


def _dot_tn(a, b):
    return lax.dot_general(a, b, (((0,), (0,)), ((), ())), preferred_element_type=F32)


def _split2(x):
    hi = x.astype(BF16)
    lo = (x - hi.astype(F32)).astype(BF16)
    return hi, lo


def _split3(x):
    p0 = x.astype(BF16)
    r1 = x - p0.astype(F32)
    p1 = r1.astype(BF16)
    p2 = (r1 - p1.astype(F32)).astype(BF16)
    return p0, p1, p2


def _rms(x, g):
    return x * lax.rsqrt(jnp.mean(x * x, axis=-1, keepdims=True) + EPS) * g


def _mod_specs(tile, d, npt, tps, bp):
    return [pl.BlockSpec((None, 1, d), lambda i: (jnp.minimum(i // tps, bp - 1), 0, 0)),
            pl.BlockSpec((tile, d), lambda i: (jnp.maximum(i - npt, 0), 0), pipeline_mode=pl.Buffered(1))]


def _sample_rows(i, npt, tm):
    return lax.broadcasted_iota(I32, (tm, 1), 0) >= jnp.where(i >= npt, 0, tm)


def _pick(is_s, vp_ref, vs_ref):
    return jnp.where(is_s, vs_ref[...], vp_ref[...])


def _const_spec(shape):
    nd = len(shape)
    return pl.BlockSpec(shape, lambda *_: (0,) * nd, pipeline_mode=pl.Buffered(1))


def _mod_kernel(c_ref, w_ref, b_ref, o_ref):
    c = c_ref[...]
    a_hi, a_lo = _split2(c * jax.nn.sigmoid(c))
    w_hi, w_lo = _split2(w_ref[...])
    o_ref[...] = _dot(a_hi, w_hi) + _dot(a_hi, w_lo) + _dot(a_lo, w_hi) + b_ref[...]


def _modulation(c_all, w_mod, b_mod):
    depth, d, n6 = w_mod.shape
    nseq = c_all.shape[0]
    tn = 512
    return pl.pallas_call(
        _mod_kernel,
        grid=(depth, n6 // tn),
        in_specs=[
            pl.BlockSpec((nseq, d), lambda l, j: (0, 0)),
            pl.BlockSpec((None, d, tn), lambda l, j: (l, 0, j)),
            pl.BlockSpec((None, 1, tn), lambda l, j: (l, 0, j)),
        ],
        out_specs=pl.BlockSpec((None, nseq, tn), lambda l, j: (l, 0, j)),
        out_shape=jax.ShapeDtypeStruct((depth, nseq, n6), F32),
        compiler_params=_cparams(("parallel", "parallel")),
        name="modulation",
    )(c_all, w_mod, b_mod.reshape(depth, 1, n6))


def _ab_in_kernel(x_ref, g_ref, shp_ref, shs_ref, scp_ref, scs_ref, w_ref, qkg_ref, bd_ref,
                  p16_ref, ka_ref, va_ref, kb_ref, vb_ref, *, npt):
    is_s = _sample_rows(pl.program_id(0), npt, x_ref.shape[0])
    h = _rms(x_ref[...], g_ref[...]) * (1.0 + _pick(is_s, scp_ref, scs_ref)) + _pick(is_s, shp_ref, shs_ref)
    y = _dot(h.astype(BF16), w_ref[...])
    bd = bd_ref[...]
    wa = A_HEADS * 2 * HEAD_DIM

    def group_norm(seg, gi):
        hi, lo = _split2(seg * seg)
        ss = _dot(hi, bd) + _dot(lo, bd)
        return seg * lax.rsqrt(ss * (1.0 / HEAD_DIM) + EPS) * qkg_ref[gi:gi + 1, :]

    qa = group_norm(y[:, 0 * wa:1 * wa], 0)
    ka = group_norm(y[:, 1 * wa:2 * wa], 1)
    va = y[:, 2 * wa:3 * wa]
    qb = group_norm(y[:, 3 * wa:4 * wa], 2)
    kb = group_norm(y[:, 4 * wa:5 * wa], 3)
    vb = y[:, 5 * wa:6 * wa]
    scale = HEAD_DIM ** -0.5
    p16_ref[:, 0 * wa:1 * wa] = (qa * scale).astype(BF16)
    p16_ref[:, 1 * wa:2 * wa] = ka.astype(BF16)
    p16_ref[:, 2 * wa:3 * wa] = va.astype(BF16)
    p16_ref[:, 3 * wa:4 * wa] = (qb * scale).astype(BF16)
    p16_ref[:, 4 * wa:5 * wa] = kb.astype(BF16)
    p16_ref[:, 5 * wa:6 * wa] = vb.astype(BF16)
    ka_ref[...] = ka
    va_ref[...] = va
    kb_ref[...] = kb
    vb_ref[...] = vb


def _ab_in_proj(x_all, g, sh, sc, w16, qkg_t, bd, bp, s):
    n, d = x_all.shape
    wa = A_HEADS * 2 * HEAD_DIM
    n_in = w16.shape[1]
    tok = lambda i: (i, 0)
    npt = bp * s // TM
    mod = _mod_specs(TM, d, npt, s // TM, bp)
    return pl.pallas_call(
        functools.partial(_ab_in_kernel, npt=npt),
        grid=(n // TM,),
        in_specs=[
            pl.BlockSpec((TM, d), tok),
            _const_spec((1, d)),
            *mod,
            *mod,
            _const_spec((d, n_in)),
            _const_spec((4, wa)),
            _const_spec((wa, wa)),
        ],
        out_specs=[pl.BlockSpec((TM, n_in), tok)] + [pl.BlockSpec((TM, wa), tok)] * 4,
        out_shape=[jax.ShapeDtypeStruct((n, n_in), BF16)] + [jax.ShapeDtypeStruct((n, wa), F32)] * 4,
        compiler_params=_cparams(("parallel",)),
        name="ab_in_proj",
    )(x_all, g, *sh, *sc, w16, qkg_t, bd)


def _t5_bucket(rel):
    half = T5_BUCKETS // 2
    exact = half // 2
    n = jnp.abs(rel)
    large = exact + (jnp.log(jnp.maximum(n, 1).astype(F32) / exact)
                     / math.log(T5_MAX_DIST / exact) * (half - exact)).astype(I32)
    large = jnp.minimum(large, half - 1)
    return jnp.where(rel > 0, half, 0) + jnp.where(n < exact, n, large)


def _lookup(table, idx):
    onehot = (idx[..., None] == jnp.arange(table.shape[0], dtype=I32)).astype(F32)
    return jnp.einsum('...n,nh->...h', onehot, table.astype(F32), precision=lax.Precision.HIGHEST)


def _diff_bias_prompt(t5_bias, tq):
    i = jnp.arange(tq)[None, :]
    j = jnp.arange(tq)[:, None]
    diag = jnp.where(((j // CHUNK) <= (i // CHUNK))[..., None], _lookup(t5_bias, _t5_bucket(j - i)), NEG)
    prev = _lookup(t5_bias, _t5_bucket(j - i - tq))
    first = jnp.concatenate([diag, jnp.full_like(diag, NEG)], axis=0)
    later = jnp.concatenate([prev, diag], axis=0)
    tiles = jnp.transpose(jnp.stack([first, later]), (3, 0, 1, 2))
    tiles = jnp.concatenate([tiles, tiles], axis=3)
    far = _lookup(t5_bias, _t5_bucket(jnp.full((1,), -T5_MAX_DIST, I32)))[0]
    return tiles, far


def _diff_bias_sample(t5_bias, past, t):
    qpos = past + jnp.arange(t)
    kpos = jnp.arange(past + t)
    rel = kpos[None, :] - qpos[:, None]
    vis = (kpos[None, :] // CHUNK) <= (qpos[:, None] // CHUNK)
    b = jnp.where(vis[..., None], _lookup(t5_bias, _t5_bucket(rel)), NEG)
    b = jnp.transpose(b, (2, 0, 1))
    b = jnp.concatenate([b, b], axis=1)
    return b[:, :, :past], b[:, :, past:]


def _band_bias_prompt(rel_bias, tq):
    nvar = BAND_PAST // tq + 1
    win = BAND_PAST + tq
    u = jnp.arange(nvar)[:, None, None]
    i = jnp.arange(tq)[None, :, None]
    j = jnp.arange(win)[None, None, :]
    qp = u * tq + i
    qc = qp // CHUNK
    kc = j // CHUNK
    valid = (kc <= qc) & (kc >= qc - BAND_CHUNKS)
    b = _lookup(rel_bias, jnp.clip(j - qp, -REL_CLIP, REL_CLIP) + REL_CLIP)
    b = jnp.where(valid[..., None], b, NEG)
    b = jnp.transpose(b, (0, 3, 1, 2))
    return b.reshape(nvar, B_HEADS // 2, 2 * tq, win)


def _band_bias_sample(rel_bias, past, lb, t):
    qpos = past + jnp.arange(t)
    kpos = past - lb + jnp.arange(lb + t)
    band_lo = (past // CHUNK - BAND_CHUNKS) * CHUNK
    rel = jnp.clip(kpos[None, :] - qpos[:, None], -REL_CLIP, REL_CLIP) + REL_CLIP
    b = jnp.where((kpos >= band_lo)[None, :, None], _lookup(rel_bias, rel), NEG)
    b = jnp.transpose(b, (2, 0, 1)).reshape(B_HEADS // 2, 2 * t, lb + t)
    return b[:, :, :lb], b[:, :, lb:]


def _stack_halves(q):
    lane = lax.broadcasted_iota(I32, q.shape, 1)
    zero = jnp.zeros_like(q)
    return jnp.concatenate([jnp.where(lane < HEAD_DIM, q, zero), jnp.where(lane >= HEAD_DIM, q, zero)], axis=0)


def _diff_finish(o1, o2, lam, hg, out_scale):
    o = o1 - lam * o2
    return (_rms(o, hg) * out_scale).astype(BF16)


def _band_finish(o):
    tq = o.shape[0] // 2
    lane = lax.broadcasted_iota(I32, (tq, o.shape[1]), 1)
    return jnp.where(lane < HEAD_DIM, o[:tq], o[tq:]).astype(BF16)


def _diff_prompt_kernel(far_ref, lam_ref, q_ref, k_ref, v_ref, bias_ref, hg_ref, o_ref,
                        q2t_s, vt_s, m_s, l_s, acc_s, *, tq, out_scale):
    h = pl.program_id(1)
    i = pl.program_id(2)

    @pl.when(i == 0)
    def _():
        for jj in range(vt_s.shape[0]):
            vt_s[jj] = v_ref[jj * tq:(jj + 1) * tq, :].astype(F32).T.astype(BF16)

    qt = q_ref[...].astype(F32).T.astype(BF16)
    row = lax.broadcasted_iota(I32, qt.shape, 0)
    zero = jnp.zeros_like(qt)
    q2t_s[:, 0:tq] = jnp.where(row < HEAD_DIM, qt, zero)
    q2t_s[:, tq:2 * tq] = jnp.where(row >= HEAD_DIM, qt, zero)
    m_s[...] = jnp.full(m_s.shape, NEG, F32)
    l_s[...] = jnp.zeros(l_s.shape, F32)
    acc_s[...] = jnp.zeros(acc_s.shape, F32)

    def step(jb, nblk, bias, shift):
        kb = k_ref[pl.ds(pl.multiple_of(jb * tq, tq), nblk * tq), :]
        s = _dot(kb, q2t_s[...])
        if bias is not None:
            s = s + bias
        cmax = jnp.max(s, axis=0, keepdims=True)
        if shift is not None:
            cmax = cmax + shift
        m_prev = m_s[...]
        m_new = jnp.maximum(m_prev, cmax)
        alpha = jnp.exp(m_prev - m_new)
        p = jnp.exp(s - (m_new if shift is None else m_new - shift))
        l_s[...] = alpha * l_s[...] + jnp.sum(p, axis=0, keepdims=True)
        pb = p.astype(BF16)
        pv = _dot(vt_s[jb], pb[0:tq])
        for u in range(1, nblk):
            pv = pv + _dot(vt_s[jb + u], pb[u * tq:(u + 1) * tq])
        acc_s[...] = alpha * acc_s[...] + pv
        m_s[...] = m_new

    far = far_ref[h]
    nfar = jnp.maximum(i - 1, 0)

    def far_pair(j, carry):
        step(2 * j, 2, None, far)
        return carry

    lax.fori_loop(0, nfar // 2, far_pair, 0)

    @pl.when(nfar % 2 == 1)
    def _():
        step(nfar - 1, 1, None, far)

    step(nfar, 2, bias_ref[...], None)
    o = acc_s[...] / l_s[...]
    od = o[:, 0:tq] - lam_ref[0] * o[:, tq:2 * tq]
    on = od * lax.rsqrt(jnp.mean(od * od, axis=0, keepdims=True) + EPS) * hg_ref[...] * out_scale
    o_ref[...] = on.T.astype(BF16)


def _diff_attn_prompt(p16, tiles, far, lam, hg_col, bp, s, out_scale):
    tq = TQ_DIFF
    nq = s // tq
    wa = A_HEADS * LANES
    kern = functools.partial(_diff_prompt_kernel, tq=tq, out_scale=out_scale)
    return pl.pallas_call(
        kern,
        grid=(bp, A_HEADS, nq),
        in_specs=[
            pl.BlockSpec(memory_space=pltpu.SMEM),
            pl.BlockSpec(memory_space=pltpu.SMEM),
            pl.BlockSpec((tq, LANES), lambda b, h, i: (b * nq + i, h)),
            pl.BlockSpec((s, LANES), lambda b, h, i: (b, A_HEADS + h)),
            pl.BlockSpec((s, LANES), lambda b, h, i: (b, 2 * A_HEADS + h)),
            pl.BlockSpec((None, None, 2 * tq, 2 * tq), lambda b, h, i: (h, jnp.minimum(i, 1), 0, 0)),
            _const_spec((LANES, 1)),
        ],
        out_specs=pl.BlockSpec((tq, LANES), lambda b, h, i: (b * nq + i, h)),
        out_shape=jax.ShapeDtypeStruct((bp * s, wa), BF16),
        scratch_shapes=[
            pltpu.VMEM((LANES, 2 * tq), BF16),
            pltpu.VMEM((nq, LANES, tq), BF16),
            pltpu.VMEM((1, 2 * tq), F32),
            pltpu.VMEM((1, 2 * tq), F32),
            pltpu.VMEM((LANES, 2 * tq), F32),
        ],
        compiler_params=_cparams(("parallel", "parallel", "arbitrary")),
        name="diff_attn_prompt",
    )(far, lam, p16, p16, p16, tiles, hg_col)


def _band_prompt_kernel(q_ref, k_ref, v_ref, bias_ref, o_ref, *, tq, win):
    t = pl.program_id(2)
    start = pl.multiple_of(jnp.maximum(t * tq - BAND_PAST, 0), tq)
    kb = k_ref[pl.ds(start, win), :]
    vb = v_ref[pl.ds(start, win), :]
    s = _dot_nt(_stack_halves(q_ref[...]), kb) + bias_ref[...]
    m = jnp.max(s, axis=-1, keepdims=True)
    p = jnp.exp(s - m)
    l = jnp.sum(p, axis=-1, keepdims=True)
    o_ref[...] = _band_finish(_dot(p.astype(BF16), vb) / l)


def _band_attn_prompt(p16, bias, bp, s):
    tq = TQ_BAND
    nq = s // tq
    win = BAND_PAST + tq
    nvar = bias.shape[0]
    npair = B_HEADS // 2
    c0 = 3 * A_HEADS
    kern = functools.partial(_band_prompt_kernel, tq=tq, win=win)
    return pl.pallas_call(
        kern,
        grid=(bp, npair, nq),
        in_specs=[
            pl.BlockSpec((tq, LANES), lambda b, p, t: (b * nq + t, c0 + p)),
            pl.BlockSpec((s, LANES), lambda b, p, t: (b, c0 + npair + p)),
            pl.BlockSpec((s, LANES), lambda b, p, t: (b, c0 + 2 * npair + p)),
            pl.BlockSpec((None, None, 2 * tq, win), lambda b, p, t: (jnp.minimum(t, nvar - 1), p, 0, 0)),
        ],
        out_specs=pl.BlockSpec((tq, LANES), lambda b, p, t: (b * nq + t, p)),
        out_shape=jax.ShapeDtypeStruct((bp * s, npair * LANES), BF16),
        compiler_params=_cparams(("parallel", "parallel", "parallel")),
        name="band_attn_prompt",
    )(p16, p16, p16, bias)


def _sample_attn_kernel(lam_ref, q_ref, kc_ref, vc_ref, kn_ref, vn_ref, bc_ref, bn_ref, hg_ref, o_ref,
                        *, diff, out_scale):
    t = q_ref.shape[0]
    q2 = _stack_halves(q_ref[...])
    sc = _dot_nt(q2, kc_ref[...].astype(BF16)) + bc_ref[...]
    sn = _dot_nt(q2, kn_ref[...]) + bn_ref[...]
    m = jnp.maximum(jnp.max(sc, axis=-1, keepdims=True), jnp.max(sn, axis=-1, keepdims=True))
    pc = jnp.exp(sc - m)
    pn = jnp.exp(sn - m)
    l = jnp.sum(pc, axis=-1, keepdims=True) + jnp.sum(pn, axis=-1, keepdims=True)
    o = (_dot(pc.astype(BF16), vc_ref[...].astype(BF16)) + _dot(pn.astype(BF16), vn_ref[...])) / l
    if diff:
        o_ref[...] = _diff_finish(o[:t], o[t:], lam_ref[0], hg_ref[...], out_scale)
    else:
        o_ref[...] = _band_finish(o)


def _sample_attn(p16, cache_k, cache_v, bias_c, bias_n, lam, hg, np_rows, bs, t, diff, out_scale):
    past = cache_k.shape[1]
    ncol = cache_k.shape[2] // LANES
    row0 = np_rows // t
    if diff:
        qc, kc, vc = 0, A_HEADS, 2 * A_HEADS
    else:
        qc, kc, vc = 3 * A_HEADS, 3 * A_HEADS + ncol, 3 * A_HEADS + 2 * ncol
    kern = functools.partial(_sample_attn_kernel, diff=diff, out_scale=out_scale)
    return pl.pallas_call(
        kern,
        grid=(bs, ncol),
        in_specs=[
            pl.BlockSpec(memory_space=pltpu.SMEM),
            pl.BlockSpec((t, LANES), lambda b, h: (row0 + b, qc + h)),
            pl.BlockSpec((None, past, LANES), lambda b, h: (b, 0, h)),
            pl.BlockSpec((None, past, LANES), lambda b, h: (b, 0, h)),
            pl.BlockSpec((t, LANES), lambda b, h: (row0 + b, kc + h)),
            pl.BlockSpec((t, LANES), lambda b, h: (row0 + b, vc + h)),
            pl.BlockSpec((None, 2 * t, past), lambda b, h: (h, 0, 0)),
            pl.BlockSpec((None, 2 * t, t), lambda b, h: (h, 0, 0)),
            _const_spec((1, LANES)),
        ],
        out_specs=pl.BlockSpec((None, t, LANES), lambda b, h: (b, 0, h)),
        out_shape=jax.ShapeDtypeStruct((bs, t, ncol * LANES), BF16),
        compiler_params=_cparams(("parallel", "parallel")),
        name="diff_attn_sample" if diff else "band_attn_sample",
    )(lam, p16, cache_k, cache_v, p16, p16, bias_c, bias_n, hg)


def _out_proj_kernel(a0_ref, a1_ref, w_ref, x_ref, gtp_ref, gts_ref, g2_ref, shp_ref, shs_ref, scp_ref, scs_ref,
                     rwh_ref, rwl_ref, xo_ref, h2_ref, lg_ref, *, npt):
    half = a0_ref.shape[1]
    is_s = _sample_rows(pl.program_id(0), npt, x_ref.shape[0])
    mix = _dot(a0_ref[...], w_ref[0:half, :]) + _dot(a1_ref[...], w_ref[half:2 * half, :])
    x = x_ref[...] + _pick(is_s, gtp_ref, gts_ref) * mix
    xo_ref[...] = x
    h2 = _rms(x, g2_ref[...]) * (1.0 + _pick(is_s, scp_ref, scs_ref)) + _pick(is_s, shp_ref, shs_ref)
    h2_ref[...] = h2
    h_hi, h_lo = _split2(h2)
    rw_hi = rwh_ref[...]
    lg_ref[...] = _dot_nt(rw_hi, h_hi) + _dot_nt(rw_hi, h_lo) + _dot_nt(rwl_ref[...], h_hi)


def _out_proj(a0, c0, a1, c1, w16, x_all, gt, g2, sh, sc, rw_hi, rw_lo, bp, s):
    n, d = x_all.shape
    half = d // 2
    ne = rw_hi.shape[0]
    tok = lambda i: (i, 0)
    npt = bp * s // TM
    mod = _mod_specs(TM, d, npt, s // TM, bp)
    return pl.pallas_call(
        functools.partial(_out_proj_kernel, npt=npt),
        grid=(n // TM,),
        in_specs=[
            pl.BlockSpec((TM, half), lambda i: (i, c0)),
            pl.BlockSpec((TM, half), lambda i: (i, c1)),
            _const_spec((d, d)),
            pl.BlockSpec((TM, d), tok),
            *mod,
            _const_spec((1, d)),
            *mod,
            *mod,
            _const_spec((ne, d)),
            _const_spec((ne, d)),
        ],
        out_specs=[pl.BlockSpec((TM, d), tok), pl.BlockSpec((TM, d), tok), pl.BlockSpec((ne, TM), lambda i: (0, i))],
        out_shape=[jax.ShapeDtypeStruct((n, d), F32), jax.ShapeDtypeStruct((n, d), F32),
                   jax.ShapeDtypeStruct((ne, n), F32)],
        compiler_params=_cparams(("parallel",)),
        name="out_proj",
    )(a0, a1, w16, x_all, *gt, g2, *sh, *sc, rw_hi, rw_lo)


def _route_kernel(lg_ref, rb_ref, tri_ref, idx_ref, gate_ref, rank_ref, cnt_ref, carry_s):
    @pl.when(pl.program_id(0) == 0)
    def _():
        carry_s[...] = jnp.zeros(carry_s.shape, F32)

    s = jax.nn.sigmoid(lg_ref[...])
    sb = s + rb_ref[...]
    row = lax.broadcasted_iota(I32, s.shape, 0).astype(F32)
    picks = []
    sel = jnp.zeros(s.shape, F32)
    for _ in range(TOP_K):
        m = jnp.max(sb, axis=0, keepdims=True)
        ik = jnp.min(jnp.where(sb == m, row, float(N_EXPERTS)), axis=0, keepdims=True)
        oh = row == ik
        picks.append((ik, oh, jnp.sum(jnp.where(oh, s, 0.0), axis=0, keepdims=True)))
        sel = sel + oh.astype(F32)
        sb = jnp.where(oh, -jnp.inf, sb)
    before = _dot(sel.astype(BF16), tri_ref[...]) + carry_s[...]
    gsum = functools.reduce(lambda a, b: a + b, [g for _, _, g in picks])
    for k, (ik, oh, g) in enumerate(picks):
        idx_ref[k:k + 1, :] = ik.astype(I32)
        gate_ref[k:k + 1, :] = g / gsum * ROUTE_SCALE
        rank_ref[k:k + 1, :] = jnp.sum(jnp.where(oh, before, 0.0), axis=0, keepdims=True).astype(I32)
    carry_s[...] = carry_s[...] + jnp.sum(sel, axis=1, keepdims=True)
    cnt_ref[...] = carry_s[...]


def _route(lg_t, rb, tri):
    ne, n = lg_t.shape
    tm = tri.shape[0]
    tokk = lambda i: (0, i)
    return pl.pallas_call(
        _route_kernel,
        grid=(n // tm,),
        in_specs=[pl.BlockSpec((ne, tm), tokk), _const_spec((ne, 1)), _const_spec((tm, tm))],
        out_specs=[pl.BlockSpec((TOP_K, tm), tokk)] * 3 + [_const_spec((ne, 1))],
        out_shape=[jax.ShapeDtypeStruct((TOP_K, n), I32), jax.ShapeDtypeStruct((TOP_K, n), F32),
                   jax.ShapeDtypeStruct((TOP_K, n), I32), jax.ShapeDtypeStruct((ne, 1), F32)],
        scratch_shapes=[pltpu.VMEM((ne, 1), F32)],
        compiler_params=_cparams(("arbitrary",)),
        name="moe_route",
    )(lg_t, rb, tri)


def _dispatch_kernel(pad_ref, dest_ref, h_ref, xs_ref, zero_s, sem, *, tm):
    i = pl.program_id(0)
    nrow = zero_s.shape[0]

    @pl.when(i == 0)
    def _():
        zero_s[...] = jnp.zeros(zero_s.shape, F32)

        def fill(start):
            cp = pltpu.make_async_copy(zero_s, xs_ref.at[pl.ds(pl.multiple_of(start, 8), nrow), :], sem)
            cp.start()
            cp.wait()

        def fill_pad(e, c):
            fill(pad_ref[e] // 8 * 8)
            return c

        lax.fori_loop(0, N_EXPERTS, fill_pad, 0)
        total = xs_ref.shape[0]
        tail = pad_ref[N_EXPERTS]

        def fill_tail(j, c):
            fill(jnp.minimum(tail + j * nrow, total - nrow))
            return c

        lax.fori_loop(0, (total - tail + nrow - 1) // nrow, fill_tail, 0)

    def issue_row(r, c):
        for k in range(TOP_K):
            d = dest_ref[r * TOP_K + k]
            pltpu.make_async_copy(h_ref.at[pl.ds(r, 1), :], xs_ref.at[pl.ds(d, 1), :], sem).start()
        return c

    lax.fori_loop(0, tm, issue_row, 0)
    pltpu.make_async_copy(xs_ref.at[pl.ds(0, tm * TOP_K), :], xs_ref.at[pl.ds(0, tm * TOP_K), :], sem).wait()


def _dispatch(pad_start, dest_flat, h2, cap):
    n, d = h2.shape
    tm = TM_MOVE
    kern = functools.partial(_dispatch_kernel, tm=tm)
    return pl.pallas_call(
        kern,
        grid_spec=pltpu.PrefetchScalarGridSpec(
            num_scalar_prefetch=1,
            grid=(n // tm,),
            in_specs=[
                pl.BlockSpec((tm * TOP_K,), lambda i, ps: (i,), memory_space=pltpu.SMEM),
                pl.BlockSpec((tm, d), lambda i, ps: (i, 0)),
            ],
            out_specs=pl.BlockSpec(memory_space=pl.ANY),
            scratch_shapes=[pltpu.VMEM((BM + 8, d), F32), pltpu.SemaphoreType.DMA(())],
        ),
        out_shape=jax.ShapeDtypeStruct((cap + BM + 8, d), F32),
        compiler_params=_cparams(("arbitrary",)),
        name="moe_dispatch",
    )(pad_start, dest_flat, h2)


def _experts_kernel(exp_ref, nused_ref, x_ref, w1_ref, w3_ref, w2_ref, y_ref):
    used = pl.program_id(0) < nused_ref[0]

    @pl.when(used)
    def _():
        xb = x_ref[...].astype(BF16)
        a = _dot(xb, w1_ref[...])
        hid = a * jax.nn.sigmoid(a) * _dot(xb, w3_ref[...])
        y_ref[...] = _dot(hid.astype(BF16), w2_ref[...])

    @pl.when(jnp.logical_not(used))
    def _():
        y_ref[...] = jnp.zeros(y_ref.shape, F32)


def _experts(blk_e, n_used, xs, w1, w3, w2):
    d = xs.shape[1]
    de = w1.shape[2]
    nb = blk_e.shape[0]
    return pl.pallas_call(
        _experts_kernel,
        grid_spec=pltpu.PrefetchScalarGridSpec(
            num_scalar_prefetch=2,
            grid=(nb,),
            in_specs=[
                pl.BlockSpec((BM, d), lambda i, e, u: (i, 0)),
                pl.BlockSpec((None, d, de), lambda i, e, u: (e[i], 0, 0)),
                pl.BlockSpec((None, d, de), lambda i, e, u: (e[i], 0, 0)),
                pl.BlockSpec((None, de, d), lambda i, e, u: (e[i], 0, 0)),
            ],
            out_specs=pl.BlockSpec((BM, d), lambda i, e, u: (i, 0)),
        ),
        out_shape=jax.ShapeDtypeStruct((nb * BM, d), F32),
        compiler_params=_cparams(("arbitrary",)),
        name="moe_experts",
    )(blk_e, n_used, xs, w1, w3, w2)


def _combine_kernel(dest_ref, ys_ref, g_ref, h_ref, x_ref, gtp_ref, gts_ref, s1_ref, s3_ref, s2_ref, xo_ref,
                    buf_s, sem, *, tm, npt):
    def issue_row(r, c):
        for k in range(TOP_K):
            d = dest_ref[r * TOP_K + k]
            pltpu.make_async_copy(ys_ref.at[pl.ds(d, 1), :], buf_s.at[k, pl.ds(r, 1), :], sem).start()
        return c

    lax.fori_loop(0, tm, issue_row, 0)
    hb = h_ref[...].astype(BF16)
    a = _dot(hb, s1_ref[...])
    shared = _dot((a * jax.nn.sigmoid(a) * _dot(hb, s3_ref[...])).astype(BF16), s2_ref[...])
    pltpu.make_async_copy(ys_ref.at[pl.ds(0, tm * TOP_K), :], ys_ref.at[pl.ds(0, tm * TOP_K), :], sem).wait()
    g = g_ref[...]
    routed = g[:, 0:1] * buf_s[0]
    for k in range(1, TOP_K):
        routed = routed + g[:, k:k + 1] * buf_s[k]
    gt = _pick(_sample_rows(pl.program_id(0), npt, tm), gtp_ref, gts_ref)
    xo_ref[...] = x_ref[...] + gt * (routed + shared)


def _combine(dest_flat, ys, gates, h2, x_all, gt, s1, s3, s2, bp, s):
    n, d = x_all.shape
    ds_ = s1.shape[1]
    tm = TM_MOVE
    npt = bp * s // tm
    kern = functools.partial(_combine_kernel, tm=tm, npt=npt)
    tok = lambda i: (i, 0)
    return pl.pallas_call(
        kern,
        grid=(n // tm,),
        in_specs=[
            pl.BlockSpec((tm * TOP_K,), lambda i: (i,), memory_space=pltpu.SMEM),
            pl.BlockSpec(memory_space=pl.ANY),
            pl.BlockSpec((tm, TOP_K), tok),
            pl.BlockSpec((tm, d), tok),
            pl.BlockSpec((tm, d), tok),
            *_mod_specs(tm, d, npt, s // tm, bp),
            _const_spec((d, ds_)),
            _const_spec((d, ds_)),
            _const_spec((ds_, d)),
        ],
        out_specs=pl.BlockSpec((tm, d), tok),
        out_shape=jax.ShapeDtypeStruct((n, d), F32),
        scratch_shapes=[pltpu.VMEM((TOP_K, tm, d), F32), pltpu.SemaphoreType.DMA(())],
        compiler_params=_cparams(("arbitrary",)),
        name="moe_combine",
    )(dest_flat, ys, gates, h2, x_all, *gt, s1, s3, s2)


def _moe(lg_t, h2, x_all, gt, rb, w1, w3, w2, s1, s3, s2, tri, bp, s):
    n, d = h2.shape
    idx_t, gate_t, rank_t, cnt = _route(lg_t, rb.reshape(N_EXPERTS, 1).astype(F32), tri)
    counts = cnt[:, 0].astype(I32)
    padded = (counts + BM - 1) // BM * BM
    pend = jnp.cumsum(padded)
    pstart = pend - padded
    nb = (n * TOP_K + N_EXPERTS * (BM - 1) + BM - 1) // BM
    n_used = pend[-1] // BM
    blk_e = jnp.minimum(jnp.sum(pend[None, :] <= (jnp.arange(nb, dtype=I32) * BM)[:, None], axis=1),
                        N_EXPERTS - 1).astype(I32)
    dest_t = jnp.sum(jnp.where(idx_t[None] == jnp.arange(N_EXPERTS, dtype=I32)[:, None, None],
                               pstart[:, None, None], 0), axis=0) + rank_t
    dest_flat = dest_t.T.reshape(-1)
    fill = jnp.concatenate([pstart + counts, pend[-1:]]).astype(I32)
    xs = _dispatch(fill, dest_flat, h2, nb * BM)
    ys = _experts(blk_e, n_used.reshape(1).astype(I32), xs, w1, w3, w2)
    return _combine(dest_flat, ys, gate_t.T, h2, x_all, gt, s1, s3, s2, bp, s)


def _mlstm_pre_kernel(x_ref, g_ref, shp_ref, shs_ref, scp_ref, scs_ref, wxo_ref, wgh_ref, wgl_ref, bif_ref, cw_ref, cb_ref,
                      wq_ref, wk_ref, wv_ref, halo_ref,
                      q_ref, k_ref, v_ref, o_ref, xcv_ref, gates_ref, tail_ref, xcs_ref,
                      xpad_s, *, npt, tps, t_s):
    i = pl.program_id(0)
    tm, d = x_ref.shape
    is_s = _sample_rows(i, npt, tm)
    h = _rms(x_ref[...], g_ref[...]) * (1.0 + _pick(is_s, scp_ref, scs_ref)) + _pick(is_s, shp_ref, shs_ref)
    h_hi, h_lo = _split2(h)
    y = _dot(h_hi, wxo_ref[...])
    xc = y[:, :d]
    o_ref[...] = y[:, d:].astype(BF16)
    wgh = wgh_ref[...]
    gp = _dot(h_hi, wgh) + _dot(h_hi, wgl_ref[...]) + _dot(h_lo, wgh) + bif_ref[...]
    lane = lax.broadcasted_iota(I32, gp.shape, 1)
    log_sig = jnp.minimum(gp, 0.0) - jnp.log(1.0 + jnp.exp(-jnp.abs(gp)))
    gates_ref[...] = jnp.where(lane >= C_HEADS, log_sig, gp)

    is_sample = i >= npt

    @pl.when(jnp.logical_or(i % tps == 0, is_sample))
    def _():
        xpad_s[0:8, :] = jnp.zeros((8, d), F32)

    xpad_s[8:, :] = xc
    row = (lax.broadcasted_iota(I32, (tm, 1), 0) & (t_s - 1)) + jnp.where(is_sample, 0, C_CONV)
    acc = xc * cw_ref[C_CONV - 1:C_CONV, :] + cb_ref[...]
    for j in range(1, C_CONV):
        prev = xpad_s[8 - j:8 - j + tm, :]
        prev = jnp.where(row < j, halo_ref[j - 1], prev)
        acc = acc + prev * cw_ref[C_CONV - 1 - j:C_CONV - j, :]
    xpad_s[0:8, :] = xc[tm - 8:, :]
    xconv = acc * jax.nn.sigmoid(acc)
    xcv16 = xconv.astype(BF16)
    xc16 = xc.astype(BF16)
    xcv_ref[...] = xcv16
    for hh in range(C_HEADS):
        cs = slice(hh * C_HEAD_DIM, (hh + 1) * C_HEAD_DIM)
        q_ref[:, cs] = _dot(xcv16[:, cs], wq_ref[hh]).astype(BF16)
        k_ref[:, cs] = (_dot(xcv16[:, cs], wk_ref[hh]) * C_HEAD_DIM ** -0.5).astype(BF16)
        v_ref[:, cs] = _dot(xc16[:, cs], wv_ref[hh]).astype(BF16)

    @pl.when(jnp.logical_not(is_sample))
    def _():
        tail_ref[...] = xc[tm - 8:, :]

    @pl.when(is_sample)
    def _():
        xcs_ref[...] = xc


def _mlstm_pre(x_all, g, sh, sc, wxo, wgh, wgl, bif, cw, cb, wq, wk, wv, halo, bp, s, t_s):
    n, d = x_all.shape
    npt = bp * s // TM
    tps = s // TM
    nst = n // TM - npt
    tok = lambda i: (i, 0)
    mod = _mod_specs(TM, d, npt, tps, bp)
    kern = functools.partial(_mlstm_pre_kernel, npt=npt, tps=tps, t_s=t_s)
    b16 = jax.ShapeDtypeStruct((n, d), BF16)
    return pl.pallas_call(
        kern,
        grid=(n // TM,),
        in_specs=[
            pl.BlockSpec((TM, d), tok),
            _const_spec((1, d)),
            *mod,
            *mod,
            _const_spec((d, 2 * d)),
            _const_spec((d, LANES)),
            _const_spec((d, LANES)),
            _const_spec((1, LANES)),
            _const_spec((C_CONV, d)),
            _const_spec((1, d)),
            _const_spec((C_HEADS, C_HEAD_DIM, C_HEAD_DIM)),
            _const_spec((C_HEADS, C_HEAD_DIM, C_HEAD_DIM)),
            _const_spec((C_HEADS, C_HEAD_DIM, C_HEAD_DIM)),
            pl.BlockSpec((None, C_CONV - 1, TM, d), lambda i: (jnp.maximum(i - npt, 0), 0, 0, 0),
                         pipeline_mode=pl.Buffered(1)),
        ],
        out_specs=[pl.BlockSpec((TM, d), tok)] * 5 + [
            pl.BlockSpec((TM, LANES), tok),
            pl.BlockSpec((None, 8, d), lambda i: (jnp.minimum(i // tps, bp - 1), 0, 0)),
            pl.BlockSpec((TM, d), lambda i: (jnp.maximum(i - npt, 0), 0)),
        ],
        out_shape=[b16] * 5 + [
            jax.ShapeDtypeStruct((n, LANES), F32),
            jax.ShapeDtypeStruct((bp, 8, d), F32),
            jax.ShapeDtypeStruct((nst * TM, d), F32),
        ],
        scratch_shapes=[pltpu.VMEM((TM + 8, d), F32)],
        compiler_params=_cparams(("arbitrary",)),
        name="mlstm_pre",
    )(x_all, g, *sh, *sc, wxo, wgh, wgl, bif, cw, cb, wq, wk, wv, halo)


def _mlstm_scan_kernel(q_ref, k_ref, v_ref, o_ref, xcv_ref, gc_ref, gr_ref, tri_ref, trit_ref, hg_ref, sk_ref,
                       mem0_ref, nrm0_ref, mx0_ref,
                       a_ref, memo_ref, nrmo_ref, mxo_ref,
                       mem_s, nrm_s, mx_s, *, nc):
    c = pl.program_id(1)
    ln = q_ref.shape[0]

    @pl.when(c == 0)
    def _():
        mem_s[...] = mem0_ref[...]
        nrm_s[...] = nrm0_ref[...]
        mx_s[...] = mx0_ref[...]

    gc = gc_ref[...]
    gr = gr_ref[...]
    tri = tri_ref[...]
    trit = trit_ref[...]
    bc = functools.reduce(lambda a, b: a + b, [_dot(tri, p) for p in _split3(gc)])
    br = functools.reduce(lambda a, b: a + b, [_dot(p, trit) for p in _split3(gr)])
    causal = lax.broadcasted_iota(I32, (ln, ln), 1) <= lax.broadcasted_iota(I32, (ln, ln), 0)
    for h in range(C_HEADS):
        cs = slice(h * C_HEAD_DIM, (h + 1) * C_HEAD_DIM)
        b_col = bc[:, C_HEADS + h:C_HEADS + h + 1]
        ig_col = gc[:, h:h + 1]
        b_row = br[C_HEADS + h:C_HEADS + h + 1, :]
        ig_row = gr[h:h + 1, :]
        b_last = b_row[:, ln - 1:ln]
        mx = mx_s[h:h + 1, 0:1]
        logw = jnp.where(causal, b_col - b_row + ig_row, NEG)
        g = b_col + mx
        m_t = jnp.maximum(g, jnp.max(logw, axis=1, keepdims=True))
        w = jnp.exp(logw - m_t)
        inter = jnp.exp(g - m_t)
        qh = q_ref[:, cs]
        kh = k_ref[:, cs]
        vh = v_ref[:, cs]
        a = w * _dot_nt(qh, kh)
        mem = mem_s[h]
        nrm = nrm_s[h:h + 1, :]
        num = _dot(a.astype(BF16), vh) + inter * _dot(qh, mem.astype(BF16))
        den = jnp.sum(a, axis=1, keepdims=True) + inter * jnp.sum(qh.astype(F32) * nrm, axis=1, keepdims=True)
        hout = num / jnp.maximum(jnp.abs(den), jnp.exp(-m_t))
        logs = b_last - b_col + ig_col
        m_new = jnp.maximum(b_last + mx, jnp.max(logs, axis=0, keepdims=True))
        decay = jnp.exp(b_last + mx - m_new)
        kw = kh.astype(F32) * jnp.exp(logs - m_new)
        mem_s[h] = decay * mem + _dot_tn(kw.astype(BF16), vh)
        nrm_s[h:h + 1, :] = decay * nrm + jnp.sum(kw, axis=0, keepdims=True)
        mx_s[h:h + 1, :] = jnp.broadcast_to(m_new, (1, mx_s.shape[1]))
        hh = hout * jax.nn.sigmoid(o_ref[:, cs].astype(F32))
        a_ref[:, cs] = (_rms(hh, hg_ref[:, cs]) + sk_ref[:, cs] * xcv_ref[:, cs].astype(F32)).astype(BF16)

    @pl.when(c == nc - 1)
    def _():
        memo_ref[...] = mem_s[...]
        nrmo_ref[...] = nrm_s[...]
        mxo_ref[...] = mx_s[...]


def _mlstm_scan(q, k, v, o, xcv, gates, hg, sk, mem0, nrm0, mx0, row0, nb, nc, ln):
    d = q.shape[1]
    nrow = nb * nc * ln
    gsl = lax.slice_in_dim(gates, row0 * ln, row0 * ln + nrow, axis=0)[:, :16]
    gr = jnp.transpose(gsl.reshape(nb * nc, ln, 16), (0, 2, 1))
    r = jnp.arange(ln)
    tri = (r[None, :] <= r[:, None]).astype(BF16)
    chunk = lambda b, c: (row0 + b * nc + c, 0)
    seq4 = lambda b, c: (b, 0, 0, 0)
    seq3 = lambda b, c: (b, 0, 0)
    kern = functools.partial(_mlstm_scan_kernel, nc=nc)
    return pl.pallas_call(
        kern,
        grid=(nb, nc),
        in_specs=[pl.BlockSpec((ln, d), chunk)] * 5 + [
            pl.BlockSpec((ln, LANES), chunk),
            pl.BlockSpec((None, 16, ln), lambda b, c: (b * nc + c, 0, 0)),
            _const_spec((ln, ln)),
            _const_spec((ln, ln)),
            _const_spec((1, d)),
            _const_spec((1, d)),
            pl.BlockSpec((None, C_HEADS, C_HEAD_DIM, C_HEAD_DIM), seq4),
            pl.BlockSpec((None, 8, C_HEAD_DIM), seq3),
            pl.BlockSpec((None, 8, LANES), seq3),
        ],
        out_specs=[
            pl.BlockSpec((ln, d), lambda b, c: (b * nc + c, 0)),
            pl.BlockSpec((None, C_HEADS, C_HEAD_DIM, C_HEAD_DIM), seq4),
            pl.BlockSpec((None, 8, C_HEAD_DIM), seq3),
            pl.BlockSpec((None, 8, LANES), seq3),
        ],
        out_shape=[
            jax.ShapeDtypeStruct((nrow, d), BF16),
            jax.ShapeDtypeStruct((nb, C_HEADS, C_HEAD_DIM, C_HEAD_DIM), F32),
            jax.ShapeDtypeStruct((nb, 8, C_HEAD_DIM), F32),
            jax.ShapeDtypeStruct((nb, 8, LANES), F32),
        ],
        scratch_shapes=[
            pltpu.VMEM((C_HEADS, C_HEAD_DIM, C_HEAD_DIM), F32),
            pltpu.VMEM((8, C_HEAD_DIM), F32),
            pltpu.VMEM((8, LANES), F32),
        ],
        compiler_params=_cparams(("parallel", "arbitrary")),
        name="mlstm_scan",
    )(q, k, v, o, xcv, gates, gr, tri, tri.T, hg, sk, mem0, nrm0, mx0)


def _pad_heads(a, width):
    nb = a.shape[0]
    if a.ndim == 2:
        a = jnp.broadcast_to(a[:, :, None], (nb, C_HEADS, width))
    return jnp.concatenate([a.astype(F32), jnp.zeros((nb, 8 - C_HEADS, width), F32)], axis=1)


def kernel(x_prompt, x_sample, c_prompt, c_sample, cache_a_k, cache_a_v, cache_b_k, cache_b_v, state_c_mem, state_c_norm, state_c_max, state_c_conv, norm_g, w_mod, b_mod, t5_bias, ab_w_in, ab_qk_g, ab_lambda, ab_head_g, ab_rel_bias, ab_w_out, c_w_in, c_b_if, c_conv_w, c_conv_b, c_w_qkv, c_head_g, c_skip, c_w_out, router_w, router_b, exp_w1, exp_w3, exp_w2, sh_w1, sh_w3, sh_w2):
    bp, s, d = x_prompt.shape
    bs, t = x_sample.shape[:2]
    depth = norm_g.shape[0]
    past = cache_a_k.shape[2]
    lb = cache_b_k.shape[2]
    n_p = bp * s
    n_s = bs * t
    n_all = n_p + n_s
    assert s % TM == 0 and n_s % TM == 0 and TM % t == 0 and t & (t - 1) == 0
    assert s % TQ_DIFF == 0 and s % ML_CHUNK == 0 and s >= BAND_PAST + TQ_BAND and TQ_DIFF >= T5_MAX_DIST
    assert past % CHUNK == 0 and lb == BAND_PAST and t <= CHUNK and t >= C_CONV - 1 and s >= BAND_PAST

    def per_token(vec):
        return vec[:bp].reshape(bp, 1, d), jnp.repeat(vec[bp:], t, axis=0)

    x_all = jnp.concatenate([x_prompt.reshape(n_p, d), x_sample.reshape(n_s, d)], axis=0)
    c_all = jnp.concatenate([c_prompt, c_sample], axis=0)
    mods = _modulation(c_all, w_mod, b_mod)

    r = jnp.arange(TM)
    tri_route = (r[:, None] < r[None, :]).astype(BF16)
    hd = jnp.arange(A_HEADS * 2 * HEAD_DIM) // HEAD_DIM
    bd = (hd[:, None] == hd[None, :]).astype(BF16)

    leaves = {}
    for l in range(depth):
        m6 = [mods[l][:, j * d:(j + 1) * d] for j in range(6)]
        sh1, sc1, gt1, sh2, sc2, gt2 = [per_token(v) for v in m6]
        i = l // 2
        if l % 2 == 0:
            lam_init = 0.8 - 0.6 * math.exp(-0.3 * l)
            lp = ab_lambda[i].astype(F32)
            lam = (jnp.exp(jnp.sum(lp[0] * lp[1])) - jnp.exp(jnp.sum(lp[2] * lp[3])) + lam_init).reshape(1)
            qkg_t = jnp.tile(ab_qk_g[i].astype(F32), (1, A_HEADS * 2))
            p16, ka, va, kb, vb = _ab_in_proj(x_all, norm_g[l, 0].reshape(1, d), sh1, sc1,
                                              ab_w_in[i].astype(BF16), qkg_t, bd, bp, s)
            hg = ab_head_g[i].reshape(1, 2 * HEAD_DIM).astype(F32)
            out_scale = 1.0 - lam_init
            tiles, far = _diff_bias_prompt(t5_bias, TQ_DIFF)
            oa_p = _diff_attn_prompt(p16, tiles, far, lam, hg.reshape(2 * HEAD_DIM, 1), bp, s, out_scale)
            ob_p = _band_attn_prompt(p16, _band_bias_prompt(ab_rel_bias[i], TQ_BAND), bp, s)
            dbc, dbn = _diff_bias_sample(t5_bias, past, t)
            oa_s = _sample_attn(p16, cache_a_k[i].reshape(bs, past, -1), cache_a_v[i].reshape(bs, past, -1),
                                dbc, dbn, lam, hg, n_p, bs, t, True, out_scale)
            bbc, bbn = _band_bias_sample(ab_rel_bias[i], past, lb, t)
            ob_s = _sample_attn(p16, cache_b_k[i].reshape(bs, lb, -1), cache_b_v[i].reshape(bs, lb, -1),
                                bbc, bbn, lam, hg, n_p, bs, t, False, out_scale)
            oa = jnp.concatenate([oa_p, oa_s.reshape(n_s, -1)], axis=0)
            ob = jnp.concatenate([ob_p, ob_s.reshape(n_s, -1)], axis=0)
            mix_in = (oa, 0, ob, 0)
            w_out16 = ab_w_out[i].astype(BF16)
            keep = min(BAND_PAST, s)
            leaves.setdefault('akp', []).append(ka[:n_p].reshape(bp, s, A_HEADS, 2, HEAD_DIM))
            leaves.setdefault('avp', []).append(va[:n_p].reshape(bp, s, A_HEADS, 2 * HEAD_DIM))
            leaves.setdefault('aks', []).append(ka[n_p:].reshape(bs, t, A_HEADS, 2, HEAD_DIM))
            leaves.setdefault('avs', []).append(va[n_p:].reshape(bs, t, A_HEADS, 2 * HEAD_DIM))
            leaves.setdefault('bkp', []).append(kb[:n_p].reshape(bp, s, B_HEADS, HEAD_DIM)[:, s - keep:])
            leaves.setdefault('bvp', []).append(vb[:n_p].reshape(bp, s, B_HEADS, HEAD_DIM)[:, s - keep:])
            leaves.setdefault('bks', []).append(kb[n_p:].reshape(bs, t, B_HEADS, HEAD_DIM))
            leaves.setdefault('bvs', []).append(vb[n_p:].reshape(bs, t, B_HEADS, HEAD_DIM))
        else:
            w_in = c_w_in[i]
            wg = jnp.pad(w_in[:, 2 * d:].astype(F32), ((0, 0), (0, LANES - 2 * C_HEADS)))
            wgh = wg.astype(BF16)
            wgl = (wg - wgh.astype(F32)).astype(BF16)
            bif = jnp.pad(c_b_if[i].astype(F32).reshape(1, 2 * C_HEADS), ((0, 0), (0, LANES - 2 * C_HEADS)))
            cprev = state_c_conv[i].astype(F32)
            planes = []
            for j in range(1, C_CONV):
                rows = jnp.concatenate([cprev[:, C_CONV - 1 - j:, :], jnp.zeros((bs, t - j, d), F32)], axis=1)
                planes.append(rows.reshape(n_s // TM, TM, d))
            halo = jnp.stack(planes, axis=1)
            wqkv = c_w_qkv[i].astype(BF16)
            q, k, v, o, xcv, gates, tail, xcs = _mlstm_pre(
                x_all, norm_g[l, 0].reshape(1, d), sh1, sc1, w_in[:, :2 * d].astype(BF16), wgh, wgl, bif,
                c_conv_w[i].astype(F32), c_conv_b[i].reshape(1, d).astype(F32), wqkv[0], wqkv[1], wqkv[2],
                halo, bp, s, t)
            hg = c_head_g[i].reshape(1, d).astype(F32)
            sk = c_skip[i].reshape(1, d).astype(F32)
            zm = jnp.zeros((bp, C_HEADS, C_HEAD_DIM, C_HEAD_DIM), F32)
            a_p, mem_p, nrm_p, mx_p = _mlstm_scan(
                q, k, v, o, xcv, gates, hg, sk, zm, jnp.zeros((bp, 8, C_HEAD_DIM), F32),
                jnp.zeros((bp, 8, LANES), F32), 0, bp, s // ML_CHUNK, ML_CHUNK)
            a_s, mem_s, nrm_s, mx_s = _mlstm_scan(
                q, k, v, o, xcv, gates, hg, sk, state_c_mem[i].astype(F32),
                _pad_heads(state_c_norm[i], C_HEAD_DIM), _pad_heads(state_c_max[i], LANES),
                n_p // t, bs, 1, t)
            a_all = jnp.concatenate([a_p, a_s], axis=0)
            mix_in = (a_all, 0, a_all, 1)
            w_out16 = c_w_out[i].astype(BF16)
            leaves.setdefault('memp', []).append(mem_p)
            leaves.setdefault('normp', []).append(nrm_p[:, :C_HEADS])
            leaves.setdefault('maxp', []).append(mx_p[:, :C_HEADS, 0])
            leaves.setdefault('convp', []).append(tail[:, 8 - (C_CONV - 1):])
            leaves.setdefault('mems', []).append(mem_s)
            leaves.setdefault('norms', []).append(nrm_s[:, :C_HEADS])
            leaves.setdefault('maxs', []).append(mx_s[:, :C_HEADS, 0])
            leaves.setdefault('convs', []).append(xcs.reshape(bs, t, d)[:, t - (C_CONV - 1):])
        rw_t = router_w[l].astype(F32).T
        rw_hi = rw_t.astype(BF16)
        rw_lo = (rw_t - rw_hi.astype(F32)).astype(BF16)
        x_all, h2, lg_t = _out_proj(mix_in[0], mix_in[1], mix_in[2], mix_in[3], w_out16, x_all, gt1,
                                    norm_g[l, 1].reshape(1, d), sh2, sc2, rw_hi, rw_lo, bp, s)
        x_all = _moe(lg_t, h2, x_all, gt2, router_b[l], exp_w1[l].astype(BF16), exp_w3[l].astype(BF16),
                     exp_w2[l].astype(BF16), sh_w1[l].astype(BF16), sh_w3[l].astype(BF16),
                     sh_w2[l].astype(BF16), tri_route, bp, s)

    order = ['akp', 'avp', 'aks', 'avs', 'bkp', 'bvp', 'bks', 'bvs',
             'memp', 'normp', 'maxp', 'convp', 'mems', 'norms', 'maxs', 'convs']
    return (x_all[:n_p].reshape(bp, s, d), x_all[n_p:].reshape(bs, t, d)) + tuple(
        jnp.stack(leaves[name]) for name in order)
```

```python
import functools
import math

import jax
import jax.numpy as jnp
from jax import lax
from jax.experimental import pallas as pl
from jax.experimental.pallas import tpu as pltpu

F32 = jnp.float32
BF16 = jnp.bfloat16
I32 = jnp.int32
U32 = jnp.uint32

EPS = 1e-6
NEG = -1e30
CHUNK = 64
HEAD_DIM = 64
A_HEADS = 4
B_HEADS = 8
BAND_CHUNKS = 8
BAND_PAST = BAND_CHUNKS * CHUNK
REL_CLIP = 128
T5_BUCKETS = 32
T5_MAX_DIST = 128
C_HEADS = 4
C_HEAD_DIM = 256
C_CONV = 4
N_EXPERTS = 64
TOP_K = 8
ROUTE_SCALE = 2.5

LANES = 128
TM = 512
TQ_DIFF = 256
TQ_BAND = 128
ML_CHUNK = 256
BM = 512
TM_MOVE = 256
VMEM_LIMIT = 56 * 1024 * 1024


def _cparams(sem):
    return pltpu.CompilerParams(dimension_semantics=sem, vmem_limit_bytes=VMEM_LIMIT)


def _dot(a, b):
    return jnp.dot(a, b, preferred_element_type=F32)


def _dot_nt(a, b):
    return lax.dot_general(a, b, (((1,), (1,)), ((), ())), preferred_element_type=F32)


def _dot_tn(a, b):
    return lax.dot_general(a, b, (((0,), (0,)), ((), ())), preferred_element_type=F32)


def _split2(x):
    hi = x.astype(BF16)
    lo = (x - hi.astype(F32)).astype(BF16)
    return hi, lo


def _split3(x):
    p0 = x.astype(BF16)
    r1 = x - p0.astype(F32)
    p1 = r1.astype(BF16)
    p2 = (r1 - p1.astype(F32)).astype(BF16)
    return p0, p1, p2


def _pack_rows(x):
    half = x.shape[1] // 2
    bits = lax.bitcast_convert_type(x.astype(BF16).astype(F32), U32)
    return bits[:, :half] | (bits[:, half:] >> 16)


def _unpack_rows(p):
    return (lax.bitcast_convert_type(p & jnp.uint32(0xFFFF0000), F32),
            lax.bitcast_convert_type(p << 16, F32))


def _rms(x, g):
    return x * lax.rsqrt(jnp.mean(x * x, axis=-1, keepdims=True) + EPS) * g


def _mod_specs(tile, d, npt, tps, bp):
    return [pl.BlockSpec((None, 1, d), lambda i: (jnp.minimum(i // tps, bp - 1), 0, 0)),
            pl.BlockSpec((tile, d), lambda i: (jnp.maximum(i - npt, 0), 0), pipeline_mode=pl.Buffered(1))]


def _tok_specs(tile, width, npt):
    return [pl.BlockSpec((tile, width), lambda i: (jnp.minimum(i, npt - 1), 0)),
            pl.BlockSpec((tile, width), lambda i: (jnp.maximum(i - npt, 0), 0))]


def _store_tok(i, npt, p_ref, s_ref, val):
    @pl.when(i < npt)
    def _():
        p_ref[...] = val

    @pl.when(i >= npt)
    def _():
        s_ref[...] = val


def _sample_rows(i, npt, tm):
    return lax.broadcasted_iota(I32, (tm, 1), 0) >= jnp.where(i >= npt, 0, tm)


def _pick(is_s, vp_ref, vs_ref):
    return jnp.where(is_s, vs_ref[...], vp_ref[...])


def _const_spec(shape):
    nd = len(shape)
    return pl.BlockSpec(shape, lambda *_: (0,) * nd, pipeline_mode=pl.Buffered(1))


def _mod_kernel(c_ref, w_ref, b_ref, o_ref):
    c = c_ref[...]
    a_hi, a_lo = _split2(c * jax.nn.sigmoid(c))
    w_hi, w_lo = _split2(w_ref[...])
    o_ref[...] = _dot(a_hi, w_hi) + _dot(a_hi, w_lo) + _dot(a_lo, w_hi) + b_ref[...]


def _modulation(c_all, w_mod, b_mod):
    depth, d, n6 = w_mod.shape
    nseq = c_all.shape[0]
    tn = 512
    return pl.pallas_call(
        _mod_kernel,
        grid=(depth, n6 // tn),
        in_specs=[
            pl.BlockSpec((nseq, d), lambda l, j: (0, 0)),
            pl.BlockSpec((None, d, tn), lambda l, j: (l, 0, j)),
            pl.BlockSpec((None, 1, tn), lambda l, j: (l, 0, j)),
        ],
        out_specs=pl.BlockSpec((None, nseq, tn), lambda l, j: (l, 0, j)),
        out_shape=jax.ShapeDtypeStruct((depth, nseq, n6), F32),
        compiler_params=_cparams(("parallel", "parallel")),
        name="modulation",
    )(c_all, w_mod, b_mod.reshape(depth, 1, n6))


def _ab_in_kernel(xp_ref, xs_ref, g_ref, shp_ref, shs_ref, scp_ref, scs_ref, w_ref, qkg_ref, bd_ref,
                  p16_ref, kap_ref, kas_ref, vap_ref, vas_ref, kbp_ref, kbs_ref, vbp_ref, vbs_ref, *, npt, tps):
    i = pl.program_id(0)
    is_s = _sample_rows(i, npt, xp_ref.shape[0])
    h = (_rms(_pick(is_s, xp_ref, xs_ref), g_ref[...]) * (1.0 + _pick(is_s, scp_ref, scs_ref))
         + _pick(is_s, shp_ref, shs_ref))
    y = _dot(h.astype(BF16), w_ref[...])
    bd = bd_ref[...]
    wa = A_HEADS * 2 * HEAD_DIM

    def group_norm(seg, gi):
        hi, lo = _split2(seg * seg)
        ss = _dot(hi, bd) + _dot(lo, bd)
        return seg * lax.rsqrt(ss * (1.0 / HEAD_DIM) + EPS) * qkg_ref[gi:gi + 1, :]

    qa = group_norm(y[:, 0 * wa:1 * wa], 0)
    ka = group_norm(y[:, 1 * wa:2 * wa], 1)
    va = y[:, 2 * wa:3 * wa]
    qb = group_norm(y[:, 3 * wa:4 * wa], 2)
    kb = group_norm(y[:, 4 * wa:5 * wa], 3)
    vb = y[:, 5 * wa:6 * wa]
    scale = HEAD_DIM ** -0.5
    p16_ref[:, 0 * wa:1 * wa] = (qa * scale).astype(BF16)
    p16_ref[:, 1 * wa:2 * wa] = ka.astype(BF16)
    p16_ref[:, 2 * wa:3 * wa] = va.astype(BF16)
    p16_ref[:, 3 * wa:4 * wa] = (qb * scale).astype(BF16)
    p16_ref[:, 4 * wa:5 * wa] = kb.astype(BF16)
    p16_ref[:, 5 * wa:6 * wa] = vb.astype(BF16)
    _store_tok(i, npt, kap_ref, kas_ref, ka)
    _store_tok(i, npt, vap_ref, vas_ref, va)

    @pl.when(jnp.logical_and(i < npt, i % tps == tps - 1))
    def _():
        kbp_ref[...] = kb
        vbp_ref[...] = vb

    @pl.when(i >= npt)
    def _():
        kbs_ref[...] = kb
        vbs_ref[...] = vb


def _ab_in_proj(x, g, sh, sc, w16, qkg_t, bd, bp, s):
    n_p, d = x[0].shape
    n_s = x[1].shape[0]
    n = n_p + n_s
    wa = A_HEADS * 2 * HEAD_DIM
    n_in = w16.shape[1]
    tok = lambda i: (i, 0)
    npt = n_p // TM
    tps = s // TM
    mod = _mod_specs(TM, d, npt, tps, bp)
    leaf = _tok_specs(TM, wa, npt)
    band = [pl.BlockSpec((None, TM, wa), lambda i: (jnp.minimum(i // tps, bp - 1), 0, 0)), leaf[1]]
    f32 = lambda rows: jax.ShapeDtypeStruct((rows, wa), F32)
    band_shape = [jax.ShapeDtypeStruct((bp, TM, wa), F32), f32(n_s)]
    return pl.pallas_call(
        functools.partial(_ab_in_kernel, npt=npt, tps=tps),
        grid=(n // TM,),
        in_specs=[
            *_tok_specs(TM, d, npt),
            _const_spec((1, d)),
            *mod,
            *mod,
            _const_spec((d, n_in)),
            _const_spec((4, wa)),
            _const_spec((wa, wa)),
        ],
        out_specs=[pl.BlockSpec((TM, n_in), tok)] + leaf + leaf + band + band,
        out_shape=[jax.ShapeDtypeStruct((n, n_in), BF16), f32(n_p), f32(n_s), f32(n_p), f32(n_s)]
        + band_shape + band_shape,
        compiler_params=_cparams(("arbitrary",)),
        name="ab_in_proj",
    )(*x, g, *sh, *sc, w16, qkg_t, bd)


def _t5_bucket(rel):
    half = T5_BUCKETS // 2
    exact = half // 2
    n = jnp.abs(rel)
    large = exact + (jnp.log(jnp.maximum(n, 1).astype(F32) / exact)
                     / math.log(T5_MAX_DIST / exact) * (half - exact)).astype(I32)
    large = jnp.minimum(large, half - 1)
    return jnp.where(rel > 0, half, 0) + jnp.where(n < exact, n, large)


def _lookup(table, idx):
    onehot = (idx[..., None] == jnp.arange(table.shape[0], dtype=I32)).astype(F32)
    return jnp.einsum('...n,nh->...h', onehot, table.astype(F32), precision=lax.Precision.HIGHEST)


def _diff_bias_prompt(t5_bias, tq):
    i = jnp.arange(tq)[None, :]
    j = jnp.arange(tq)[:, None]
    diag = jnp.where(((j // CHUNK) <= (i // CHUNK))[..., None], _lookup(t5_bias, _t5_bucket(j - i)), NEG)
    prev = _lookup(t5_bias, _t5_bucket(j - i - tq))
    first = jnp.concatenate([diag, jnp.full_like(diag, NEG)], axis=0)
    later = jnp.concatenate([prev, diag], axis=0)
    tiles = jnp.transpose(jnp.stack([first, later]), (3, 0, 1, 2))
    tiles = jnp.concatenate([tiles, tiles], axis=3)
    far = _lookup(t5_bias, _t5_bucket(jnp.full((1,), -T5_MAX_DIST, I32)))[0]
    return tiles, far


def _diff_bias_sample(t5_bias, past, t):
    qpos = past + jnp.arange(t)
    kpos = jnp.arange(past + t)
    rel = kpos[None, :] - qpos[:, None]
    vis = (kpos[None, :] // CHUNK) <= (qpos[:, None] // CHUNK)
    b = jnp.where(vis[..., None], _lookup(t5_bias, _t5_bucket(rel)), NEG)
    b = jnp.transpose(b, (2, 0, 1))
    b = jnp.concatenate([b, b], axis=1)
    return b[:, :, :past], b[:, :, past:]


def _band_bias_prompt(rel_bias, tq):
    nvar = BAND_PAST // tq + 1
    win = BAND_PAST + tq
    u = jnp.arange(nvar)[:, None, None]
    i = jnp.arange(tq)[None, :, None]
    j = jnp.arange(win)[None, None, :]
    qp = u * tq + i
    qc = qp // CHUNK
    kc = j // CHUNK
    valid = (kc <= qc) & (kc >= qc - BAND_CHUNKS)
    b = _lookup(rel_bias, jnp.clip(j - qp, -REL_CLIP, REL_CLIP) + REL_CLIP)
    b = jnp.where(valid[..., None], b, NEG)
    b = jnp.transpose(b, (0, 3, 1, 2))
    return b.reshape(nvar, B_HEADS // 2, 2 * tq, win)


def _band_bias_sample(rel_bias, past, lb, t):
    qpos = past + jnp.arange(t)
    kpos = past - lb + jnp.arange(lb + t)
    band_lo = (past // CHUNK - BAND_CHUNKS) * CHUNK
    rel = jnp.clip(kpos[None, :] - qpos[:, None], -REL_CLIP, REL_CLIP) + REL_CLIP
    b = jnp.where((kpos >= band_lo)[None, :, None], _lookup(rel_bias, rel), NEG)
    b = jnp.transpose(b, (2, 0, 1)).reshape(B_HEADS // 2, 2 * t, lb + t)
    return b[:, :, :lb], b[:, :, lb:]


def _stack_halves(q):
    lane = lax.broadcasted_iota(I32, q.shape, 1)
    zero = jnp.zeros_like(q)
    return jnp.concatenate([jnp.where(lane < HEAD_DIM, q, zero), jnp.where(lane >= HEAD_DIM, q, zero)], axis=0)


def _diff_finish(o1, o2, lam, hg, out_scale):
    o = o1 - lam * o2
    return (_rms(o, hg) * out_scale).astype(BF16)


def _band_finish(o):
    tq = o.shape[0] // 2
    lane = lax.broadcasted_iota(I32, (tq, o.shape[1]), 1)
    return jnp.where(lane < HEAD_DIM, o[:tq], o[tq:]).astype(BF16)


def _diff_prompt_kernel(far_ref, lam_ref, q_ref, k_ref, v_ref, bias_ref, hg_ref, o_ref,
                        q2t_s, vt_s, m_s, l_s, acc_s, *, tq, out_scale):
    h = pl.program_id(1)
    i = pl.program_id(2)

    @pl.when(i == 0)
    def _():
        for jj in range(vt_s.shape[0]):
            vt_s[jj] = v_ref[jj * tq:(jj + 1) * tq, :].astype(F32).T.astype(BF16)

    qt = q_ref[...].astype(F32).T.astype(BF16)
    row = lax.broadcasted_iota(I32, qt.shape, 0)
    zero = jnp.zeros_like(qt)
    q2t_s[:, 0:tq] = jnp.where(row < HEAD_DIM, qt, zero)
    q2t_s[:, tq:2 * tq] = jnp.where(row >= HEAD_DIM, qt, zero)
    m_s[...] = jnp.full(m_s.shape, NEG, F32)
    l_s[...] = jnp.zeros(l_s.shape, F32)
    acc_s[...] = jnp.zeros(acc_s.shape, F32)

    def step(jb, nblk, bias, shift):
        kb = k_ref[pl.ds(pl.multiple_of(jb * tq, tq), nblk * tq), :]
        s = _dot(kb, q2t_s[...])
        if bias is not None:
            s = s + bias
        cmax = jnp.max(s, axis=0, keepdims=True)
        if shift is not None:
            cmax = cmax + shift
        m_prev = m_s[...]
        m_new = jnp.maximum(m_prev, cmax)
        alpha = jnp.exp(m_prev - m_new)
        p = jnp.exp(s - (m_new if shift is None else m_new - shift))
        l_s[...] = alpha * l_s[...] + jnp.sum(p, axis=0, keepdims=True)
        pb = p.astype(BF16)
        pv = _dot(vt_s[jb], pb[0:tq])
        for u in range(1, nblk):
            pv = pv + _dot(vt_s[jb + u], pb[u * tq:(u + 1) * tq])
        acc_s[...] = alpha * acc_s[...] + pv
        m_s[...] = m_new

    far = far_ref[h]
    nfar = jnp.maximum(i - 1, 0)

    def far_pair(j, carry):
        step(2 * j, 2, None, far)
        return carry

    lax.fori_loop(0, nfar // 2, far_pair, 0)

    @pl.when(nfar % 2 == 1)
    def _():
        step(nfar - 1, 1, None, far)

    step(nfar, 2, bias_ref[...], None)
    o = acc_s[...] / l_s[...]
    od = o[:, 0:tq] - lam_ref[0] * o[:, tq:2 * tq]
    on = od * lax.rsqrt(jnp.mean(od * od, axis=0, keepdims=True) + EPS) * hg_ref[...] * out_scale
    o_ref[...] = on.T.astype(BF16)


def _diff_attn_prompt(p16, tiles, far, lam, hg_col, bp, s, out_scale):
    tq = TQ_DIFF
    nq = s // tq
    wa = A_HEADS * LANES
    kern = functools.partial(_diff_prompt_kernel, tq=tq, out_scale=out_scale)
    return pl.pallas_call(
        kern,
        grid=(bp, A_HEADS, nq),
        in_specs=[
            pl.BlockSpec(memory_space=pltpu.SMEM),
            pl.BlockSpec(memory_space=pltpu.SMEM),
            pl.BlockSpec((tq, LANES), lambda b, h, i: (b * nq + i, h)),
            pl.BlockSpec((s, LANES), lambda b, h, i: (b, A_HEADS + h)),
            pl.BlockSpec((s, LANES), lambda b, h, i: (b, 2 * A_HEADS + h)),
            pl.BlockSpec((None, None, 2 * tq, 2 * tq), lambda b, h, i: (h, jnp.minimum(i, 1), 0, 0)),
            _const_spec((LANES, 1)),
        ],
        out_specs=pl.BlockSpec((tq, LANES), lambda b, h, i: (b * nq + i, h)),
        out_shape=jax.ShapeDtypeStruct((bp * s, wa), BF16),
        scratch_shapes=[
            pltpu.VMEM((LANES, 2 * tq), BF16),
            pltpu.VMEM((nq, LANES, tq), BF16),
            pltpu.VMEM((1, 2 * tq), F32),
            pltpu.VMEM((1, 2 * tq), F32),
            pltpu.VMEM((LANES, 2 * tq), F32),
        ],
        compiler_params=_cparams(("parallel", "parallel", "arbitrary")),
        name="diff_attn_prompt",
    )(far, lam, p16, p16, p16, tiles, hg_col)


def _band_prompt_kernel(q_ref, k_ref, v_ref, bias_ref, o_ref, *, tq, win):
    t = pl.program_id(2)
    start = pl.multiple_of(jnp.maximum(t * tq - BAND_PAST, 0), tq)
    kb = k_ref[pl.ds(start, win), :]
    vb = v_ref[pl.ds(start, win), :]
    s = _dot_nt(_stack_halves(q_ref[...]), kb) + bias_ref[...]
    m = jnp.max(s, axis=-1, keepdims=True)
    p = jnp.exp(s - m)
    l = jnp.sum(p, axis=-1, keepdims=True)
    o_ref[...] = _band_finish(_dot(p.astype(BF16), vb) / l)


def _band_attn_prompt(p16, bias, bp, s):
    tq = TQ_BAND
    nq = s // tq
    win = BAND_PAST + tq
    nvar = bias.shape[0]
    npair = B_HEADS // 2
    c0 = 3 * A_HEADS
    kern = functools.partial(_band_prompt_kernel, tq=tq, win=win)
    return pl.pallas_call(
        kern,
        grid=(bp, npair, nq),
        in_specs=[
            pl.BlockSpec((tq, LANES), lambda b, p, t: (b * nq + t, c0 + p)),
            pl.BlockSpec((s, LANES), lambda b, p, t: (b, c0 + npair + p)),
            pl.BlockSpec((s, LANES), lambda b, p, t: (b, c0 + 2 * npair + p)),
            pl.BlockSpec((None, None, 2 * tq, win), lambda b, p, t: (jnp.minimum(t, nvar - 1), p, 0, 0)),
        ],
        out_specs=pl.BlockSpec((tq, LANES), lambda b, p, t: (b * nq + t, p)),
        out_shape=jax.ShapeDtypeStruct((bp * s, npair * LANES), BF16),
        compiler_params=_cparams(("parallel", "parallel", "parallel")),
        name="band_attn_prompt",
    )(p16, p16, p16, bias)


def _sample_attn_kernel(lam_ref, q_ref, kc_ref, vc_ref, kn_ref, vn_ref, bc_ref, bn_ref, hg_ref, o_ref,
                        *, diff, out_scale):
    t = q_ref.shape[0]
    q2 = _stack_halves(q_ref[...])
    sc = _dot_nt(q2, kc_ref[...].astype(BF16)) + bc_ref[...]
    sn = _dot_nt(q2, kn_ref[...]) + bn_ref[...]
    m = jnp.maximum(jnp.max(sc, axis=-1, keepdims=True), jnp.max(sn, axis=-1, keepdims=True))
    pc = jnp.exp(sc - m)
    pn = jnp.exp(sn - m)
    l = jnp.sum(pc, axis=-1, keepdims=True) + jnp.sum(pn, axis=-1, keepdims=True)
    o = (_dot(pc.astype(BF16), vc_ref[...].astype(BF16)) + _dot(pn.astype(BF16), vn_ref[...])) / l
    if diff:
        o_ref[...] = _diff_finish(o[:t], o[t:], lam_ref[0], hg_ref[...], out_scale)
    else:
        o_ref[...] = _band_finish(o)


def _sample_attn(p16, cache_k, cache_v, bias_c, bias_n, lam, hg, np_rows, bs, t, diff, out_scale):
    past = cache_k.shape[1]
    ncol = cache_k.shape[2] // LANES
    row0 = np_rows // t
    if diff:
        qc, kc, vc = 0, A_HEADS, 2 * A_HEADS
    else:
        qc, kc, vc = 3 * A_HEADS, 3 * A_HEADS + ncol, 3 * A_HEADS + 2 * ncol
    kern = functools.partial(_sample_attn_kernel, diff=diff, out_scale=out_scale)
    return pl.pallas_call(
        kern,
        grid=(bs, ncol),
        in_specs=[
            pl.BlockSpec(memory_space=pltpu.SMEM),
            pl.BlockSpec((t, LANES), lambda b, h: (row0 + b, qc + h)),
            pl.BlockSpec((None, past, LANES), lambda b, h: (b, 0, h)),
            pl.BlockSpec((None, past, LANES), lambda b, h: (b, 0, h)),
            pl.BlockSpec((t, LANES), lambda b, h: (row0 + b, kc + h)),
            pl.BlockSpec((t, LANES), lambda b, h: (row0 + b, vc + h)),
            pl.BlockSpec((None, 2 * t, past), lambda b, h: (h, 0, 0)),
            pl.BlockSpec((None, 2 * t, t), lambda b, h: (h, 0, 0)),
            _const_spec((1, LANES)),
        ],
        out_specs=pl.BlockSpec((None, t, LANES), lambda b, h: (b, 0, h)),
        out_shape=jax.ShapeDtypeStruct((bs, t, ncol * LANES), BF16),
        compiler_params=_cparams(("parallel", "parallel")),
        name="diff_attn_sample" if diff else "band_attn_sample",
    )(lam, p16, cache_k, cache_v, p16, p16, bias_c, bias_n, hg)


def _out_proj_kernel(a0_ref, a1_ref, w_ref, xp_ref, xs_ref, gtp_ref, gts_ref, g2_ref, shp_ref, shs_ref,
                     scp_ref, scs_ref, rwh_ref, rwl_ref, xop_ref, xos_ref, hp_ref, lg_ref, *, npt):
    half = a0_ref.shape[1]
    i = pl.program_id(0)
    is_s = _sample_rows(i, npt, xp_ref.shape[0])
    mix = _dot(a0_ref[...], w_ref[0:half, :]) + _dot(a1_ref[...], w_ref[half:2 * half, :])
    x = _pick(is_s, xp_ref, xs_ref) + _pick(is_s, gtp_ref, gts_ref) * mix
    _store_tok(i, npt, xop_ref, xos_ref, x)
    h2 = _rms(x, g2_ref[...]) * (1.0 + _pick(is_s, scp_ref, scs_ref)) + _pick(is_s, shp_ref, shs_ref)
    hp_ref[...] = _pack_rows(h2)
    h_hi, h_lo = _split2(h2)
    rw_hi = rwh_ref[...]
    lg_ref[...] = _dot_nt(rw_hi, h_hi) + _dot_nt(rw_hi, h_lo) + _dot_nt(rwl_ref[...], h_hi)


def _out_proj(a0, c0, a1, c1, w16, x, gt, g2, sh, sc, rw_hi, rw_lo, bp, s):
    n_p, d = x[0].shape
    n_s = x[1].shape[0]
    n = n_p + n_s
    half = d // 2
    ne = rw_hi.shape[0]
    tok = lambda i: (i, 0)
    npt = n_p // TM
    mod = _mod_specs(TM, d, npt, s // TM, bp)
    xspecs = _tok_specs(TM, d, npt)
    return pl.pallas_call(
        functools.partial(_out_proj_kernel, npt=npt),
        grid=(n // TM,),
        in_specs=[
            pl.BlockSpec((TM, half), lambda i: (i, c0)),
            pl.BlockSpec((TM, half), lambda i: (i, c1)),
            _const_spec((d, d)),
            *xspecs,
            *mod,
            _const_spec((1, d)),
            *mod,
            *mod,
            _const_spec((ne, d)),
            _const_spec((ne, d)),
        ],
        out_specs=xspecs + [pl.BlockSpec((TM, half), tok), pl.BlockSpec((ne, TM), lambda i: (0, i))],
        out_shape=[jax.ShapeDtypeStruct((n_p, d), F32), jax.ShapeDtypeStruct((n_s, d), F32),
                   jax.ShapeDtypeStruct((n, half), U32), jax.ShapeDtypeStruct((ne, n), F32)],
        compiler_params=_cparams(("arbitrary",)),
        name="out_proj",
    )(a0, a1, w16, *x, *gt, g2, *sh, *sc, rw_hi, rw_lo)


def _route_kernel(lg_ref, rb_ref, tri_ref, idx_ref, gate_ref, rank_ref, cnt_ref, carry_s):
    @pl.when(pl.program_id(0) == 0)
    def _():
        carry_s[...] = jnp.zeros(carry_s.shape, F32)

    s = jax.nn.sigmoid(lg_ref[...])
    sb = s + rb_ref[...]
    row = lax.broadcasted_iota(I32, s.shape, 0).astype(F32)
    picks = []
    sel = jnp.zeros(s.shape, F32)
    for _ in range(TOP_K):
        m = jnp.max(sb, axis=0, keepdims=True)
        ik = jnp.min(jnp.where(sb == m, row, float(N_EXPERTS)), axis=0, keepdims=True)
        oh = row == ik
        picks.append((ik, oh, jnp.sum(jnp.where(oh, s, 0.0), axis=0, keepdims=True)))
        sel = sel + oh.astype(F32)
        sb = jnp.where(oh, -jnp.inf, sb)
    before = _dot(sel.astype(BF16), tri_ref[...]) + carry_s[...]
    gsum = functools.reduce(lambda a, b: a + b, [g for _, _, g in picks])
    for k, (ik, oh, g) in enumerate(picks):
        idx_ref[k:k + 1, :] = ik.astype(I32)
        gate_ref[k:k + 1, :] = g / gsum * ROUTE_SCALE
        rank_ref[k:k + 1, :] = jnp.sum(jnp.where(oh, before, 0.0), axis=0, keepdims=True).astype(I32)
    carry_s[...] = carry_s[...] + jnp.sum(sel, axis=1, keepdims=True)
    cnt_ref[...] = carry_s[...]


def _route(lg_t, rb, tri):
    ne, n = lg_t.shape
    tm = tri.shape[0]
    tokk = lambda i: (0, i)
    return pl.pallas_call(
        _route_kernel,
        grid=(n // tm,),
        in_specs=[pl.BlockSpec((ne, tm), tokk), _const_spec((ne, 1)), _const_spec((tm, tm))],
        out_specs=[pl.BlockSpec((TOP_K, tm), tokk)] * 3 + [_const_spec((ne, 1))],
        out_shape=[jax.ShapeDtypeStruct((TOP_K, n), I32), jax.ShapeDtypeStruct((TOP_K, n), F32),
                   jax.ShapeDtypeStruct((TOP_K, n), I32), jax.ShapeDtypeStruct((ne, 1), F32)],
        scratch_shapes=[pltpu.VMEM((ne, 1), F32)],
        compiler_params=_cparams(("arbitrary",)),
        name="moe_route",
    )(lg_t, rb, tri)


def _dispatch_kernel(pad_ref, dest_ref, h_ref, xs_ref, zero_s, sem, *, tm):
    i = pl.program_id(0)
    nrow = zero_s.shape[0]

    @pl.when(i == 0)
    def _():
        zero_s[...] = jnp.zeros(zero_s.shape, U32)

        def fill(start):
            cp = pltpu.make_async_copy(zero_s, xs_ref.at[pl.ds(pl.multiple_of(start, 8), nrow), :], sem)
            cp.start()
            cp.wait()

        def fill_pad(e, c):
            fill(pad_ref[e] // 8 * 8)
            return c

        lax.fori_loop(0, N_EXPERTS, fill_pad, 0)
        total = xs_ref.shape[0]
        tail = pad_ref[N_EXPERTS]

        def fill_tail(j, c):
            fill(jnp.minimum(tail + j * nrow, total - nrow))
            return c

        lax.fori_loop(0, (total - tail + nrow - 1) // nrow, fill_tail, 0)

    def issue_row(r, c):
        for k in range(TOP_K):
            d = dest_ref[r * TOP_K + k]
            pltpu.make_async_copy(h_ref.at[pl.ds(r, 1), :], xs_ref.at[pl.ds(d, 1), :], sem).start()
        return c

    lax.fori_loop(0, tm, issue_row, 0)
    pltpu.make_async_copy(xs_ref.at[pl.ds(0, tm * TOP_K), :], xs_ref.at[pl.ds(0, tm * TOP_K), :], sem).wait()


def _dispatch(pad_start, dest_flat, hp, cap):
    n, d = hp.shape
    tm = TM_MOVE
    kern = functools.partial(_dispatch_kernel, tm=tm)
    return pl.pallas_call(
        kern,
        grid_spec=pltpu.PrefetchScalarGridSpec(
            num_scalar_prefetch=1,
            grid=(n // tm,),
            in_specs=[
                pl.BlockSpec((tm * TOP_K,), lambda i, ps: (i,), memory_space=pltpu.SMEM),
                pl.BlockSpec((tm, d), lambda i, ps: (i, 0)),
            ],
            out_specs=pl.BlockSpec(memory_space=pl.ANY),
            scratch_shapes=[pltpu.VMEM((BM + 8, d), U32), pltpu.SemaphoreType.DMA(())],
        ),
        out_shape=jax.ShapeDtypeStruct((cap + BM + 8, d), U32),
        compiler_params=_cparams(("arbitrary",)),
        name="moe_dispatch",
    )(pad_start, dest_flat, hp)


def _experts_kernel(exp_ref, nused_ref, x_ref, w1_ref, w3_ref, w2_ref, y_ref):
    used = pl.program_id(0) < nused_ref[0]

    @pl.when(used)
    def _():
        y_ref[...] = _pack_rows(_swiglu_packed(x_ref[...], w1_ref, w3_ref, w2_ref))

    @pl.when(jnp.logical_not(used))
    def _():
        y_ref[...] = jnp.zeros(y_ref.shape, U32)


def _swiglu_packed(xp, w1_ref, w3_ref, w2_ref):
    half = xp.shape[1]
    hi, lo = [v.astype(BF16) for v in _unpack_rows(xp)]
    a = _dot(hi, w1_ref[0:half, :]) + _dot(lo, w1_ref[half:2 * half, :])
    b = _dot(hi, w3_ref[0:half, :]) + _dot(lo, w3_ref[half:2 * half, :])
    return _dot((a * jax.nn.sigmoid(a) * b).astype(BF16), w2_ref[...])


def _experts(blk_e, n_used, xs, w1, w3, w2):
    d = w1.shape[1]
    de = w1.shape[2]
    nb = blk_e.shape[0]
    return pl.pallas_call(
        _experts_kernel,
        grid_spec=pltpu.PrefetchScalarGridSpec(
            num_scalar_prefetch=2,
            grid=(nb,),
            in_specs=[
                pl.BlockSpec((BM, d // 2), lambda i, e, u: (i, 0)),
                pl.BlockSpec((None, d, de), lambda i, e, u: (e[i], 0, 0)),
                pl.BlockSpec((None, d, de), lambda i, e, u: (e[i], 0, 0)),
                pl.BlockSpec((None, de, d), lambda i, e, u: (e[i], 0, 0)),
            ],
            out_specs=pl.BlockSpec((BM, d // 2), lambda i, e, u: (i, 0)),
        ),
        out_shape=jax.ShapeDtypeStruct((nb * BM, d // 2), U32),
        compiler_params=_cparams(("arbitrary",)),
        name="moe_experts",
    )(blk_e, n_used, xs, w1, w3, w2)


def _combine_kernel(dcur_ref, dnxt_ref, ys_ref, g_ref, hp_ref, xp_ref, xs_ref, gtp_ref, gts_ref, s1_ref, s3_ref,
                    s2_ref, xop_ref, xos_ref, buf_s, sem, *, tm, npt):
    i = pl.program_id(0)
    slot = i % 2

    def issue(dest_ref, sl):
        def issue_row(r, c):
            for k in range(TOP_K):
                d = dest_ref[r * TOP_K + k]
                pltpu.make_async_copy(ys_ref.at[pl.ds(d, 1), :], buf_s.at[sl, k, pl.ds(r, 1), :],
                                      sem.at[sl]).start()
            return c

        lax.fori_loop(0, tm, issue_row, 0)

    @pl.when(i == 0)
    def _():
        issue(dcur_ref, 0)

    for sl in range(2):
        @pl.when(jnp.logical_and(i + 1 < pl.num_programs(0), slot != sl))
        def _():
            issue(dnxt_ref, sl)

    shared = _swiglu_packed(hp_ref[...], s1_ref, s3_ref, s2_ref)
    pltpu.make_async_copy(ys_ref.at[pl.ds(0, tm * TOP_K), :], ys_ref.at[pl.ds(0, tm * TOP_K), :],
                          sem.at[slot]).wait()
    g = g_ref[...]
    hi, lo = _unpack_rows(buf_s[slot, 0])
    r_hi = g[:, 0:1] * hi
    r_lo = g[:, 0:1] * lo
    for k in range(1, TOP_K):
        hi, lo = _unpack_rows(buf_s[slot, k])
        r_hi = r_hi + g[:, k:k + 1] * hi
        r_lo = r_lo + g[:, k:k + 1] * lo
    is_s = _sample_rows(i, npt, tm)
    routed = jnp.concatenate([r_hi, r_lo], axis=1)
    _store_tok(i, npt, xop_ref, xos_ref,
               _pick(is_s, xp_ref, xs_ref) + _pick(is_s, gtp_ref, gts_ref) * (routed + shared))


def _combine(dest_flat, ys, gates, hp, x, gt, s1, s3, s2, bp, s):
    n_p, d = x[0].shape
    n_s = x[1].shape[0]
    n = n_p + n_s
    ds_ = s1.shape[1]
    tm = TM_MOVE
    npt = n_p // tm
    nstep = n // tm
    xspecs = _tok_specs(tm, d, npt)
    kern = functools.partial(_combine_kernel, tm=tm, npt=npt)
    tok = lambda i: (i, 0)
    return pl.pallas_call(
        kern,
        grid=(nstep,),
        in_specs=[
            pl.BlockSpec((tm * TOP_K,), lambda i: (i,), memory_space=pltpu.SMEM),
            pl.BlockSpec((tm * TOP_K,), lambda i: (jnp.minimum(i + 1, nstep - 1),), memory_space=pltpu.SMEM),
            pl.BlockSpec(memory_space=pl.ANY),
            pl.BlockSpec((tm, TOP_K), tok),
            pl.BlockSpec((tm, d // 2), tok),
            *xspecs,
            *_mod_specs(tm, d, npt, s // tm, bp),
            _const_spec((d, ds_)),
            _const_spec((d, ds_)),
            _const_spec((ds_, d)),
        ],
        out_specs=xspecs,
        out_shape=[jax.ShapeDtypeStruct((n_p, d), F32), jax.ShapeDtypeStruct((n_s, d), F32)],
        scratch_shapes=[pltpu.VMEM((2, TOP_K, tm, d // 2), U32), pltpu.SemaphoreType.DMA((2,))],
        compiler_params=_cparams(("arbitrary",)),
        name="moe_combine",
    )(dest_flat, dest_flat, ys, gates, hp, *x, *gt, s1, s3, s2)


def _moe(lg_t, hp, x, gt, rb, w1, w3, w2, s1, s3, s2, tri, bp, s):
    n = hp.shape[0]
    idx_t, gate_t, rank_t, cnt = _route(lg_t, rb.reshape(N_EXPERTS, 1).astype(F32), tri)
    counts = cnt[:, 0].astype(I32)
    padded = (counts + BM - 1) // BM * BM
    pend = jnp.cumsum(padded)
    pstart = pend - padded
    nb = (n * TOP_K + N_EXPERTS * (BM - 1) + BM - 1) // BM
    n_used = pend[-1] // BM
    blk_e = jnp.minimum(jnp.sum(pend[None, :] <= (jnp.arange(nb, dtype=I32) * BM)[:, None], axis=1),
                        N_EXPERTS - 1).astype(I32)
    dest_t = jnp.sum(jnp.where(idx_t[None] == jnp.arange(N_EXPERTS, dtype=I32)[:, None, None],
                               pstart[:, None, None], 0), axis=0) + rank_t
    dest_flat = dest_t.T.reshape(-1)
    fill = jnp.concatenate([pstart + counts, pend[-1:]]).astype(I32)
    xs = _dispatch(fill, dest_flat, hp, nb * BM)
    ys = _experts(blk_e, n_used.reshape(1).astype(I32), xs, w1, w3, w2)
    return _combine(dest_flat, ys, gate_t.T, hp, x, gt, s1, s3, s2, bp, s)


def _mlstm_pre_kernel(xp_ref, xs_ref, g_ref, shp_ref, shs_ref, scp_ref, scs_ref, wxo_ref, wgh_ref, wgl_ref, bif_ref, cw_ref, cb_ref,
                      wq_ref, wk_ref, wv_ref, halo_ref,
                      q_ref, k_ref, v_ref, o_ref, xcv_ref, gates_ref, tail_ref, xcs_ref,
                      xpad_s, *, npt, tps, t_s):
    i = pl.program_id(0)
    tm, d = xp_ref.shape
    is_s = _sample_rows(i, npt, tm)
    h = (_rms(_pick(is_s, xp_ref, xs_ref), g_ref[...]) * (1.0 + _pick(is_s, scp_ref, scs_ref))
         + _pick(is_s, shp_ref, shs_ref))
    h_hi, h_lo = _split2(h)
    y = _dot(h_hi, wxo_ref[...])
    xc = y[:, :d]
    o_ref[...] = y[:, d:].astype(BF16)
    wgh = wgh_ref[...]
    gp = _dot(h_hi, wgh) + _dot(h_hi, wgl_ref[...]) + _dot(h_lo, wgh) + bif_ref[...]
    lane = lax.broadcasted_iota(I32, gp.shape, 1)
    log_sig = jnp.minimum(gp, 0.0) - jnp.log(1.0 + jnp.exp(-jnp.abs(gp)))
    gates_ref[...] = jnp.where(lane >= C_HEADS, log_sig, gp)

    is_sample = i >= npt

    @pl.when(jnp.logical_or(i % tps == 0, is_sample))
    def _():
        xpad_s[0:8, :] = jnp.zeros((8, d), F32)

    xpad_s[8:, :] = xc
    row = (lax.broadcasted_iota(I32, (tm, 1), 0) & (t_s - 1)) + jnp.where(is_sample, 0, C_CONV)
    acc = xc * cw_ref[C_CONV - 1:C_CONV, :] + cb_ref[...]
    for j in range(1, C_CONV):
        prev = xpad_s[8 - j:8 - j + tm, :]
        prev = jnp.where(row < j, halo_ref[j - 1], prev)
        acc = acc + prev * cw_ref[C_CONV - 1 - j:C_CONV - j, :]
    xpad_s[0:8, :] = xc[tm - 8:, :]
    xconv = acc * jax.nn.sigmoid(acc)
    xcv16 = xconv.astype(BF16)
    xc16 = xc.astype(BF16)
    xcv_ref[...] = xcv16
    for hh in range(C_HEADS):
        cs = slice(hh * C_HEAD_DIM, (hh + 1) * C_HEAD_DIM)
        q_ref[:, cs] = _dot(xcv16[:, cs], wq_ref[hh]).astype(BF16)
        k_ref[:, cs] = (_dot(xcv16[:, cs], wk_ref[hh]) * C_HEAD_DIM ** -0.5).astype(BF16)
        v_ref[:, cs] = _dot(xc16[:, cs], wv_ref[hh]).astype(BF16)

    @pl.when(jnp.logical_not(is_sample))
    def _():
        tail_ref[...] = xc[tm - 8:, :]

    @pl.when(is_sample)
    def _():
        xcs_ref[...] = xc


def _mlstm_pre(x, g, sh, sc, wxo, wgh, wgl, bif, cw, cb, wq, wk, wv, halo, bp, s, t_s):
    d = x[0].shape[1]
    n = x[0].shape[0] + x[1].shape[0]
    npt = bp * s // TM
    tps = s // TM
    nst = n // TM - npt
    tok = lambda i: (i, 0)
    mod = _mod_specs(TM, d, npt, tps, bp)
    kern = functools.partial(_mlstm_pre_kernel, npt=npt, tps=tps, t_s=t_s)
    b16 = jax.ShapeDtypeStruct((n, d), BF16)
    return pl.pallas_call(
        kern,
        grid=(n // TM,),
        in_specs=[
            *_tok_specs(TM, d, npt),
            _const_spec((1, d)),
            *mod,
            *mod,
            _const_spec((d, 2 * d)),
            _const_spec((d, LANES)),
            _const_spec((d, LANES)),
            _const_spec((1, LANES)),
            _const_spec((C_CONV, d)),
            _const_spec((1, d)),
            _const_spec((C_HEADS, C_HEAD_DIM, C_HEAD_DIM)),
            _const_spec((C_HEADS, C_HEAD_DIM, C_HEAD_DIM)),
            _const_spec((C_HEADS, C_HEAD_DIM, C_HEAD_DIM)),
            pl.BlockSpec((None, C_CONV - 1, TM, d), lambda i: (jnp.maximum(i - npt, 0), 0, 0, 0),
                         pipeline_mode=pl.Buffered(1)),
        ],
        out_specs=[pl.BlockSpec((TM, d), tok)] * 5 + [
            pl.BlockSpec((TM, LANES), tok),
            pl.BlockSpec((None, 8, d), lambda i: (jnp.minimum(i // tps, bp - 1), 0, 0)),
            pl.BlockSpec((TM, d), lambda i: (jnp.maximum(i - npt, 0), 0)),
        ],
        out_shape=[b16] * 5 + [
            jax.ShapeDtypeStruct((n, LANES), F32),
            jax.ShapeDtypeStruct((bp, 8, d), F32),
            jax.ShapeDtypeStruct((nst * TM, d), F32),
        ],
        scratch_shapes=[pltpu.VMEM((TM + 8, d), F32)],
        compiler_params=_cparams(("arbitrary",)),
        name="mlstm_pre",
    )(*x, g, *sh, *sc, wxo, wgh, wgl, bif, cw, cb, wq, wk, wv, halo)


def _mlstm_scan_kernel(q_ref, k_ref, v_ref, o_ref, xcv_ref, gc_ref, gr_ref, tri_ref, trit_ref, hg_ref, sk_ref,
                       mem0_ref, nrm0_ref, mx0_ref,
                       a_ref, memo_ref, nrmo_ref, mxo_ref,
                       mem_s, nrm_s, mx_s, *, nc):
    c = pl.program_id(1)
    ln = q_ref.shape[0]

    @pl.when(c == 0)
    def _():
        mem_s[...] = mem0_ref[...]
        nrm_s[...] = nrm0_ref[...]
        mx_s[...] = mx0_ref[...]

    gc = gc_ref[...]
    gr = gr_ref[...]
    tri = tri_ref[...]
    trit = trit_ref[...]
    bc = functools.reduce(lambda a, b: a + b, [_dot(tri, p) for p in _split3(gc)])
    br = functools.reduce(lambda a, b: a + b, [_dot(p, trit) for p in _split3(gr)])
    causal = lax.broadcasted_iota(I32, (ln, ln), 1) <= lax.broadcasted_iota(I32, (ln, ln), 0)
    for h in range(C_HEADS):
        cs = slice(h * C_HEAD_DIM, (h + 1) * C_HEAD_DIM)
        b_col = bc[:, C_HEADS + h:C_HEADS + h + 1]
        ig_col = gc[:, h:h + 1]
        b_row = br[C_HEADS + h:C_HEADS + h + 1, :]
        ig_row = gr[h:h + 1, :]
        b_last = b_row[:, ln - 1:ln]
        mx = mx_s[h:h + 1, 0:1]
        logw = jnp.where(causal, b_col - b_row + ig_row, NEG)
        g = b_col + mx
        m_t = jnp.maximum(g, jnp.max(logw, axis=1, keepdims=True))
        w = jnp.exp(logw - m_t)
        inter = jnp.exp(g - m_t)
        qh = q_ref[:, cs]
        kh = k_ref[:, cs]
        vh = v_ref[:, cs]
        a = w * _dot_nt(qh, kh)
        mem = mem_s[h]
        nrm = nrm_s[h:h + 1, :]
        num = _dot(a.astype(BF16), vh) + inter * _dot(qh, mem.astype(BF16))
        den = jnp.sum(a, axis=1, keepdims=True) + inter * jnp.sum(qh.astype(F32) * nrm, axis=1, keepdims=True)
        hout = num / jnp.maximum(jnp.abs(den), jnp.exp(-m_t))
        logs = b_last - b_col + ig_col
        m_new = jnp.maximum(b_last + mx, jnp.max(logs, axis=0, keepdims=True))
        decay = jnp.exp(b_last + mx - m_new)
        kw = kh.astype(F32) * jnp.exp(logs - m_new)
        mem_s[h] = decay * mem + _dot_tn(kw.astype(BF16), vh)
        nrm_s[h:h + 1, :] = decay * nrm + jnp.sum(kw, axis=0, keepdims=True)
        mx_s[h:h + 1, :] = jnp.broadcast_to(m_new, (1, mx_s.shape[1]))
        hh = hout * jax.nn.sigmoid(o_ref[:, cs].astype(F32))
        a_ref[:, cs] = (_rms(hh, hg_ref[:, cs]) + sk_ref[:, cs] * xcv_ref[:, cs].astype(F32)).astype(BF16)

    @pl.when(c == nc - 1)
    def _():
        memo_ref[...] = mem_s[...]
        nrmo_ref[...] = nrm_s[...]
        mxo_ref[...] = mx_s[...]


def _mlstm_scan(q, k, v, o, xcv, gates, hg, sk, mem0, nrm0, mx0, row0, nb, nc, ln):
    d = q.shape[1]
    nrow = nb * nc * ln
    gsl = lax.slice_in_dim(gates, row0 * ln, row0 * ln + nrow, axis=0)[:, :16]
    gr = jnp.transpose(gsl.reshape(nb * nc, ln, 16), (0, 2, 1))
    r = jnp.arange(ln)
    tri = (r[None, :] <= r[:, None]).astype(BF16)
    chunk = lambda b, c: (row0 + b * nc + c, 0)
    seq4 = lambda b, c: (b, 0, 0, 0)
    seq3 = lambda b, c: (b, 0, 0)
    kern = functools.partial(_mlstm_scan_kernel, nc=nc)
    return pl.pallas_call(
        kern,
        grid=(nb, nc),
        in_specs=[pl.BlockSpec((ln, d), chunk)] * 5 + [
            pl.BlockSpec((ln, LANES), chunk),
            pl.BlockSpec((None, 16, ln), lambda b, c: (b * nc + c, 0, 0)),
            _const_spec((ln, ln)),
            _const_spec((ln, ln)),
            _const_spec((1, d)),
            _const_spec((1, d)),
            pl.BlockSpec((None, C_HEADS, C_HEAD_DIM, C_HEAD_DIM), seq4),
            pl.BlockSpec((None, 8, C_HEAD_DIM), seq3),
            pl.BlockSpec((None, 8, LANES), seq3),
        ],
        out_specs=[
            pl.BlockSpec((ln, d), lambda b, c: (b * nc + c, 0)),
            pl.BlockSpec((None, C_HEADS, C_HEAD_DIM, C_HEAD_DIM), seq4),
            pl.BlockSpec((None, 8, C_HEAD_DIM), seq3),
            pl.BlockSpec((None, 8, LANES), seq3),
        ],
        out_shape=[
            jax.ShapeDtypeStruct((nrow, d), BF16),
            jax.ShapeDtypeStruct((nb, C_HEADS, C_HEAD_DIM, C_HEAD_DIM), F32),
            jax.ShapeDtypeStruct((nb, 8, C_HEAD_DIM), F32),
            jax.ShapeDtypeStruct((nb, 8, LANES), F32),
        ],
        scratch_shapes=[
            pltpu.VMEM((C_HEADS, C_HEAD_DIM, C_HEAD_DIM), F32),
            pltpu.VMEM((8, C_HEAD_DIM), F32),
            pltpu.VMEM((8, LANES), F32),
        ],
        compiler_params=_cparams(("parallel", "arbitrary")),
        name="mlstm_scan",
    )(q, k, v, o, xcv, gates, gr, tri, tri.T, hg, sk, mem0, nrm0, mx0)


def _pad_heads(a, width):
    nb = a.shape[0]
    if a.ndim == 2:
        a = jnp.broadcast_to(a[:, :, None], (nb, C_HEADS, width))
    return jnp.concatenate([a.astype(F32), jnp.zeros((nb, 8 - C_HEADS, width), F32)], axis=1)


def kernel(x_prompt, x_sample, c_prompt, c_sample, cache_a_k, cache_a_v, cache_b_k, cache_b_v, state_c_mem, state_c_norm, state_c_max, state_c_conv, norm_g, w_mod, b_mod, t5_bias, ab_w_in, ab_qk_g, ab_lambda, ab_head_g, ab_rel_bias, ab_w_out, c_w_in, c_b_if, c_conv_w, c_conv_b, c_w_qkv, c_head_g, c_skip, c_w_out, router_w, router_b, exp_w1, exp_w3, exp_w2, sh_w1, sh_w3, sh_w2):
    bp, s, d = x_prompt.shape
    bs, t = x_sample.shape[:2]
    depth = norm_g.shape[0]
    past = cache_a_k.shape[2]
    lb = cache_b_k.shape[2]
    n_p = bp * s
    n_s = bs * t
    n_all = n_p + n_s
    assert s % TM == 0 and n_s % TM == 0 and TM % t == 0 and t & (t - 1) == 0
    assert s % TQ_DIFF == 0 and s % ML_CHUNK == 0 and s >= BAND_PAST + TQ_BAND and TQ_DIFF >= T5_MAX_DIST
    assert past % CHUNK == 0 and lb == BAND_PAST and t <= CHUNK and t >= C_CONV - 1 and TM == BAND_PAST

    def per_token(vec):
        return vec[:bp].reshape(bp, 1, d), jnp.repeat(vec[bp:], t, axis=0)

    x = (x_prompt.reshape(n_p, d), x_sample.reshape(n_s, d))
    c_all = jnp.concatenate([c_prompt, c_sample], axis=0)
    mods = _modulation(c_all, w_mod, b_mod)

    r = jnp.arange(TM)
    tri_route = (r[:, None] < r[None, :]).astype(BF16)
    hd = jnp.arange(A_HEADS * 2 * HEAD_DIM) // HEAD_DIM
    bd = (hd[:, None] == hd[None, :]).astype(BF16)

    leaves = {}
    for l in range(depth):
        m6 = [mods[l][:, j * d:(j + 1) * d] for j in range(6)]
        sh1, sc1, gt1, sh2, sc2, gt2 = [per_token(v) for v in m6]
        i = l // 2
        if l % 2 == 0:
            lam_init = 0.8 - 0.6 * math.exp(-0.3 * l)
            lp = ab_lambda[i].astype(F32)
            lam = (jnp.exp(jnp.sum(lp[0] * lp[1])) - jnp.exp(jnp.sum(lp[2] * lp[3])) + lam_init).reshape(1)
            qkg_t = jnp.tile(ab_qk_g[i].astype(F32), (1, A_HEADS * 2))
            p16, ka_p, ka_s, va_p, va_s, kb_p, kb_s, vb_p, vb_s = _ab_in_proj(
                x, norm_g[l, 0].reshape(1, d), sh1, sc1, ab_w_in[i].astype(BF16), qkg_t, bd, bp, s)
            hg = ab_head_g[i].reshape(1, 2 * HEAD_DIM).astype(F32)
            out_scale = 1.0 - lam_init
            tiles, far = _diff_bias_prompt(t5_bias, TQ_DIFF)
            oa_p = _diff_attn_prompt(p16, tiles, far, lam, hg.reshape(2 * HEAD_DIM, 1), bp, s, out_scale)
            ob_p = _band_attn_prompt(p16, _band_bias_prompt(ab_rel_bias[i], TQ_BAND), bp, s)
            dbc, dbn = _diff_bias_sample(t5_bias, past, t)
            oa_s = _sample_attn(p16, cache_a_k[i].reshape(bs, past, -1), cache_a_v[i].reshape(bs, past, -1),
                                dbc, dbn, lam, hg, n_p, bs, t, True, out_scale)
            bbc, bbn = _band_bias_sample(ab_rel_bias[i], past, lb, t)
            ob_s = _sample_attn(p16, cache_b_k[i].reshape(bs, lb, -1), cache_b_v[i].reshape(bs, lb, -1),
                                bbc, bbn, lam, hg, n_p, bs, t, False, out_scale)
            oa = jnp.concatenate([oa_p, oa_s.reshape(n_s, -1)], axis=0)
            ob = jnp.concatenate([ob_p, ob_s.reshape(n_s, -1)], axis=0)
            mix_in = (oa, 0, ob, 0)
            w_out16 = ab_w_out[i].astype(BF16)
            leaves.setdefault('akp', []).append(ka_p.reshape(bp, s, A_HEADS, 2, HEAD_DIM))
            leaves.setdefault('avp', []).append(va_p.reshape(bp, s, A_HEADS, 2 * HEAD_DIM))
            leaves.setdefault('aks', []).append(ka_s.reshape(bs, t, A_HEADS, 2, HEAD_DIM))
            leaves.setdefault('avs', []).append(va_s.reshape(bs, t, A_HEADS, 2 * HEAD_DIM))
            leaves.setdefault('bkp', []).append(kb_p.reshape(bp, TM, B_HEADS, HEAD_DIM))
            leaves.setdefault('bvp', []).append(vb_p.reshape(bp, TM, B_HEADS, HEAD_DIM))
            leaves.setdefault('bks', []).append(kb_s.reshape(bs, t, B_HEADS, HEAD_DIM))
            leaves.setdefault('bvs', []).append(vb_s.reshape(bs, t, B_HEADS, HEAD_DIM))
        else:
            w_in = c_w_in[i]
            wg = jnp.pad(w_in[:, 2 * d:].astype(F32), ((0, 0), (0, LANES - 2 * C_HEADS)))
            wgh = wg.astype(BF16)
            wgl = (wg - wgh.astype(F32)).astype(BF16)
            bif = jnp.pad(c_b_if[i].astype(F32).reshape(1, 2 * C_HEADS), ((0, 0), (0, LANES - 2 * C_HEADS)))
            cprev = state_c_conv[i].astype(F32)
            planes = []
            for j in range(1, C_CONV):
                rows = jnp.concatenate([cprev[:, C_CONV - 1 - j:, :], jnp.zeros((bs, t - j, d), F32)], axis=1)
                planes.append(rows.reshape(n_s // TM, TM, d))
            halo = jnp.stack(planes, axis=1)
            wqkv = c_w_qkv[i].astype(BF16)
            q, k, v, o, xcv, gates, tail, xcs = _mlstm_pre(
                x, norm_g[l, 0].reshape(1, d), sh1, sc1, w_in[:, :2 * d].astype(BF16), wgh, wgl, bif,
                c_conv_w[i].astype(F32), c_conv_b[i].reshape(1, d).astype(F32), wqkv[0], wqkv[1], wqkv[2],
                halo, bp, s, t)
            hg = c_head_g[i].reshape(1, d).astype(F32)
            sk = c_skip[i].reshape(1, d).astype(F32)
            zm = jnp.zeros((bp, C_HEADS, C_HEAD_DIM, C_HEAD_DIM), F32)
            a_p, mem_p, nrm_p, mx_p = _mlstm_scan(
                q, k, v, o, xcv, gates, hg, sk, zm, jnp.zeros((bp, 8, C_HEAD_DIM), F32),
                jnp.zeros((bp, 8, LANES), F32), 0, bp, s // ML_CHUNK, ML_CHUNK)
            a_s, mem_s, nrm_s, mx_s = _mlstm_scan(
                q, k, v, o, xcv, gates, hg, sk, state_c_mem[i].astype(F32),
                _pad_heads(state_c_norm[i], C_HEAD_DIM), _pad_heads(state_c_max[i], LANES),
                n_p // t, bs, 1, t)
            a_all = jnp.concatenate([a_p, a_s], axis=0)
            mix_in = (a_all, 0, a_all, 1)
            w_out16 = c_w_out[i].astype(BF16)
            leaves.setdefault('memp', []).append(mem_p)
            leaves.setdefault('normp', []).append(nrm_p[:, :C_HEADS])
            leaves.setdefault('maxp', []).append(mx_p[:, :C_HEADS, 0])
            leaves.setdefault('convp', []).append(tail[:, 8 - (C_CONV - 1):])
            leaves.setdefault('mems', []).append(mem_s)
            leaves.setdefault('norms', []).append(nrm_s[:, :C_HEADS])
            leaves.setdefault('maxs', []).append(mx_s[:, :C_HEADS, 0])
            leaves.setdefault('convs', []).append(xcs.reshape(bs, t, d)[:, t - (C_CONV - 1):])
        rw_t = router_w[l].astype(F32).T
        rw_hi = rw_t.astype(BF16)
        rw_lo = (rw_t - rw_hi.astype(F32)).astype(BF16)
        xp, xs, hp, lg_t = _out_proj(mix_in[0], mix_in[1], mix_in[2], mix_in[3], w_out16, x, gt1,
                                     norm_g[l, 1].reshape(1, d), sh2, sc2, rw_hi, rw_lo, bp, s)
        x = _moe(lg_t, hp, (xp, xs), gt2, router_b[l], exp_w1[l].astype(BF16), exp_w3[l].astype(BF16),
                     exp_w2[l].astype(BF16), sh_w1[l].astype(BF16), sh_w3[l].astype(BF16),
                     sh_w2[l].astype(BF16), tri_route, bp, s)

    order = ['akp', 'avp', 'aks', 'avs', 'bkp', 'bvp', 'bks', 'bvs',
             'memp', 'normp', 'maxp', 'convp', 'mems', 'norms', 'maxs', 'convs']
    return (x[0].reshape(bp, s, d), x[1].reshape(bs, t, d)) + tuple(
        jnp.stack(leaves[name]) for name in order)
```

```python
import functools
import math

import jax
import jax.numpy as jnp
from jax import lax
from jax.experimental import pallas as pl
from jax.experimental.pallas import tpu as pltpu

F32 = jnp.float32
BF16 = jnp.bfloat16
I32 = jnp.int32
U32 = jnp.uint32

EPS = 1e-6
NEG = -1e30
CHUNK = 64
HEAD_DIM = 64
A_HEADS = 4
B_HEADS = 8
BAND_CHUNKS = 8
BAND_PAST = BAND_CHUNKS * CHUNK
REL_CLIP = 128
T5_BUCKETS = 32
T5_MAX_DIST = 128
C_HEADS = 4
C_HEAD_DIM = 256
C_CONV = 4
N_EXPERTS = 64
TOP_K = 8
ROUTE_SCALE = 2.5

LANES = 128
TM = 512
TQ_DIFF = 512
TQ_BAND = 128
ML_CHUNK = 256
BM = 512
TM_MOVE = 256
VMEM_LIMIT = 56 * 1024 * 1024


def _cparams(sem):
    return pltpu.CompilerParams(dimension_semantics=sem, vmem_limit_bytes=VMEM_LIMIT)


def _dot(a, b):
    return jnp.dot(a, b, preferred_element_type=F32)


def _dot_nt(a, b):
    return lax.dot_general(a, b, (((1,), (1,)), ((), ())), preferred_element_type=F32)


def _dot_tn(a, b):
    return lax.dot_general(a, b, (((0,), (0,)), ((), ())), preferred_element_type=F32)


def _split2(x):
    hi = x.astype(BF16)
    lo = (x - hi.astype(F32)).astype(BF16)
    return hi, lo


def _split3(x):
    p0 = x.astype(BF16)
    r1 = x - p0.astype(F32)
    p1 = r1.astype(BF16)
    p2 = (r1 - p1.astype(F32)).astype(BF16)
    return p0, p1, p2


def _pack_rows(x):
    half = x.shape[1] // 2
    bits = lax.bitcast_convert_type(x.astype(BF16).astype(F32), U32)
    return bits[:, :half] | (bits[:, half:] >> 16)


def _unpack_rows(p):
    return (lax.bitcast_convert_type(p & jnp.uint32(0xFFFF0000), F32),
            lax.bitcast_convert_type(p << 16, F32))


def _rms(x, g):
    return x * lax.rsqrt(jnp.mean(x * x, axis=-1, keepdims=True) + EPS) * g


def _mod_specs(tile, d, npt, tps, bp):
    return [pl.BlockSpec((None, 1, d), lambda i: (jnp.minimum(i // tps, bp - 1), 0, 0)),
            pl.BlockSpec((tile, d), lambda i: (jnp.maximum(i - npt, 0), 0), pipeline_mode=pl.Buffered(1))]


def _tok_specs(tile, width, npt, col=0):
    return [pl.BlockSpec((tile, width), lambda i: (jnp.minimum(i, npt - 1), col)),
            pl.BlockSpec((tile, width), lambda i: (jnp.maximum(i - npt, 0), col))]


def _store_tok(i, npt, p_ref, s_ref, val):
    @pl.when(i < npt)
    def _():
        p_ref[...] = val

    @pl.when(i >= npt)
    def _():
        s_ref[...] = val


def _sample_rows(i, npt, tm):
    return lax.broadcasted_iota(I32, (tm, 1), 0) >= jnp.where(i >= npt, 0, tm)


def _pick(is_s, vp_ref, vs_ref):
    return jnp.where(is_s, vs_ref[...], vp_ref[...])


def _const_spec(shape):
    nd = len(shape)
    return pl.BlockSpec(shape, lambda *_: (0,) * nd, pipeline_mode=pl.Buffered(1))


def _mod_kernel(c_ref, w_ref, b_ref, o_ref):
    c = c_ref[...]
    a_hi, a_lo = _split2(c * jax.nn.sigmoid(c))
    w_hi, w_lo = _split2(w_ref[...])
    o_ref[...] = _dot(a_hi, w_hi) + _dot(a_hi, w_lo) + _dot(a_lo, w_hi) + b_ref[...]


def _modulation(c_all, w_mod, b_mod):
    depth, d, n6 = w_mod.shape
    nseq = c_all.shape[0]
    tn = 512
    return pl.pallas_call(
        _mod_kernel,
        grid=(depth, n6 // tn),
        in_specs=[
            pl.BlockSpec((nseq, d), lambda l, j: (0, 0)),
            pl.BlockSpec((None, d, tn), lambda l, j: (l, 0, j)),
            pl.BlockSpec((None, 1, tn), lambda l, j: (l, 0, j)),
        ],
        out_specs=pl.BlockSpec((None, nseq, tn), lambda l, j: (l, 0, j)),
        out_shape=jax.ShapeDtypeStruct((depth, nseq, n6), F32),
        compiler_params=_cparams(("parallel", "parallel")),
        name="modulation",
    )(c_all, w_mod, b_mod.reshape(depth, 1, n6))


def _ab_in_kernel(xp_ref, xs_ref, g_ref, shp_ref, shs_ref, scp_ref, scs_ref, w_ref, qkg_ref, bd_ref,
                  p16_ref, kap_ref, kas_ref, vap_ref, vas_ref, kbp_ref, kbs_ref, vbp_ref, vbs_ref, *, npt, tps):
    i = pl.program_id(0)
    is_s = _sample_rows(i, npt, xp_ref.shape[0])
    h = (_rms(_pick(is_s, xp_ref, xs_ref), g_ref[...]) * (1.0 + _pick(is_s, scp_ref, scs_ref))
         + _pick(is_s, shp_ref, shs_ref))
    y = _dot(h.astype(BF16), w_ref[...])
    bd = bd_ref[...]
    wa = A_HEADS * 2 * HEAD_DIM

    def group_norm(seg, gi):
        hi, lo = _split2(seg * seg)
        ss = _dot(hi, bd) + _dot(lo, bd)
        return seg * lax.rsqrt(ss * (1.0 / HEAD_DIM) + EPS) * qkg_ref[gi:gi + 1, :]

    qa = group_norm(y[:, 0 * wa:1 * wa], 0)
    ka = group_norm(y[:, 1 * wa:2 * wa], 1)
    va = y[:, 2 * wa:3 * wa]
    qb = group_norm(y[:, 3 * wa:4 * wa], 2)
    kb = group_norm(y[:, 4 * wa:5 * wa], 3)
    vb = y[:, 5 * wa:6 * wa]
    scale = HEAD_DIM ** -0.5
    p16_ref[:, 0 * wa:1 * wa] = (qa * scale).astype(BF16)
    p16_ref[:, 1 * wa:2 * wa] = ka.astype(BF16)
    p16_ref[:, 2 * wa:3 * wa] = va.astype(BF16)
    p16_ref[:, 3 * wa:4 * wa] = (qb * scale).astype(BF16)
    p16_ref[:, 4 * wa:5 * wa] = kb.astype(BF16)
    p16_ref[:, 5 * wa:6 * wa] = vb.astype(BF16)
    _store_tok(i, npt, kap_ref, kas_ref, ka)
    _store_tok(i, npt, vap_ref, vas_ref, va)

    @pl.when(jnp.logical_and(i < npt, i % tps == tps - 1))
    def _():
        kbp_ref[...] = kb
        vbp_ref[...] = vb

    @pl.when(i >= npt)
    def _():
        kbs_ref[...] = kb
        vbs_ref[...] = vb


def _ab_in_proj(x, g, sh, sc, w16, qkg_t, bd, bp, s):
    n_p, d = x[0].shape
    n_s = x[1].shape[0]
    n = n_p + n_s
    wa = A_HEADS * 2 * HEAD_DIM
    n_in = w16.shape[1]
    tok = lambda i: (i, 0)
    npt = n_p // TM
    tps = s // TM
    mod = _mod_specs(TM, d, npt, tps, bp)
    leaf = _tok_specs(TM, wa, npt)
    band = [pl.BlockSpec((None, TM, wa), lambda i: (jnp.minimum(i // tps, bp - 1), 0, 0)), leaf[1]]
    f32 = lambda rows: jax.ShapeDtypeStruct((rows, wa), F32)
    band_shape = [jax.ShapeDtypeStruct((bp, TM, wa), F32), f32(n_s)]
    return pl.pallas_call(
        functools.partial(_ab_in_kernel, npt=npt, tps=tps),
        grid=(n // TM,),
        in_specs=[
            *_tok_specs(TM, d, npt),
            _const_spec((1, d)),
            *mod,
            *mod,
            _const_spec((d, n_in)),
            _const_spec((4, wa)),
            _const_spec((wa, wa)),
        ],
        out_specs=[pl.BlockSpec((TM, n_in), tok)] + leaf + leaf + band + band,
        out_shape=[jax.ShapeDtypeStruct((n, n_in), BF16), f32(n_p), f32(n_s), f32(n_p), f32(n_s)]
        + band_shape + band_shape,
        compiler_params=_cparams(("arbitrary",)),
        name="ab_in_proj",
    )(*x, g, *sh, *sc, w16, qkg_t, bd)


def _t5_bucket(rel):
    half = T5_BUCKETS // 2
    exact = half // 2
    n = jnp.abs(rel)
    large = exact + (jnp.log(jnp.maximum(n, 1).astype(F32) / exact)
                     / math.log(T5_MAX_DIST / exact) * (half - exact)).astype(I32)
    large = jnp.minimum(large, half - 1)
    return jnp.where(rel > 0, half, 0) + jnp.where(n < exact, n, large)


def _lookup(table, idx):
    onehot = (idx[..., None] == jnp.arange(table.shape[0], dtype=I32)).astype(F32)
    return jnp.einsum('...n,nh->...h', onehot, table.astype(F32), precision=lax.Precision.HIGHEST)


def _diff_bias_prompt(t5_bias, tq):
    i = jnp.arange(tq)[None, :]
    j = jnp.arange(tq)[:, None]
    diag = jnp.where(((j // CHUNK) <= (i // CHUNK))[..., None], _lookup(t5_bias, _t5_bucket(j - i)), NEG)
    prev = _lookup(t5_bias, _t5_bucket(j - i - tq))
    first = jnp.concatenate([diag, jnp.full_like(diag, NEG)], axis=0)
    later = jnp.concatenate([prev, diag], axis=0)
    tiles = jnp.transpose(jnp.stack([first, later]), (3, 0, 1, 2))
    tiles = jnp.concatenate([tiles, tiles], axis=3)
    far = _lookup(t5_bias, _t5_bucket(jnp.full((1,), -T5_MAX_DIST, I32)))[0]
    return tiles, far


def _diff_bias_sample(t5_bias, past, t):
    qpos = past + jnp.arange(t)
    kpos = jnp.arange(past + t)
    rel = kpos[None, :] - qpos[:, None]
    vis = (kpos[None, :] // CHUNK) <= (qpos[:, None] // CHUNK)
    b = jnp.where(vis[..., None], _lookup(t5_bias, _t5_bucket(rel)), NEG)
    b = jnp.transpose(b, (2, 0, 1))
    b = jnp.concatenate([b, b], axis=1)
    return b[:, :, :past], b[:, :, past:]


def _band_bias_prompt(rel_bias, tq):
    nvar = BAND_PAST // tq + 1
    win = BAND_PAST + tq
    u = jnp.arange(nvar)[:, None, None]
    i = jnp.arange(tq)[None, :, None]
    j = jnp.arange(win)[None, None, :]
    qp = u * tq + i
    qc = qp // CHUNK
    kc = j // CHUNK
    valid = (kc <= qc) & (kc >= qc - BAND_CHUNKS)
    b = _lookup(rel_bias, jnp.clip(j - qp, -REL_CLIP, REL_CLIP) + REL_CLIP)
    b = jnp.where(valid[..., None], b, NEG)
    b = jnp.transpose(b, (0, 3, 1, 2))
    return b.reshape(nvar, B_HEADS // 2, 2 * tq, win)


def _band_bias_sample(rel_bias, past, lb, t):
    qpos = past + jnp.arange(t)
    kpos = past - lb + jnp.arange(lb + t)
    band_lo = (past // CHUNK - BAND_CHUNKS) * CHUNK
    rel = jnp.clip(kpos[None, :] - qpos[:, None], -REL_CLIP, REL_CLIP) + REL_CLIP
    b = jnp.where((kpos >= band_lo)[None, :, None], _lookup(rel_bias, rel), NEG)
    b = jnp.transpose(b, (2, 0, 1)).reshape(B_HEADS // 2, 2 * t, lb + t)
    return b[:, :, :lb], b[:, :, lb:]


def _stack_halves(q):
    lane = lax.broadcasted_iota(I32, q.shape, 1)
    zero = jnp.zeros_like(q)
    return jnp.concatenate([jnp.where(lane < HEAD_DIM, q, zero), jnp.where(lane >= HEAD_DIM, q, zero)], axis=0)


def _diff_finish(o1, o2, lam, hg, out_scale):
    o = o1 - lam * o2
    return (_rms(o, hg) * out_scale).astype(BF16)


def _band_finish(o):
    tq = o.shape[0] // 2
    lane = lax.broadcasted_iota(I32, (tq, o.shape[1]), 1)
    return jnp.where(lane < HEAD_DIM, o[:tq], o[tq:]).astype(BF16)


def _diff_prompt_kernel(far_ref, lam_ref, q_ref, k_ref, v_ref, bias_ref, hg_ref, o_ref,
                        q2t_s, vt_s, m_s, l_s, acc_s, *, tq, out_scale):
    h = pl.program_id(1)
    i = pl.program_id(2)

    @pl.when(i == 0)
    def _():
        for jj in range(vt_s.shape[0]):
            vt_s[jj] = v_ref[jj * tq:(jj + 1) * tq, :].astype(F32).T.astype(BF16)

    qt = q_ref[...].astype(F32).T.astype(BF16)
    row = lax.broadcasted_iota(I32, qt.shape, 0)
    zero = jnp.zeros_like(qt)
    q2t_s[:, 0:tq] = jnp.where(row < HEAD_DIM, qt, zero)
    q2t_s[:, tq:2 * tq] = jnp.where(row >= HEAD_DIM, qt, zero)
    m_s[...] = jnp.full(m_s.shape, NEG, F32)
    l_s[...] = jnp.zeros(l_s.shape, F32)
    acc_s[...] = jnp.zeros(acc_s.shape, F32)

    def step(jb, nblk, bias, shift):
        kb = k_ref[pl.ds(pl.multiple_of(jb * tq, tq), nblk * tq), :]
        s = _dot(kb, q2t_s[...])
        if bias is not None:
            s = s + bias
        cmax = jnp.max(s, axis=0, keepdims=True)
        if shift is not None:
            cmax = cmax + shift
        m_prev = m_s[...]
        m_new = jnp.maximum(m_prev, cmax)
        alpha = jnp.exp(m_prev - m_new)
        p = jnp.exp(s - (m_new if shift is None else m_new - shift))
        l_s[...] = alpha * l_s[...] + jnp.sum(p, axis=0, keepdims=True)
        pb = p.astype(BF16)
        pv = _dot(vt_s[jb], pb[0:tq])
        for u in range(1, nblk):
            pv = pv + _dot(vt_s[jb + u], pb[u * tq:(u + 1) * tq])
        acc_s[...] = alpha * acc_s[...] + pv
        m_s[...] = m_new

    far = far_ref[h]
    nfar = jnp.maximum(i - 1, 0)

    def far_pair(j, carry):
        step(2 * j, 2, None, far)
        return carry

    lax.fori_loop(0, nfar // 2, far_pair, 0)

    @pl.when(nfar % 2 == 1)
    def _():
        step(nfar - 1, 1, None, far)

    step(nfar, 2, bias_ref[...], None)
    o = acc_s[...] / l_s[...]
    od = o[:, 0:tq] - lam_ref[0] * o[:, tq:2 * tq]
    on = od * lax.rsqrt(jnp.mean(od * od, axis=0, keepdims=True) + EPS) * hg_ref[...] * out_scale
    o_ref[...] = on.T.astype(BF16)


def _diff_attn_prompt(p16, tiles, far, lam, hg_col, bp, s, out_scale):
    tq = TQ_DIFF
    nq = s // tq
    wa = A_HEADS * LANES
    kern = functools.partial(_diff_prompt_kernel, tq=tq, out_scale=out_scale)
    return pl.pallas_call(
        kern,
        grid=(bp, A_HEADS, nq),
        in_specs=[
            pl.BlockSpec(memory_space=pltpu.SMEM),
            pl.BlockSpec(memory_space=pltpu.SMEM),
            pl.BlockSpec((tq, LANES), lambda b, h, i: (b * nq + i, h)),
            pl.BlockSpec((s, LANES), lambda b, h, i: (b, A_HEADS + h)),
            pl.BlockSpec((s, LANES), lambda b, h, i: (b, 2 * A_HEADS + h)),
            pl.BlockSpec((None, None, 2 * tq, 2 * tq), lambda b, h, i: (h, jnp.minimum(i, 1), 0, 0)),
            _const_spec((LANES, 1)),
        ],
        out_specs=pl.BlockSpec((tq, LANES), lambda b, h, i: (b * nq + i, h)),
        out_shape=jax.ShapeDtypeStruct((bp * s, wa), BF16),
        scratch_shapes=[
            pltpu.VMEM((LANES, 2 * tq), BF16),
            pltpu.VMEM((nq, LANES, tq), BF16),
            pltpu.VMEM((1, 2 * tq), F32),
            pltpu.VMEM((1, 2 * tq), F32),
            pltpu.VMEM((LANES, 2 * tq), F32),
        ],
        compiler_params=_cparams(("parallel", "parallel", "arbitrary")),
        name="diff_attn_prompt",
    )(far, lam, p16, p16, p16, tiles, hg_col)


def _band_prompt_kernel(q_ref, k_ref, v_ref, bias_ref, o_ref, *, tq, win):
    t = pl.program_id(2)
    start = pl.multiple_of(jnp.maximum(t * tq - BAND_PAST, 0), tq)
    kb = k_ref[pl.ds(start, win), :]
    vb = v_ref[pl.ds(start, win), :]
    s = _dot_nt(_stack_halves(q_ref[...]), kb) + bias_ref[...]
    m = jnp.max(s, axis=-1, keepdims=True)
    p = jnp.exp(s - m)
    l = jnp.sum(p, axis=-1, keepdims=True)
    o_ref[...] = _band_finish(_dot(p.astype(BF16), vb) / l)


def _band_attn_prompt(p16, bias, bp, s):
    tq = TQ_BAND
    nq = s // tq
    win = BAND_PAST + tq
    nvar = bias.shape[0]
    npair = B_HEADS // 2
    c0 = 3 * A_HEADS
    kern = functools.partial(_band_prompt_kernel, tq=tq, win=win)
    return pl.pallas_call(
        kern,
        grid=(bp, npair, nq),
        in_specs=[
            pl.BlockSpec((tq, LANES), lambda b, p, t: (b * nq + t, c0 + p)),
            pl.BlockSpec((s, LANES), lambda b, p, t: (b, c0 + npair + p)),
            pl.BlockSpec((s, LANES), lambda b, p, t: (b, c0 + 2 * npair + p)),
            pl.BlockSpec((None, None, 2 * tq, win), lambda b, p, t: (jnp.minimum(t, nvar - 1), p, 0, 0)),
        ],
        out_specs=pl.BlockSpec((tq, LANES), lambda b, p, t: (b * nq + t, p)),
        out_shape=jax.ShapeDtypeStruct((bp * s, npair * LANES), BF16),
        compiler_params=_cparams(("parallel", "parallel", "parallel")),
        name="band_attn_prompt",
    )(p16, p16, p16, bias)


def _sample_attn_kernel(lam_ref, q_ref, kc_ref, vc_ref, kn_ref, vn_ref, bc_ref, bn_ref, hg_ref, o_ref,
                        *, diff, out_scale):
    t = q_ref.shape[0]
    q2 = _stack_halves(q_ref[...])
    sc = _dot_nt(q2, kc_ref[...].astype(BF16)) + bc_ref[...]
    sn = _dot_nt(q2, kn_ref[...]) + bn_ref[...]
    m = jnp.maximum(jnp.max(sc, axis=-1, keepdims=True), jnp.max(sn, axis=-1, keepdims=True))
    pc = jnp.exp(sc - m)
    pn = jnp.exp(sn - m)
    l = jnp.sum(pc, axis=-1, keepdims=True) + jnp.sum(pn, axis=-1, keepdims=True)
    o = (_dot(pc.astype(BF16), vc_ref[...].astype(BF16)) + _dot(pn.astype(BF16), vn_ref[...])) / l
    if diff:
        o_ref[...] = _diff_finish(o[:t], o[t:], lam_ref[0], hg_ref[...], out_scale)
    else:
        o_ref[...] = _band_finish(o)


def _sample_attn(p16, cache_k, cache_v, bias_c, bias_n, lam, hg, np_rows, bs, t, diff, out_scale):
    past = cache_k.shape[1]
    ncol = cache_k.shape[2] // LANES
    row0 = np_rows // t
    if diff:
        qc, kc, vc = 0, A_HEADS, 2 * A_HEADS
    else:
        qc, kc, vc = 3 * A_HEADS, 3 * A_HEADS + ncol, 3 * A_HEADS + 2 * ncol
    kern = functools.partial(_sample_attn_kernel, diff=diff, out_scale=out_scale)
    return pl.pallas_call(
        kern,
        grid=(bs, ncol),
        in_specs=[
            pl.BlockSpec(memory_space=pltpu.SMEM),
            pl.BlockSpec((t, LANES), lambda b, h: (row0 + b, qc + h)),
            pl.BlockSpec((None, past, LANES), lambda b, h: (b, 0, h)),
            pl.BlockSpec((None, past, LANES), lambda b, h: (b, 0, h)),
            pl.BlockSpec((t, LANES), lambda b, h: (row0 + b, kc + h)),
            pl.BlockSpec((t, LANES), lambda b, h: (row0 + b, vc + h)),
            pl.BlockSpec((None, 2 * t, past), lambda b, h: (h, 0, 0)),
            pl.BlockSpec((None, 2 * t, t), lambda b, h: (h, 0, 0)),
            _const_spec((1, LANES)),
        ],
        out_specs=pl.BlockSpec((None, t, LANES), lambda b, h: (b, 0, h)),
        out_shape=jax.ShapeDtypeStruct((bs, t, ncol * LANES), BF16),
        compiler_params=_cparams(("parallel", "parallel")),
        name="diff_attn_sample" if diff else "band_attn_sample",
    )(lam, p16, cache_k, cache_v, p16, p16, bias_c, bias_n, hg)


def _out_proj_kernel(a0p_ref, a0s_ref, a1p_ref, a1s_ref, w_ref, xp_ref, xs_ref, gtp_ref, gts_ref, g2_ref,
                     shp_ref, shs_ref, scp_ref, scs_ref, rwh_ref, rwl_ref, xop_ref, xos_ref, hp_ref, lg_ref,
                     *, npt):
    half = a0p_ref.shape[1]
    i = pl.program_id(0)
    is_s = _sample_rows(i, npt, xp_ref.shape[0])
    mix = (_dot(_pick(is_s, a0p_ref, a0s_ref), w_ref[0:half, :])
           + _dot(_pick(is_s, a1p_ref, a1s_ref), w_ref[half:2 * half, :]))
    x = _pick(is_s, xp_ref, xs_ref) + _pick(is_s, gtp_ref, gts_ref) * mix
    _store_tok(i, npt, xop_ref, xos_ref, x)
    h2 = _rms(x, g2_ref[...]) * (1.0 + _pick(is_s, scp_ref, scs_ref)) + _pick(is_s, shp_ref, shs_ref)
    hp_ref[...] = _pack_rows(h2)
    h_hi, h_lo = _split2(h2)
    rw_hi = rwh_ref[...]
    lg_ref[...] = _dot_nt(rw_hi, h_hi) + _dot_nt(rw_hi, h_lo) + _dot_nt(rwl_ref[...], h_hi)


def _out_proj(a0, c0, a1, c1, w16, x, gt, g2, sh, sc, rw_hi, rw_lo, bp, s):
    n_p, d = x[0].shape
    n_s = x[1].shape[0]
    n = n_p + n_s
    half = d // 2
    ne = rw_hi.shape[0]
    tok = lambda i: (i, 0)
    npt = n_p // TM
    mod = _mod_specs(TM, d, npt, s // TM, bp)
    xspecs = _tok_specs(TM, d, npt)
    return pl.pallas_call(
        functools.partial(_out_proj_kernel, npt=npt),
        grid=(n // TM,),
        in_specs=[
            *_tok_specs(TM, half, npt, c0),
            *_tok_specs(TM, half, npt, c1),
            _const_spec((d, d)),
            *xspecs,
            *mod,
            _const_spec((1, d)),
            *mod,
            *mod,
            _const_spec((ne, d)),
            _const_spec((ne, d)),
        ],
        out_specs=xspecs + [pl.BlockSpec((TM, half), tok), pl.BlockSpec((ne, TM), lambda i: (0, i))],
        out_shape=[jax.ShapeDtypeStruct((n_p, d), F32), jax.ShapeDtypeStruct((n_s, d), F32),
                   jax.ShapeDtypeStruct((n, half), U32), jax.ShapeDtypeStruct((ne, n), F32)],
        compiler_params=_cparams(("arbitrary",)),
        name="out_proj",
    )(*a0, *a1, w16, *x, *gt, g2, *sh, *sc, rw_hi, rw_lo)


def _route_kernel(lg_ref, rb_ref, tri_ref, idx_ref, gate_ref, rank_ref, cnt_ref, carry_s):
    @pl.when(pl.program_id(0) == 0)
    def _():
        carry_s[...] = jnp.zeros(carry_s.shape, F32)

    s = jax.nn.sigmoid(lg_ref[...])
    sb = s + rb_ref[...]
    row = lax.broadcasted_iota(I32, s.shape, 0).astype(F32)
    picks = []
    sel = jnp.zeros(s.shape, F32)
    for _ in range(TOP_K):
        m = jnp.max(sb, axis=0, keepdims=True)
        ik = jnp.min(jnp.where(sb == m, row, float(N_EXPERTS)), axis=0, keepdims=True)
        oh = row == ik
        picks.append((ik, oh, jnp.sum(jnp.where(oh, s, 0.0), axis=0, keepdims=True)))
        sel = sel + oh.astype(F32)
        sb = jnp.where(oh, -jnp.inf, sb)
    before = _dot(sel.astype(BF16), tri_ref[...]) + carry_s[...]
    gsum = functools.reduce(lambda a, b: a + b, [g for _, _, g in picks])
    for k, (ik, oh, g) in enumerate(picks):
        idx_ref[k:k + 1, :] = ik.astype(I32)
        gate_ref[k:k + 1, :] = g / gsum * ROUTE_SCALE
        rank_ref[k:k + 1, :] = jnp.sum(jnp.where(oh, before, 0.0), axis=0, keepdims=True).astype(I32)
    carry_s[...] = carry_s[...] + jnp.sum(sel, axis=1, keepdims=True)
    cnt_ref[...] = carry_s[...]


def _route(lg_t, rb, tri):
    ne, n = lg_t.shape
    tm = tri.shape[0]
    tokk = lambda i: (0, i)
    return pl.pallas_call(
        _route_kernel,
        grid=(n // tm,),
        in_specs=[pl.BlockSpec((ne, tm), tokk), _const_spec((ne, 1)), _const_spec((tm, tm))],
        out_specs=[pl.BlockSpec((TOP_K, tm), tokk)] * 3 + [_const_spec((ne, 1))],
        out_shape=[jax.ShapeDtypeStruct((TOP_K, n), I32), jax.ShapeDtypeStruct((TOP_K, n), F32),
                   jax.ShapeDtypeStruct((TOP_K, n), I32), jax.ShapeDtypeStruct((ne, 1), F32)],
        scratch_shapes=[pltpu.VMEM((ne, 1), F32)],
        compiler_params=_cparams(("arbitrary",)),
        name="moe_route",
    )(lg_t, rb, tri)


def _dispatch_kernel(pad_ref, dest_ref, h_ref, xs_ref, zero_s, sem, *, tm):
    i = pl.program_id(0)
    nrow = zero_s.shape[0]

    @pl.when(i == 0)
    def _():
        zero_s[...] = jnp.zeros(zero_s.shape, U32)

        def fill(start):
            cp = pltpu.make_async_copy(zero_s, xs_ref.at[pl.ds(pl.multiple_of(start, 8), nrow), :], sem)
            cp.start()
            cp.wait()

        def fill_pad(e, c):
            fill(pad_ref[e] // 8 * 8)
            return c

        lax.fori_loop(0, N_EXPERTS, fill_pad, 0)
        total = xs_ref.shape[0]
        tail = pad_ref[N_EXPERTS]

        def fill_tail(j, c):
            fill(jnp.minimum(tail + j * nrow, total - nrow))
            return c

        lax.fori_loop(0, (total - tail + nrow - 1) // nrow, fill_tail, 0)

    def issue_row(r, c):
        for k in range(TOP_K):
            d = dest_ref[r * TOP_K + k]
            pltpu.make_async_copy(h_ref.at[pl.ds(r, 1), :], xs_ref.at[pl.ds(d, 1), :], sem).start(priority=k % 2)
        return c

    lax.fori_loop(0, tm, issue_row, 0)
    pltpu.make_async_copy(xs_ref.at[pl.ds(0, tm * TOP_K), :], xs_ref.at[pl.ds(0, tm * TOP_K), :], sem).wait()


def _dispatch(pad_start, dest_flat, hp, cap):
    n, d = hp.shape
    tm = TM_MOVE
    kern = functools.partial(_dispatch_kernel, tm=tm)
    return pl.pallas_call(
        kern,
        grid_spec=pltpu.PrefetchScalarGridSpec(
            num_scalar_prefetch=1,
            grid=(n // tm,),
            in_specs=[
                pl.BlockSpec((tm * TOP_K,), lambda i, ps: (i,), memory_space=pltpu.SMEM),
                pl.BlockSpec((tm, d), lambda i, ps: (i, 0)),
            ],
            out_specs=pl.BlockSpec(memory_space=pl.ANY),
            scratch_shapes=[pltpu.VMEM((BM + 8, d), U32), pltpu.SemaphoreType.DMA(())],
        ),
        out_shape=jax.ShapeDtypeStruct((cap + BM + 8, d), U32),
        compiler_params=_cparams(("arbitrary",)),
        name="moe_dispatch",
    )(pad_start, dest_flat, hp)


def _experts_kernel(exp_ref, nused_ref, x_ref, w1_ref, w3_ref, w2_ref, y_ref):
    used = pl.program_id(0) < nused_ref[0]

    @pl.when(used)
    def _():
        y_ref[...] = _pack_rows(_swiglu_packed(x_ref[...], w1_ref, w3_ref, w2_ref))

    @pl.when(jnp.logical_not(used))
    def _():
        y_ref[...] = jnp.zeros(y_ref.shape, U32)


def _swiglu_packed(xp, w1_ref, w3_ref, w2_ref):
    half = xp.shape[1]
    hi, lo = [v.astype(BF16) for v in _unpack_rows(xp)]
    a = _dot(hi, w1_ref[0:half, :]) + _dot(lo, w1_ref[half:2 * half, :])
    b = _dot(hi, w3_ref[0:half, :]) + _dot(lo, w3_ref[half:2 * half, :])
    return _dot((a * jax.nn.sigmoid(a) * b).astype(BF16), w2_ref[...])


def _experts(blk_e, n_used, xs, w1, w3, w2):
    d = w1.shape[1]
    de = w1.shape[2]
    nb = blk_e.shape[0]
    return pl.pallas_call(
        _experts_kernel,
        grid_spec=pltpu.PrefetchScalarGridSpec(
            num_scalar_prefetch=2,
            grid=(nb,),
            in_specs=[
                pl.BlockSpec((BM, d // 2), lambda i, e, u: (i, 0)),
                pl.BlockSpec((None, d, de), lambda i, e, u: (e[i], 0, 0)),
                pl.BlockSpec((None, d, de), lambda i, e, u: (e[i], 0, 0)),
                pl.BlockSpec((None, de, d), lambda i, e, u: (e[i], 0, 0)),
            ],
            out_specs=pl.BlockSpec((BM, d // 2), lambda i, e, u: (i, 0)),
        ),
        out_shape=jax.ShapeDtypeStruct((nb * BM, d // 2), U32),
        compiler_params=_cparams(("arbitrary",)),
        name="moe_experts",
    )(blk_e, n_used, xs, w1, w3, w2)


def _combine_kernel(dcur_ref, dnxt_ref, ys_ref, g_ref, hp_ref, xp_ref, xs_ref, gtp_ref, gts_ref, s1_ref, s3_ref,
                    s2_ref, xop_ref, xos_ref, buf_s, sem, *, tm, npt):
    i = pl.program_id(0)
    slot = i % 2

    def issue(dest_ref, sl):
        def issue_row(r, c):
            for k in range(TOP_K):
                d = dest_ref[r * TOP_K + k]
                pltpu.make_async_copy(ys_ref.at[pl.ds(d, 1), :], buf_s.at[sl, k, pl.ds(r, 1), :],
                                      sem.at[sl]).start(priority=k % 2)
            return c

        lax.fori_loop(0, tm, issue_row, 0)

    @pl.when(i == 0)
    def _():
        issue(dcur_ref, 0)

    for sl in range(2):
        @pl.when(jnp.logical_and(i + 1 < pl.num_programs(0), slot != sl))
        def _():
            issue(dnxt_ref, sl)

    shared = _swiglu_packed(hp_ref[...], s1_ref, s3_ref, s2_ref)
    pltpu.make_async_copy(ys_ref.at[pl.ds(0, tm * TOP_K), :], ys_ref.at[pl.ds(0, tm * TOP_K), :],
                          sem.at[slot]).wait()
    g = g_ref[...]
    hi, lo = _unpack_rows(buf_s[slot, 0])
    r_hi = g[:, 0:1] * hi
    r_lo = g[:, 0:1] * lo
    for k in range(1, TOP_K):
        hi, lo = _unpack_rows(buf_s[slot, k])
        r_hi = r_hi + g[:, k:k + 1] * hi
        r_lo = r_lo + g[:, k:k + 1] * lo
    is_s = _sample_rows(i, npt, tm)
    routed = jnp.concatenate([r_hi, r_lo], axis=1)
    _store_tok(i, npt, xop_ref, xos_ref,
               _pick(is_s, xp_ref, xs_ref) + _pick(is_s, gtp_ref, gts_ref) * (routed + shared))


def _combine(dest_flat, ys, gates, hp, x, gt, s1, s3, s2, bp, s):
    n_p, d = x[0].shape
    n_s = x[1].shape[0]
    n = n_p + n_s
    ds_ = s1.shape[1]
    tm = TM_MOVE
    npt = n_p // tm
    nstep = n // tm
    xspecs = _tok_specs(tm, d, npt)
    kern = functools.partial(_combine_kernel, tm=tm, npt=npt)
    tok = lambda i: (i, 0)
    return pl.pallas_call(
        kern,
        grid=(nstep,),
        in_specs=[
            pl.BlockSpec((tm * TOP_K,), lambda i: (i,), memory_space=pltpu.SMEM),
            pl.BlockSpec((tm * TOP_K,), lambda i: (jnp.minimum(i + 1, nstep - 1),), memory_space=pltpu.SMEM),
            pl.BlockSpec(memory_space=pl.ANY),
            pl.BlockSpec((tm, TOP_K), tok),
            pl.BlockSpec((tm, d // 2), tok),
            *xspecs,
            *_mod_specs(tm, d, npt, s // tm, bp),
            _const_spec((d, ds_)),
            _const_spec((d, ds_)),
            _const_spec((ds_, d)),
        ],
        out_specs=xspecs,
        out_shape=[jax.ShapeDtypeStruct((n_p, d), F32), jax.ShapeDtypeStruct((n_s, d), F32)],
        scratch_shapes=[pltpu.VMEM((2, TOP_K, tm, d // 2), U32), pltpu.SemaphoreType.DMA((2,))],
        compiler_params=_cparams(("arbitrary",)),
        name="moe_combine",
    )(dest_flat, dest_flat, ys, gates, hp, *x, *gt, s1, s3, s2)


def _moe(lg_t, hp, x, gt, rb, w1, w3, w2, s1, s3, s2, tri, bp, s):
    n = hp.shape[0]
    idx_t, gate_t, rank_t, cnt = _route(lg_t, rb.reshape(N_EXPERTS, 1).astype(F32), tri)
    counts = cnt[:, 0].astype(I32)
    padded = (counts + BM - 1) // BM * BM
    pend = jnp.cumsum(padded)
    pstart = pend - padded
    nb = (n * TOP_K + N_EXPERTS * (BM - 1) + BM - 1) // BM
    n_used = pend[-1] // BM
    blk_e = jnp.minimum(jnp.sum(pend[None, :] <= (jnp.arange(nb, dtype=I32) * BM)[:, None], axis=1),
                        N_EXPERTS - 1).astype(I32)
    dest_t = jnp.sum(jnp.where(idx_t[None] == jnp.arange(N_EXPERTS, dtype=I32)[:, None, None],
                               pstart[:, None, None], 0), axis=0) + rank_t
    dest_flat = dest_t.T.reshape(-1)
    fill = jnp.concatenate([pstart + counts, pend[-1:]]).astype(I32)
    xs = _dispatch(fill, dest_flat, hp, nb * BM)
    ys = _experts(blk_e, n_used.reshape(1).astype(I32), xs, w1, w3, w2)
    return _combine(dest_flat, ys, gate_t.T, hp, x, gt, s1, s3, s2, bp, s)


def _mlstm_pre_kernel(xp_ref, xs_ref, g_ref, shp_ref, shs_ref, scp_ref, scs_ref, wxo_ref, wgh_ref, wgl_ref, bif_ref, cw_ref, cb_ref,
                      wq_ref, wk_ref, wv_ref, halo_ref,
                      q_ref, k_ref, v_ref, o_ref, xcv_ref, gates_ref, tail_ref, xcs_ref,
                      xpad_s, *, npt, tps, t_s):
    i = pl.program_id(0)
    tm, d = xp_ref.shape
    is_s = _sample_rows(i, npt, tm)
    h = (_rms(_pick(is_s, xp_ref, xs_ref), g_ref[...]) * (1.0 + _pick(is_s, scp_ref, scs_ref))
         + _pick(is_s, shp_ref, shs_ref))
    h_hi, h_lo = _split2(h)
    y = _dot(h_hi, wxo_ref[...])
    xc = y[:, :d]
    o_ref[...] = y[:, d:].astype(BF16)
    wgh = wgh_ref[...]
    gp = _dot(h_hi, wgh) + _dot(h_hi, wgl_ref[...]) + _dot(h_lo, wgh) + bif_ref[...]
    lane = lax.broadcasted_iota(I32, gp.shape, 1)
    log_sig = jnp.minimum(gp, 0.0) - jnp.log(1.0 + jnp.exp(-jnp.abs(gp)))
    gates_ref[...] = jnp.where(lane >= C_HEADS, log_sig, gp)

    is_sample = i >= npt

    @pl.when(jnp.logical_or(i % tps == 0, is_sample))
    def _():
        xpad_s[0:8, :] = jnp.zeros((8, d), F32)

    xpad_s[8:, :] = xc
    row = (lax.broadcasted_iota(I32, (tm, 1), 0) & (t_s - 1)) + jnp.where(is_sample, 0, C_CONV)
    acc = xc * cw_ref[C_CONV - 1:C_CONV, :] + cb_ref[...]
    for j in range(1, C_CONV):
        prev = xpad_s[8 - j:8 - j + tm, :]
        prev = jnp.where(row < j, halo_ref[j - 1], prev)
        acc = acc + prev * cw_ref[C_CONV - 1 - j:C_CONV - j, :]
    xpad_s[0:8, :] = xc[tm - 8:, :]
    xconv = acc * jax.nn.sigmoid(acc)
    xcv16 = xconv.astype(BF16)
    xc16 = xc.astype(BF16)
    xcv_ref[...] = xcv16
    for hh in range(C_HEADS):
        cs = slice(hh * C_HEAD_DIM, (hh + 1) * C_HEAD_DIM)
        q_ref[:, cs] = _dot(xcv16[:, cs], wq_ref[hh]).astype(BF16)
        k_ref[:, cs] = (_dot(xcv16[:, cs], wk_ref[hh]) * C_HEAD_DIM ** -0.5).astype(BF16)
        v_ref[:, cs] = _dot(xc16[:, cs], wv_ref[hh]).astype(BF16)

    @pl.when(jnp.logical_not(is_sample))
    def _():
        tail_ref[...] = xc[tm - 8:, :]

    @pl.when(is_sample)
    def _():
        xcs_ref[...] = xc


def _mlstm_pre(x, g, sh, sc, wxo, wgh, wgl, bif, cw, cb, wq, wk, wv, halo, bp, s, t_s):
    d = x[0].shape[1]
    n = x[0].shape[0] + x[1].shape[0]
    npt = bp * s // TM
    tps = s // TM
    nst = n // TM - npt
    tok = lambda i: (i, 0)
    mod = _mod_specs(TM, d, npt, tps, bp)
    kern = functools.partial(_mlstm_pre_kernel, npt=npt, tps=tps, t_s=t_s)
    b16 = jax.ShapeDtypeStruct((n, d), BF16)
    return pl.pallas_call(
        kern,
        grid=(n // TM,),
        in_specs=[
            *_tok_specs(TM, d, npt),
            _const_spec((1, d)),
            *mod,
            *mod,
            _const_spec((d, 2 * d)),
            _const_spec((d, LANES)),
            _const_spec((d, LANES)),
            _const_spec((1, LANES)),
            _const_spec((C_CONV, d)),
            _const_spec((1, d)),
            _const_spec((C_HEADS, C_HEAD_DIM, C_HEAD_DIM)),
            _const_spec((C_HEADS, C_HEAD_DIM, C_HEAD_DIM)),
            _const_spec((C_HEADS, C_HEAD_DIM, C_HEAD_DIM)),
            pl.BlockSpec((None, C_CONV - 1, TM, d), lambda i: (jnp.maximum(i - npt, 0), 0, 0, 0),
                         pipeline_mode=pl.Buffered(1)),
        ],
        out_specs=[pl.BlockSpec((TM, d), tok)] * 5 + [
            pl.BlockSpec((TM, LANES), tok),
            pl.BlockSpec((None, 8, d), lambda i: (jnp.minimum(i // tps, bp - 1), 0, 0)),
            pl.BlockSpec((TM, d), lambda i: (jnp.maximum(i - npt, 0), 0)),
        ],
        out_shape=[b16] * 5 + [
            jax.ShapeDtypeStruct((n, LANES), F32),
            jax.ShapeDtypeStruct((bp, 8, d), F32),
            jax.ShapeDtypeStruct((nst * TM, d), F32),
        ],
        scratch_shapes=[pltpu.VMEM((TM + 8, d), F32)],
        compiler_params=_cparams(("arbitrary",)),
        name="mlstm_pre",
    )(*x, g, *sh, *sc, wxo, wgh, wgl, bif, cw, cb, wq, wk, wv, halo)


def _mlstm_scan_kernel(q_ref, k_ref, v_ref, o_ref, xcv_ref, gc_ref, gr_ref, tri_ref, trit_ref, hg_ref, sk_ref,
                       mem0_ref, nrm0_ref, mx0_ref,
                       a_ref, memo_ref, nrmo_ref, mxo_ref,
                       mem_s, nrm_s, mx_s, *, nc):
    c = pl.program_id(1)
    ln = q_ref.shape[0]

    @pl.when(c == 0)
    def _():
        mem_s[...] = mem0_ref[...]
        nrm_s[...] = nrm0_ref[...]
        mx_s[...] = mx0_ref[...]

    gc = gc_ref[...]
    gr = gr_ref[...]
    tri = tri_ref[...]
    trit = trit_ref[...]
    bc = functools.reduce(lambda a, b: a + b, [_dot(tri, p) for p in _split3(gc)])
    br = functools.reduce(lambda a, b: a + b, [_dot(p, trit) for p in _split3(gr)])
    causal = lax.broadcasted_iota(I32, (ln, ln), 1) <= lax.broadcasted_iota(I32, (ln, ln), 0)
    for h in range(C_HEADS):
        cs = slice(h * C_HEAD_DIM, (h + 1) * C_HEAD_DIM)
        b_col = bc[:, C_HEADS + h:C_HEADS + h + 1]
        ig_col = gc[:, h:h + 1]
        b_row = br[C_HEADS + h:C_HEADS + h + 1, :]
        ig_row = gr[h:h + 1, :]
        b_last = b_row[:, ln - 1:ln]
        mx = mx_s[h:h + 1, 0:1]
        logw = jnp.where(causal, b_col - b_row + ig_row, NEG)
        g = b_col + mx
        m_t = jnp.maximum(g, jnp.max(logw, axis=1, keepdims=True))
        w = jnp.exp(logw - m_t)
        inter = jnp.exp(g - m_t)
        qh = q_ref[:, cs]
        kh = k_ref[:, cs]
        vh = v_ref[:, cs]
        a = w * _dot_nt(qh, kh)
        mem = mem_s[h]
        nrm = nrm_s[h:h + 1, :]
        num = _dot(a.astype(BF16), vh) + inter * _dot(qh, mem.astype(BF16))
        den = jnp.sum(a, axis=1, keepdims=True) + inter * jnp.sum(qh.astype(F32) * nrm, axis=1, keepdims=True)
        hout = num / jnp.maximum(jnp.abs(den), jnp.exp(-m_t))
        logs = b_last - b_col + ig_col
        m_new = jnp.maximum(b_last + mx, jnp.max(logs, axis=0, keepdims=True))
        decay = jnp.exp(b_last + mx - m_new)
        kw = kh.astype(F32) * jnp.exp(logs - m_new)
        mem_s[h] = decay * mem + _dot_tn(kw.astype(BF16), vh)
        nrm_s[h:h + 1, :] = decay * nrm + jnp.sum(kw, axis=0, keepdims=True)
        mx_s[h:h + 1, :] = jnp.broadcast_to(m_new, (1, mx_s.shape[1]))
        hh = hout * jax.nn.sigmoid(o_ref[:, cs].astype(F32))
        a_ref[:, cs] = (_rms(hh, hg_ref[:, cs]) + sk_ref[:, cs] * xcv_ref[:, cs].astype(F32)).astype(BF16)

    @pl.when(c == nc - 1)
    def _():
        memo_ref[...] = mem_s[...]
        nrmo_ref[...] = nrm_s[...]
        mxo_ref[...] = mx_s[...]


def _mlstm_scan(q, k, v, o, xcv, gates, hg, sk, mem0, nrm0, mx0, row0, nb, nc, ln):
    d = q.shape[1]
    nrow = nb * nc * ln
    gsl = lax.slice_in_dim(gates, row0 * ln, row0 * ln + nrow, axis=0)[:, :16]
    gr = jnp.transpose(gsl.reshape(nb * nc, ln, 16), (0, 2, 1))
    r = jnp.arange(ln)
    tri = (r[None, :] <= r[:, None]).astype(BF16)
    chunk = lambda b, c: (row0 + b * nc + c, 0)
    seq4 = lambda b, c: (b, 0, 0, 0)
    seq3 = lambda b, c: (b, 0, 0)
    kern = functools.partial(_mlstm_scan_kernel, nc=nc)
    return pl.pallas_call(
        kern,
        grid=(nb, nc),
        in_specs=[pl.BlockSpec((ln, d), chunk)] * 5 + [
            pl.BlockSpec((ln, LANES), chunk),
            pl.BlockSpec((None, 16, ln), lambda b, c: (b * nc + c, 0, 0)),
            _const_spec((ln, ln)),
            _const_spec((ln, ln)),
            _const_spec((1, d)),
            _const_spec((1, d)),
            pl.BlockSpec((None, C_HEADS, C_HEAD_DIM, C_HEAD_DIM), seq4),
            pl.BlockSpec((None, 8, C_HEAD_DIM), seq3),
            pl.BlockSpec((None, 8, LANES), seq3),
        ],
        out_specs=[
            pl.BlockSpec((ln, d), lambda b, c: (b * nc + c, 0)),
            pl.BlockSpec((None, C_HEADS, C_HEAD_DIM, C_HEAD_DIM), seq4),
            pl.BlockSpec((None, 8, C_HEAD_DIM), seq3),
            pl.BlockSpec((None, 8, LANES), seq3),
        ],
        out_shape=[
            jax.ShapeDtypeStruct((nrow, d), BF16),
            jax.ShapeDtypeStruct((nb, C_HEADS, C_HEAD_DIM, C_HEAD_DIM), F32),
            jax.ShapeDtypeStruct((nb, 8, C_HEAD_DIM), F32),
            jax.ShapeDtypeStruct((nb, 8, LANES), F32),
        ],
        scratch_shapes=[
            pltpu.VMEM((C_HEADS, C_HEAD_DIM, C_HEAD_DIM), F32),
            pltpu.VMEM((8, C_HEAD_DIM), F32),
            pltpu.VMEM((8, LANES), F32),
        ],
        compiler_params=_cparams(("parallel", "arbitrary")),
        name="mlstm_scan",
    )(q, k, v, o, xcv, gates, gr, tri, tri.T, hg, sk, mem0, nrm0, mx0)


def _pad_heads(a, width):
    nb = a.shape[0]
    if a.ndim == 2:
        a = jnp.broadcast_to(a[:, :, None], (nb, C_HEADS, width))
    return jnp.concatenate([a.astype(F32), jnp.zeros((nb, 8 - C_HEADS, width), F32)], axis=1)


def kernel(x_prompt, x_sample, c_prompt, c_sample, cache_a_k, cache_a_v, cache_b_k, cache_b_v, state_c_mem, state_c_norm, state_c_max, state_c_conv, norm_g, w_mod, b_mod, t5_bias, ab_w_in, ab_qk_g, ab_lambda, ab_head_g, ab_rel_bias, ab_w_out, c_w_in, c_b_if, c_conv_w, c_conv_b, c_w_qkv, c_head_g, c_skip, c_w_out, router_w, router_b, exp_w1, exp_w3, exp_w2, sh_w1, sh_w3, sh_w2):
    bp, s, d = x_prompt.shape
    bs, t = x_sample.shape[:2]
    depth = norm_g.shape[0]
    past = cache_a_k.shape[2]
    lb = cache_b_k.shape[2]
    n_p = bp * s
    n_s = bs * t
    n_all = n_p + n_s
    assert s % TM == 0 and n_s % TM == 0 and TM % t == 0 and t & (t - 1) == 0
    assert s % TQ_DIFF == 0 and s % ML_CHUNK == 0 and s >= BAND_PAST + TQ_BAND and TQ_DIFF >= T5_MAX_DIST
    assert past % CHUNK == 0 and lb == BAND_PAST and t <= CHUNK and t >= C_CONV - 1 and TM == BAND_PAST

    def per_token(vec):
        return vec[:bp].reshape(bp, 1, d), jnp.repeat(vec[bp:], t, axis=0)

    x = (x_prompt.reshape(n_p, d), x_sample.reshape(n_s, d))
    c_all = jnp.concatenate([c_prompt, c_sample], axis=0)
    mods = _modulation(c_all, w_mod, b_mod)

    r = jnp.arange(TM)
    tri_route = (r[:, None] < r[None, :]).astype(BF16)
    hd = jnp.arange(A_HEADS * 2 * HEAD_DIM) // HEAD_DIM
    bd = (hd[:, None] == hd[None, :]).astype(BF16)

    leaves = {}
    for l in range(depth):
        m6 = [mods[l][:, j * d:(j + 1) * d] for j in range(6)]
        sh1, sc1, gt1, sh2, sc2, gt2 = [per_token(v) for v in m6]
        i = l // 2
        if l % 2 == 0:
            lam_init = 0.8 - 0.6 * math.exp(-0.3 * l)
            lp = ab_lambda[i].astype(F32)
            lam = (jnp.exp(jnp.sum(lp[0] * lp[1])) - jnp.exp(jnp.sum(lp[2] * lp[3])) + lam_init).reshape(1)
            qkg_t = jnp.tile(ab_qk_g[i].astype(F32), (1, A_HEADS * 2))
            p16, ka_p, ka_s, va_p, va_s, kb_p, kb_s, vb_p, vb_s = _ab_in_proj(
                x, norm_g[l, 0].reshape(1, d), sh1, sc1, ab_w_in[i].astype(BF16), qkg_t, bd, bp, s)
            hg = ab_head_g[i].reshape(1, 2 * HEAD_DIM).astype(F32)
            out_scale = 1.0 - lam_init
            tiles, far = _diff_bias_prompt(t5_bias, TQ_DIFF)
            oa_p = _diff_attn_prompt(p16, tiles, far, lam, hg.reshape(2 * HEAD_DIM, 1), bp, s, out_scale)
            ob_p = _band_attn_prompt(p16, _band_bias_prompt(ab_rel_bias[i], TQ_BAND), bp, s)
            dbc, dbn = _diff_bias_sample(t5_bias, past, t)
            oa_s = _sample_attn(p16, cache_a_k[i].reshape(bs, past, -1), cache_a_v[i].reshape(bs, past, -1),
                                dbc, dbn, lam, hg, n_p, bs, t, True, out_scale)
            bbc, bbn = _band_bias_sample(ab_rel_bias[i], past, lb, t)
            ob_s = _sample_attn(p16, cache_b_k[i].reshape(bs, lb, -1), cache_b_v[i].reshape(bs, lb, -1),
                                bbc, bbn, lam, hg, n_p, bs, t, False, out_scale)
            mix_in = ((oa_p, oa_s.reshape(n_s, -1)), 0, (ob_p, ob_s.reshape(n_s, -1)), 0)
            w_out16 = ab_w_out[i].astype(BF16)
            leaves.setdefault('akp', []).append(ka_p.reshape(bp, s, A_HEADS, 2, HEAD_DIM))
            leaves.setdefault('avp', []).append(va_p.reshape(bp, s, A_HEADS, 2 * HEAD_DIM))
            leaves.setdefault('aks', []).append(ka_s.reshape(bs, t, A_HEADS, 2, HEAD_DIM))
            leaves.setdefault('avs', []).append(va_s.reshape(bs, t, A_HEADS, 2 * HEAD_DIM))
            leaves.setdefault('bkp', []).append(kb_p.reshape(bp, TM, B_HEADS, HEAD_DIM))
            leaves.setdefault('bvp', []).append(vb_p.reshape(bp, TM, B_HEADS, HEAD_DIM))
            leaves.setdefault('bks', []).append(kb_s.reshape(bs, t, B_HEADS, HEAD_DIM))
            leaves.setdefault('bvs', []).append(vb_s.reshape(bs, t, B_HEADS, HEAD_DIM))
        else:
            w_in = c_w_in[i]
            wg = jnp.pad(w_in[:, 2 * d:].astype(F32), ((0, 0), (0, LANES - 2 * C_HEADS)))
            wgh = wg.astype(BF16)
            wgl = (wg - wgh.astype(F32)).astype(BF16)
            bif = jnp.pad(c_b_if[i].astype(F32).reshape(1, 2 * C_HEADS), ((0, 0), (0, LANES - 2 * C_HEADS)))
            cprev = state_c_conv[i].astype(F32)
            planes = []
            for j in range(1, C_CONV):
                rows = jnp.concatenate([cprev[:, C_CONV - 1 - j:, :], jnp.zeros((bs, t - j, d), F32)], axis=1)
                planes.append(rows.reshape(n_s // TM, TM, d))
            halo = jnp.stack(planes, axis=1)
            wqkv = c_w_qkv[i].astype(BF16)
            q, k, v, o, xcv, gates, tail, xcs = _mlstm_pre(
                x, norm_g[l, 0].reshape(1, d), sh1, sc1, w_in[:, :2 * d].astype(BF16), wgh, wgl, bif,
                c_conv_w[i].astype(F32), c_conv_b[i].reshape(1, d).astype(F32), wqkv[0], wqkv[1], wqkv[2],
                halo, bp, s, t)
            hg = c_head_g[i].reshape(1, d).astype(F32)
            sk = c_skip[i].reshape(1, d).astype(F32)
            zm = jnp.zeros((bp, C_HEADS, C_HEAD_DIM, C_HEAD_DIM), F32)
            a_p, mem_p, nrm_p, mx_p = _mlstm_scan(
                q, k, v, o, xcv, gates, hg, sk, zm, jnp.zeros((bp, 8, C_HEAD_DIM), F32),
                jnp.zeros((bp, 8, LANES), F32), 0, bp, s // ML_CHUNK, ML_CHUNK)
            a_s, mem_s, nrm_s, mx_s = _mlstm_scan(
                q, k, v, o, xcv, gates, hg, sk, state_c_mem[i].astype(F32),
                _pad_heads(state_c_norm[i], C_HEAD_DIM), _pad_heads(state_c_max[i], LANES),
                n_p // t, bs, 1, t)
            mix_in = ((a_p, a_s), 0, (a_p, a_s), 1)
            w_out16 = c_w_out[i].astype(BF16)
            leaves.setdefault('memp', []).append(mem_p)
            leaves.setdefault('normp', []).append(nrm_p[:, :C_HEADS])
            leaves.setdefault('maxp', []).append(mx_p[:, :C_HEADS, 0])
            leaves.setdefault('convp', []).append(tail[:, 8 - (C_CONV - 1):])
            leaves.setdefault('mems', []).append(mem_s)
            leaves.setdefault('norms', []).append(nrm_s[:, :C_HEADS])
            leaves.setdefault('maxs', []).append(mx_s[:, :C_HEADS, 0])
            leaves.setdefault('convs', []).append(xcs.reshape(bs, t, d)[:, t - (C_CONV - 1):])
        rw_t = router_w[l].astype(F32).T
        rw_hi = rw_t.astype(BF16)
        rw_lo = (rw_t - rw_hi.astype(F32)).astype(BF16)
        xp, xs, hp, lg_t = _out_proj(mix_in[0], mix_in[1], mix_in[2], mix_in[3], w_out16, x, gt1,
                                     norm_g[l, 1].reshape(1, d), sh2, sc2, rw_hi, rw_lo, bp, s)
        x = _moe(lg_t, hp, (xp, xs), gt2, router_b[l], exp_w1[l].astype(BF16), exp_w3[l].astype(BF16),
                     exp_w2[l].astype(BF16), sh_w1[l].astype(BF16), sh_w3[l].astype(BF16),
                     sh_w2[l].astype(BF16), tri_route, bp, s)

    order = ['akp', 'avp', 'aks', 'avs', 'bkp', 'bvp', 'bks', 'bvs',
             'memp', 'normp', 'maxp', 'convp', 'mems', 'norms', 'maxs', 'convs']
    return (x[0].reshape(bp, s, d), x[1].reshape(bs, t, d)) + tuple(
        jnp.stack(leaves[name]) for name in order)
```

```python
import functools
import math

import jax
import jax.numpy as jnp
from jax import lax
from jax.experimental import pallas as pl
from jax.experimental.pallas import tpu as pltpu

F32 = jnp.float32
BF16 = jnp.bfloat16
I32 = jnp.int32
U32 = jnp.uint32

EPS = 1e-6
NEG = -1e30
CHUNK = 64
HEAD_DIM = 64
A_HEADS = 4
B_HEADS = 8
BAND_CHUNKS = 8
BAND_PAST = BAND_CHUNKS * CHUNK
REL_CLIP = 128
T5_BUCKETS = 32
T5_MAX_DIST = 128
C_HEADS = 4
C_HEAD_DIM = 256
C_CONV = 4
N_EXPERTS = 64
TOP_K = 8
ROUTE_SCALE = 2.5

LANES = 128
TM = 512
TQ_DIFF = 512
TQ_BAND = 128
ML_CHUNK = 256
BM = 512
TM_MOVE = 256
VMEM_LIMIT = 56 * 1024 * 1024


def _cparams(sem):
    return pltpu.CompilerParams(dimension_semantics=sem, vmem_limit_bytes=VMEM_LIMIT)


def _dot(a, b):
    return jnp.dot(a, b, preferred_element_type=F32)


def _dot_nt(a, b):
    return lax.dot_general(a, b, (((1,), (1,)), ((), ())), preferred_element_type=F32)


def _dot_tn(a, b):
    return lax.dot_general(a, b, (((0,), (0,)), ((), ())), preferred_element_type=F32)


def _split2(x):
    hi = x.astype(BF16)
    lo = (x - hi.astype(F32)).astype(BF16)
    return hi, lo


def _split3(x):
    p0 = x.astype(BF16)
    r1 = x - p0.astype(F32)
    p1 = r1.astype(BF16)
    p2 = (r1 - p1.astype(F32)).astype(BF16)
    return p0, p1, p2


def _pack_rows(x):
    half = x.shape[1] // 2
    bits = lax.bitcast_convert_type(x.astype(BF16).astype(F32), U32)
    return bits[:, :half] | (bits[:, half:] >> 16)


def _unpack_rows(p):
    return (lax.bitcast_convert_type(p & jnp.uint32(0xFFFF0000), F32),
            lax.bitcast_convert_type(p << 16, F32))


def _rms(x, g):
    return x * lax.rsqrt(jnp.mean(x * x, axis=-1, keepdims=True) + EPS) * g


def _mod_specs(tile, d, npt, tps, bp):
    return [pl.BlockSpec((None, 1, d), lambda i: (jnp.minimum(i // tps, bp - 1), 0, 0)),
            pl.BlockSpec((tile, d), lambda i: (jnp.maximum(i - npt, 0), 0), pipeline_mode=pl.Buffered(1))]


def _tok_specs(tile, width, npt, col=0):
    return [pl.BlockSpec((tile, width), lambda i: (jnp.minimum(i, npt - 1), col)),
            pl.BlockSpec((tile, width), lambda i: (jnp.maximum(i - npt, 0), col))]


def _by_kind(i, npt, body):
    @pl.when(i < npt)
    def _():
        body(False)

    @pl.when(i >= npt)
    def _():
        body(True)


def _pick(sample, p_ref, s_ref):
    return s_ref[...] if sample else p_ref[...]


def _store_tok(sample, p_ref, s_ref, val):
    (s_ref if sample else p_ref)[...] = val


def _const_spec(shape):
    nd = len(shape)
    return pl.BlockSpec(shape, lambda *_: (0,) * nd, pipeline_mode=pl.Buffered(1))


def _mod_kernel(c_ref, w_ref, b_ref, o_ref):
    c = c_ref[...]
    a_hi, a_lo = _split2(c * jax.nn.sigmoid(c))
    w_hi, w_lo = _split2(w_ref[...])
    o_ref[...] = _dot(a_hi, w_hi) + _dot(a_hi, w_lo) + _dot(a_lo, w_hi) + b_ref[...]


def _modulation(c_all, w_mod, b_mod):
    depth, d, n6 = w_mod.shape
    nseq = c_all.shape[0]
    tn = 512
    return pl.pallas_call(
        _mod_kernel,
        grid=(depth, n6 // tn),
        in_specs=[
            pl.BlockSpec((nseq, d), lambda l, j: (0, 0)),
            pl.BlockSpec((None, d, tn), lambda l, j: (l, 0, j)),
            pl.BlockSpec((None, 1, tn), lambda l, j: (l, 0, j)),
        ],
        out_specs=pl.BlockSpec((None, nseq, tn), lambda l, j: (l, 0, j)),
        out_shape=jax.ShapeDtypeStruct((depth, nseq, n6), F32),
        compiler_params=_cparams(("parallel", "parallel")),
        name="modulation",
    )(c_all, w_mod, b_mod.reshape(depth, 1, n6))


def _ab_in_kernel(xp_ref, xs_ref, g_ref, shp_ref, shs_ref, scp_ref, scs_ref, w_ref, qkg_ref, bd_ref,
                  p16_ref, kap_ref, kas_ref, vap_ref, vas_ref, kbp_ref, kbs_ref, vbp_ref, vbs_ref, *, npt, tps):
    i = pl.program_id(0)
    wa = A_HEADS * 2 * HEAD_DIM

    def group_norm(seg, gi):
        hi, lo = _split2(seg * seg)
        bd = bd_ref[...]
        ss = _dot(hi, bd) + _dot(lo, bd)
        return seg * lax.rsqrt(ss * (1.0 / HEAD_DIM) + EPS) * qkg_ref[gi:gi + 1, :]

    def body(sample):
        h = (_rms(_pick(sample, xp_ref, xs_ref), g_ref[...]) * (1.0 + _pick(sample, scp_ref, scs_ref))
             + _pick(sample, shp_ref, shs_ref))
        y = _dot(h.astype(BF16), w_ref[...])
        qa = group_norm(y[:, 0 * wa:1 * wa], 0)
        ka = group_norm(y[:, 1 * wa:2 * wa], 1)
        va = y[:, 2 * wa:3 * wa]
        qb = group_norm(y[:, 3 * wa:4 * wa], 2)
        kb = group_norm(y[:, 4 * wa:5 * wa], 3)
        vb = y[:, 5 * wa:6 * wa]
        scale = HEAD_DIM ** -0.5
        p16_ref[:, 0 * wa:1 * wa] = (qa * scale).astype(BF16)
        p16_ref[:, 1 * wa:2 * wa] = ka.astype(BF16)
        p16_ref[:, 2 * wa:3 * wa] = va.astype(BF16)
        p16_ref[:, 3 * wa:4 * wa] = (qb * scale).astype(BF16)
        p16_ref[:, 4 * wa:5 * wa] = kb.astype(BF16)
        p16_ref[:, 5 * wa:6 * wa] = vb.astype(BF16)
        _store_tok(sample, kap_ref, kas_ref, ka)
        _store_tok(sample, vap_ref, vas_ref, va)
        if sample:
            kbs_ref[...] = kb
            vbs_ref[...] = vb
        else:
            @pl.when(i % tps == tps - 1)
            def _():
                kbp_ref[...] = kb
                vbp_ref[...] = vb

    _by_kind(i, npt, body)


def _ab_in_proj(x, g, sh, sc, w16, qkg_t, bd, bp, s):
    n_p, d = x[0].shape
    n_s = x[1].shape[0]
    n = n_p + n_s
    wa = A_HEADS * 2 * HEAD_DIM
    n_in = w16.shape[1]
    tok = lambda i: (i, 0)
    npt = n_p // TM
    tps = s // TM
    mod = _mod_specs(TM, d, npt, tps, bp)
    leaf = _tok_specs(TM, wa, npt)
    band = [pl.BlockSpec((None, TM, wa), lambda i: (jnp.minimum(i // tps, bp - 1), 0, 0)), leaf[1]]
    f32 = lambda rows: jax.ShapeDtypeStruct((rows, wa), F32)
    band_shape = [jax.ShapeDtypeStruct((bp, TM, wa), F32), f32(n_s)]
    return pl.pallas_call(
        functools.partial(_ab_in_kernel, npt=npt, tps=tps),
        grid=(n // TM,),
        in_specs=[
            *_tok_specs(TM, d, npt),
            _const_spec((1, d)),
            *mod,
            *mod,
            _const_spec((d, n_in)),
            _const_spec((4, wa)),
            _const_spec((wa, wa)),
        ],
        out_specs=[pl.BlockSpec((TM, n_in), tok)] + leaf + leaf + band + band,
        out_shape=[jax.ShapeDtypeStruct((n, n_in), BF16), f32(n_p), f32(n_s), f32(n_p), f32(n_s)]
        + band_shape + band_shape,
        compiler_params=_cparams(("arbitrary",)),
        name="ab_in_proj",
    )(*x, g, *sh, *sc, w16, qkg_t, bd)


def _t5_bucket(rel):
    half = T5_BUCKETS // 2
    exact = half // 2
    n = jnp.abs(rel)
    large = exact + (jnp.log(jnp.maximum(n, 1).astype(F32) / exact)
                     / math.log(T5_MAX_DIST / exact) * (half - exact)).astype(I32)
    large = jnp.minimum(large, half - 1)
    return jnp.where(rel > 0, half, 0) + jnp.where(n < exact, n, large)


def _lookup(table, idx):
    onehot = (idx[..., None] == jnp.arange(table.shape[0], dtype=I32)).astype(F32)
    return jnp.einsum('...n,nh->...h', onehot, table.astype(F32), precision=lax.Precision.HIGHEST)


def _diff_bias_prompt(t5_bias, tq):
    i = jnp.arange(tq)[None, :]
    j = jnp.arange(tq)[:, None]
    diag = jnp.where(((j // CHUNK) <= (i // CHUNK))[..., None], _lookup(t5_bias, _t5_bucket(j - i)), NEG)
    prev = _lookup(t5_bias, _t5_bucket(j - i - tq))
    first = jnp.concatenate([diag, jnp.full_like(diag, NEG)], axis=0)
    later = jnp.concatenate([prev, diag], axis=0)
    tiles = jnp.transpose(jnp.stack([first, later]), (3, 0, 1, 2))
    tiles = jnp.concatenate([tiles, tiles], axis=3)
    far = _lookup(t5_bias, _t5_bucket(jnp.full((1,), -T5_MAX_DIST, I32)))[0]
    return tiles, far


def _diff_bias_sample(t5_bias, past, t):
    qpos = past + jnp.arange(t)
    kpos = jnp.arange(past + t)
    rel = kpos[None, :] - qpos[:, None]
    vis = (kpos[None, :] // CHUNK) <= (qpos[:, None] // CHUNK)
    b = jnp.where(vis[..., None], _lookup(t5_bias, _t5_bucket(rel)), NEG)
    b = jnp.transpose(b, (2, 0, 1))
    b = jnp.concatenate([b, b], axis=1)
    return b[:, :, :past], b[:, :, past:]


def _band_bias_prompt(rel_bias, tq):
    nvar = BAND_PAST // tq + 1
    win = BAND_PAST + tq
    u = jnp.arange(nvar)[:, None, None]
    i = jnp.arange(tq)[None, :, None]
    j = jnp.arange(win)[None, None, :]
    qp = u * tq + i
    qc = qp // CHUNK
    kc = j // CHUNK
    valid = (kc <= qc) & (kc >= qc - BAND_CHUNKS)
    b = _lookup(rel_bias, jnp.clip(j - qp, -REL_CLIP, REL_CLIP) + REL_CLIP)
    b = jnp.where(valid[..., None], b, NEG)
    b = jnp.transpose(b, (0, 3, 1, 2))
    return b.reshape(nvar, B_HEADS // 2, 2 * tq, win)


def _band_bias_sample(rel_bias, past, lb, t):
    qpos = past + jnp.arange(t)
    kpos = past - lb + jnp.arange(lb + t)
    band_lo = (past // CHUNK - BAND_CHUNKS) * CHUNK
    rel = jnp.clip(kpos[None, :] - qpos[:, None], -REL_CLIP, REL_CLIP) + REL_CLIP
    b = jnp.where((kpos >= band_lo)[None, :, None], _lookup(rel_bias, rel), NEG)
    b = jnp.transpose(b, (2, 0, 1)).reshape(B_HEADS // 2, 2 * t, lb + t)
    return b[:, :, :lb], b[:, :, lb:]


def _stack_halves(q):
    lane = lax.broadcasted_iota(I32, q.shape, 1)
    zero = jnp.zeros_like(q)
    return jnp.concatenate([jnp.where(lane < HEAD_DIM, q, zero), jnp.where(lane >= HEAD_DIM, q, zero)], axis=0)


def _diff_finish(o1, o2, lam, hg, out_scale):
    o = o1 - lam * o2
    return (_rms(o, hg) * out_scale).astype(BF16)


def _band_finish(o):
    tq = o.shape[0] // 2
    lane = lax.broadcasted_iota(I32, (tq, o.shape[1]), 1)
    return jnp.where(lane < HEAD_DIM, o[:tq], o[tq:]).astype(BF16)


def _diff_prompt_kernel(far_ref, lam_ref, q_ref, k_ref, v_ref, bias_ref, hg_ref, o_ref,
                        q2t_s, vt_s, m_s, l_s, acc_s, *, tq, out_scale):
    h = pl.program_id(1)
    i = pl.program_id(2)

    @pl.when(i == 0)
    def _():
        for jj in range(vt_s.shape[0]):
            vt_s[jj] = v_ref[jj * tq:(jj + 1) * tq, :].astype(F32).T.astype(BF16)

    qt = q_ref[...].astype(F32).T.astype(BF16)
    row = lax.broadcasted_iota(I32, qt.shape, 0)
    zero = jnp.zeros_like(qt)
    q2t_s[:, 0:tq] = jnp.where(row < HEAD_DIM, qt, zero)
    q2t_s[:, tq:2 * tq] = jnp.where(row >= HEAD_DIM, qt, zero)
    m_s[...] = jnp.full(m_s.shape, NEG, F32)
    l_s[...] = jnp.zeros(l_s.shape, F32)
    acc_s[...] = jnp.zeros(acc_s.shape, F32)

    def step(jb, nblk, bias, shift):
        kb = k_ref[pl.ds(pl.multiple_of(jb * tq, tq), nblk * tq), :]
        s = _dot(kb, q2t_s[...])
        if bias is not None:
            s = s + bias
        cmax = jnp.max(s, axis=0, keepdims=True)
        if shift is not None:
            cmax = cmax + shift
        m_prev = m_s[...]
        m_new = jnp.maximum(m_prev, cmax)
        alpha = jnp.exp(m_prev - m_new)
        p = jnp.exp(s - (m_new if shift is None else m_new - shift))
        l_s[...] = alpha * l_s[...] + jnp.sum(p, axis=0, keepdims=True)
        pb = p.astype(BF16)
        pv = _dot(vt_s[jb], pb[0:tq])
        for u in range(1, nblk):
            pv = pv + _dot(vt_s[jb + u], pb[u * tq:(u + 1) * tq])
        acc_s[...] = alpha * acc_s[...] + pv
        m_s[...] = m_new

    far = far_ref[h]
    nfar = jnp.maximum(i - 1, 0)

    def far_pair(j, carry):
        step(2 * j, 2, None, far)
        return carry

    lax.fori_loop(0, nfar // 2, far_pair, 0)

    @pl.when(nfar % 2 == 1)
    def _():
        step(nfar - 1, 1, None, far)

    step(nfar, 2, bias_ref[...], None)
    o = acc_s[...] / l_s[...]
    od = o[:, 0:tq] - lam_ref[0] * o[:, tq:2 * tq]
    on = od * lax.rsqrt(jnp.mean(od * od, axis=0, keepdims=True) + EPS) * hg_ref[...] * out_scale
    o_ref[...] = on.T.astype(BF16)


def _diff_attn_prompt(p16, tiles, far, lam, hg_col, bp, s, out_scale):
    tq = TQ_DIFF
    nq = s // tq
    wa = A_HEADS * LANES
    kern = functools.partial(_diff_prompt_kernel, tq=tq, out_scale=out_scale)
    return pl.pallas_call(
        kern,
        grid=(bp, A_HEADS, nq),
        in_specs=[
            pl.BlockSpec(memory_space=pltpu.SMEM),
            pl.BlockSpec(memory_space=pltpu.SMEM),
            pl.BlockSpec((tq, LANES), lambda b, h, i: (b * nq + i, h)),
            pl.BlockSpec((s, LANES), lambda b, h, i: (b, A_HEADS + h)),
            pl.BlockSpec((s, LANES), lambda b, h, i: (b, 2 * A_HEADS + h)),
            pl.BlockSpec((None, None, 2 * tq, 2 * tq), lambda b, h, i: (h, jnp.minimum(i, 1), 0, 0)),
            _const_spec((LANES, 1)),
        ],
        out_specs=pl.BlockSpec((tq, LANES), lambda b, h, i: (b * nq + i, h)),
        out_shape=jax.ShapeDtypeStruct((bp * s, wa), BF16),
        scratch_shapes=[
            pltpu.VMEM((LANES, 2 * tq), BF16),
            pltpu.VMEM((nq, LANES, tq), BF16),
            pltpu.VMEM((1, 2 * tq), F32),
            pltpu.VMEM((1, 2 * tq), F32),
            pltpu.VMEM((LANES, 2 * tq), F32),
        ],
        compiler_params=_cparams(("parallel", "parallel", "arbitrary")),
        name="diff_attn_prompt",
    )(far, lam, p16, p16, p16, tiles, hg_col)


def _band_prompt_kernel(q_ref, k_ref, v_ref, bias_ref, o_ref, *, tq, win):
    t = pl.program_id(2)
    start = pl.multiple_of(jnp.maximum(t * tq - BAND_PAST, 0), tq)
    for c in range(q_ref.shape[1] // LANES):
        cs = slice(c * LANES, (c + 1) * LANES)
        kb = k_ref[pl.ds(start, win), cs]
        vb = v_ref[pl.ds(start, win), cs]
        s = _dot_nt(_stack_halves(q_ref[:, cs]), kb) + bias_ref[c]
        m = jnp.max(s, axis=-1, keepdims=True)
        p = jnp.exp(s - m)
        l = jnp.sum(p, axis=-1, keepdims=True)
        o_ref[:, cs] = _band_finish(_dot(p.astype(BF16), vb) / l)


def _band_attn_prompt(p16, bias, bp, s):
    tq = TQ_BAND
    nq = s // tq
    win = BAND_PAST + tq
    nvar = bias.shape[0]
    npair = B_HEADS // 2
    pps = 2
    wide = pps * LANES
    c0 = 3 * A_HEADS // pps
    ng = npair // pps
    kern = functools.partial(_band_prompt_kernel, tq=tq, win=win)
    return pl.pallas_call(
        kern,
        grid=(bp, ng, nq),
        in_specs=[
            pl.BlockSpec((tq, wide), lambda b, p, t: (b * nq + t, c0 + p)),
            pl.BlockSpec((s, wide), lambda b, p, t: (b, c0 + ng + p)),
            pl.BlockSpec((s, wide), lambda b, p, t: (b, c0 + 2 * ng + p)),
            pl.BlockSpec((None, pps, 2 * tq, win), lambda b, p, t: (jnp.minimum(t, nvar - 1), p, 0, 0)),
        ],
        out_specs=pl.BlockSpec((tq, wide), lambda b, p, t: (b * nq + t, p)),
        out_shape=jax.ShapeDtypeStruct((bp * s, npair * LANES), BF16),
        compiler_params=_cparams(("parallel", "parallel", "parallel")),
        name="band_attn_prompt",
    )(p16, p16, p16, bias)


def _sample_attn_kernel(lam_ref, q_ref, kc_ref, vc_ref, kn_ref, vn_ref, bc_ref, bn_ref, hg_ref, o_ref,
                        *, diff, out_scale):
    t = q_ref.shape[0]
    q2 = _stack_halves(q_ref[...])
    sc = _dot_nt(q2, kc_ref[...].astype(BF16)) + bc_ref[...]
    sn = _dot_nt(q2, kn_ref[...]) + bn_ref[...]
    m = jnp.maximum(jnp.max(sc, axis=-1, keepdims=True), jnp.max(sn, axis=-1, keepdims=True))
    pc = jnp.exp(sc - m)
    pn = jnp.exp(sn - m)
    l = jnp.sum(pc, axis=-1, keepdims=True) + jnp.sum(pn, axis=-1, keepdims=True)
    o = (_dot(pc.astype(BF16), vc_ref[...].astype(BF16)) + _dot(pn.astype(BF16), vn_ref[...])) / l
    if diff:
        o_ref[...] = _diff_finish(o[:t], o[t:], lam_ref[0], hg_ref[...], out_scale)
    else:
        o_ref[...] = _band_finish(o)


def _sample_attn(p16, cache_k, cache_v, bias_c, bias_n, lam, hg, np_rows, bs, t, diff, out_scale):
    past = cache_k.shape[1]
    ncol = cache_k.shape[2] // LANES
    row0 = np_rows // t
    if diff:
        qc, kc, vc = 0, A_HEADS, 2 * A_HEADS
    else:
        qc, kc, vc = 3 * A_HEADS, 3 * A_HEADS + ncol, 3 * A_HEADS + 2 * ncol
    kern = functools.partial(_sample_attn_kernel, diff=diff, out_scale=out_scale)
    return pl.pallas_call(
        kern,
        grid=(bs, ncol),
        in_specs=[
            pl.BlockSpec(memory_space=pltpu.SMEM),
            pl.BlockSpec((t, LANES), lambda b, h: (row0 + b, qc + h)),
            pl.BlockSpec((None, past, LANES), lambda b, h: (b, 0, h)),
            pl.BlockSpec((None, past, LANES), lambda b, h: (b, 0, h)),
            pl.BlockSpec((t, LANES), lambda b, h: (row0 + b, kc + h)),
            pl.BlockSpec((t, LANES), lambda b, h: (row0 + b, vc + h)),
            pl.BlockSpec((None, 2 * t, past), lambda b, h: (h, 0, 0)),
            pl.BlockSpec((None, 2 * t, t), lambda b, h: (h, 0, 0)),
            _const_spec((1, LANES)),
        ],
        out_specs=pl.BlockSpec((None, t, LANES), lambda b, h: (b, 0, h)),
        out_shape=jax.ShapeDtypeStruct((bs, t, ncol * LANES), BF16),
        compiler_params=_cparams(("parallel", "parallel")),
        name="diff_attn_sample" if diff else "band_attn_sample",
    )(lam, p16, cache_k, cache_v, p16, p16, bias_c, bias_n, hg)


def _out_proj_kernel(a0p_ref, a0s_ref, a1p_ref, a1s_ref, w_ref, xp_ref, xs_ref, gtp_ref, gts_ref, g2_ref,
                     shp_ref, shs_ref, scp_ref, scs_ref, rwh_ref, rwl_ref, xop_ref, xos_ref, hp_ref, lg_ref,
                     *, npt):
    half = a0p_ref.shape[1]

    def body(sample):
        mix = (_dot(_pick(sample, a0p_ref, a0s_ref), w_ref[0:half, :])
               + _dot(_pick(sample, a1p_ref, a1s_ref), w_ref[half:2 * half, :]))
        x = _pick(sample, xp_ref, xs_ref) + _pick(sample, gtp_ref, gts_ref) * mix
        _store_tok(sample, xop_ref, xos_ref, x)
        h2 = _rms(x, g2_ref[...]) * (1.0 + _pick(sample, scp_ref, scs_ref)) + _pick(sample, shp_ref, shs_ref)
        hp_ref[...] = _pack_rows(h2)
        h_hi, h_lo = _split2(h2)
        rw_hi = rwh_ref[...]
        lg_ref[...] = _dot_nt(rw_hi, h_hi) + _dot_nt(rw_hi, h_lo) + _dot_nt(rwl_ref[...], h_hi)

    _by_kind(pl.program_id(0), npt, body)


def _out_proj(a0, c0, a1, c1, w16, x, gt, g2, sh, sc, rw_hi, rw_lo, bp, s):
    n_p, d = x[0].shape
    n_s = x[1].shape[0]
    n = n_p + n_s
    half = d // 2
    ne = rw_hi.shape[0]
    tok = lambda i: (i, 0)
    npt = n_p // TM
    mod = _mod_specs(TM, d, npt, s // TM, bp)
    xspecs = _tok_specs(TM, d, npt)
    return pl.pallas_call(
        functools.partial(_out_proj_kernel, npt=npt),
        grid=(n // TM,),
        in_specs=[
            *_tok_specs(TM, half, npt, c0),
            *_tok_specs(TM, half, npt, c1),
            _const_spec((d, d)),
            *xspecs,
            *mod,
            _const_spec((1, d)),
            *mod,
            *mod,
            _const_spec((ne, d)),
            _const_spec((ne, d)),
        ],
        out_specs=xspecs + [pl.BlockSpec((TM, half), tok), pl.BlockSpec((ne, TM), lambda i: (0, i))],
        out_shape=[jax.ShapeDtypeStruct((n_p, d), F32), jax.ShapeDtypeStruct((n_s, d), F32),
                   jax.ShapeDtypeStruct((n, half), U32), jax.ShapeDtypeStruct((ne, n), F32)],
        compiler_params=_cparams(("arbitrary",)),
        name="out_proj",
    )(*a0, *a1, w16, *x, *gt, g2, *sh, *sc, rw_hi, rw_lo)


def _route_kernel(lg_ref, rb_ref, tri_ref, idx_ref, gate_ref, rank_ref, cnt_ref, carry_s):
    @pl.when(pl.program_id(0) == 0)
    def _():
        carry_s[...] = jnp.zeros(carry_s.shape, F32)

    s = jax.nn.sigmoid(lg_ref[...])
    sb = s + rb_ref[...]
    row = lax.broadcasted_iota(I32, s.shape, 0).astype(F32)
    picks = []
    sel = jnp.zeros(s.shape, F32)
    for _ in range(TOP_K):
        m = jnp.max(sb, axis=0, keepdims=True)
        ik = jnp.min(jnp.where(sb == m, row, float(N_EXPERTS)), axis=0, keepdims=True)
        oh = row == ik
        picks.append((ik, oh, jnp.sum(jnp.where(oh, s, 0.0), axis=0, keepdims=True)))
        sel = sel + oh.astype(F32)
        sb = jnp.where(oh, -jnp.inf, sb)
    before = _dot(sel.astype(BF16), tri_ref[...]) + carry_s[...]
    gsum = functools.reduce(lambda a, b: a + b, [g for _, _, g in picks])
    for k, (ik, oh, g) in enumerate(picks):
        idx_ref[k:k + 1, :] = ik.astype(I32)
        gate_ref[k:k + 1, :] = g / gsum * ROUTE_SCALE
        rank_ref[k:k + 1, :] = jnp.sum(jnp.where(oh, before, 0.0), axis=0, keepdims=True).astype(I32)
    carry_s[...] = carry_s[...] + jnp.sum(sel, axis=1, keepdims=True)
    cnt_ref[...] = carry_s[...]


def _route(lg_t, rb, tri):
    ne, n = lg_t.shape
    tm = tri.shape[0]
    tokk = lambda i: (0, i)
    return pl.pallas_call(
        _route_kernel,
        grid=(n // tm,),
        in_specs=[pl.BlockSpec((ne, tm), tokk), _const_spec((ne, 1)), _const_spec((tm, tm))],
        out_specs=[pl.BlockSpec((TOP_K, tm), tokk)] * 3 + [_const_spec((ne, 1))],
        out_shape=[jax.ShapeDtypeStruct((TOP_K, n), I32), jax.ShapeDtypeStruct((TOP_K, n), F32),
                   jax.ShapeDtypeStruct((TOP_K, n), I32), jax.ShapeDtypeStruct((ne, 1), F32)],
        scratch_shapes=[pltpu.VMEM((ne, 1), F32)],
        compiler_params=_cparams(("arbitrary",)),
        name="moe_route",
    )(lg_t, rb, tri)


def _dispatch_kernel(pad_ref, dest_ref, h_ref, xs_ref, zero_s, sem, *, tm):
    i = pl.program_id(0)
    nrow = zero_s.shape[0]

    @pl.when(i == 0)
    def _():
        zero_s[...] = jnp.zeros(zero_s.shape, U32)

        def fill(start):
            cp = pltpu.make_async_copy(zero_s, xs_ref.at[pl.ds(pl.multiple_of(start, 8), nrow), :], sem)
            cp.start()
            cp.wait()

        def fill_pad(e, c):
            fill(pad_ref[e] // 8 * 8)
            return c

        lax.fori_loop(0, N_EXPERTS, fill_pad, 0)
        total = xs_ref.shape[0]
        tail = pad_ref[N_EXPERTS]

        def fill_tail(j, c):
            fill(jnp.minimum(tail + j * nrow, total - nrow))
            return c

        lax.fori_loop(0, (total - tail + nrow - 1) // nrow, fill_tail, 0)

    def issue_row(r, c):
        for k in range(TOP_K):
            d = dest_ref[r * TOP_K + k]
            pltpu.make_async_copy(h_ref.at[pl.ds(r, 1), :], xs_ref.at[pl.ds(d, 1), :], sem).start(priority=k % 2)
        return c

    lax.fori_loop(0, tm, issue_row, 0)
    pltpu.make_async_copy(xs_ref.at[pl.ds(0, tm * TOP_K), :], xs_ref.at[pl.ds(0, tm * TOP_K), :], sem).wait()


def _dispatch(pad_start, dest_flat, hp, cap):
    n, d = hp.shape
    tm = TM_MOVE
    kern = functools.partial(_dispatch_kernel, tm=tm)
    return pl.pallas_call(
        kern,
        grid_spec=pltpu.PrefetchScalarGridSpec(
            num_scalar_prefetch=1,
            grid=(n // tm,),
            in_specs=[
                pl.BlockSpec((tm * TOP_K,), lambda i, ps: (i,), memory_space=pltpu.SMEM),
                pl.BlockSpec((tm, d), lambda i, ps: (i, 0)),
            ],
            out_specs=pl.BlockSpec(memory_space=pl.ANY),
            scratch_shapes=[pltpu.VMEM((BM + 8, d), U32), pltpu.SemaphoreType.DMA(())],
        ),
        out_shape=jax.ShapeDtypeStruct((cap + BM + 8, d), U32),
        compiler_params=_cparams(("arbitrary",)),
        name="moe_dispatch",
    )(pad_start, dest_flat, hp)


def _experts_kernel(exp_ref, nused_ref, x_ref, w1_ref, w3_ref, w2_ref, y_ref):
    used = pl.program_id(0) < nused_ref[0]

    @pl.when(used)
    def _():
        y_ref[...] = _pack_rows(_swiglu_packed(x_ref[...], w1_ref, w3_ref, w2_ref))

    @pl.when(jnp.logical_not(used))
    def _():
        y_ref[...] = jnp.zeros(y_ref.shape, U32)


def _swiglu_packed(xp, w1_ref, w3_ref, w2_ref):
    half = xp.shape[1]
    hi, lo = [v.astype(BF16) for v in _unpack_rows(xp)]
    a = _dot(hi, w1_ref[0:half, :]) + _dot(lo, w1_ref[half:2 * half, :])
    b = _dot(hi, w3_ref[0:half, :]) + _dot(lo, w3_ref[half:2 * half, :])
    return _dot((a * jax.nn.sigmoid(a) * b).astype(BF16), w2_ref[...])


def _experts(blk_e, n_used, xs, w1, w3, w2):
    d = w1.shape[1]
    de = w1.shape[2]
    nb = blk_e.shape[0]
    return pl.pallas_call(
        _experts_kernel,
        grid_spec=pltpu.PrefetchScalarGridSpec(
            num_scalar_prefetch=2,
            grid=(nb,),
            in_specs=[
                pl.BlockSpec((BM, d // 2), lambda i, e, u: (i, 0)),
                pl.BlockSpec((None, d, de), lambda i, e, u: (e[i], 0, 0)),
                pl.BlockSpec((None, d, de), lambda i, e, u: (e[i], 0, 0)),
                pl.BlockSpec((None, de, d), lambda i, e, u: (e[i], 0, 0)),
            ],
            out_specs=pl.BlockSpec((BM, d // 2), lambda i, e, u: (i, 0)),
        ),
        out_shape=jax.ShapeDtypeStruct((nb * BM, d // 2), U32),
        compiler_params=_cparams(("arbitrary",)),
        name="moe_experts",
    )(blk_e, n_used, xs, w1, w3, w2)


def _combine_kernel(dcur_ref, dnxt_ref, ys_ref, g_ref, hp_ref, xp_ref, xs_ref, gtp_ref, gts_ref, s1_ref, s3_ref,
                    s2_ref, xop_ref, xos_ref, buf_s, sem, *, tm, npt):
    i = pl.program_id(0)
    slot = i % 2

    def issue(dest_ref, sl):
        def issue_row(r, c):
            for k in range(TOP_K):
                d = dest_ref[r * TOP_K + k]
                pltpu.make_async_copy(ys_ref.at[pl.ds(d, 1), :], buf_s.at[sl, k, pl.ds(r, 1), :],
                                      sem.at[sl]).start(priority=k % 2)
            return c

        lax.fori_loop(0, tm, issue_row, 0)

    @pl.when(i == 0)
    def _():
        issue(dcur_ref, 0)

    for sl in range(2):
        @pl.when(jnp.logical_and(i + 1 < pl.num_programs(0), slot != sl))
        def _():
            issue(dnxt_ref, sl)

    shared = _swiglu_packed(hp_ref[...], s1_ref, s3_ref, s2_ref)
    pltpu.make_async_copy(ys_ref.at[pl.ds(0, tm * TOP_K), :], ys_ref.at[pl.ds(0, tm * TOP_K), :],
                          sem.at[slot]).wait()
    g = g_ref[...]
    hi, lo = _unpack_rows(buf_s[slot, 0])
    r_hi = g[:, 0:1] * hi
    r_lo = g[:, 0:1] * lo
    for k in range(1, TOP_K):
        hi, lo = _unpack_rows(buf_s[slot, k])
        r_hi = r_hi + g[:, k:k + 1] * hi
        r_lo = r_lo + g[:, k:k + 1] * lo
    ffn = jnp.concatenate([r_hi, r_lo], axis=1) + shared
    _by_kind(i, npt, lambda sample: _store_tok(
        sample, xop_ref, xos_ref, _pick(sample, xp_ref, xs_ref) + _pick(sample, gtp_ref, gts_ref) * ffn))


def _combine(dest_flat, ys, gates, hp, x, gt, s1, s3, s2, bp, s):
    n_p, d = x[0].shape
    n_s = x[1].shape[0]
    n = n_p + n_s
    ds_ = s1.shape[1]
    tm = TM_MOVE
    npt = n_p // tm
    nstep = n // tm
    xspecs = _tok_specs(tm, d, npt)
    kern = functools.partial(_combine_kernel, tm=tm, npt=npt)
    tok = lambda i: (i, 0)
    return pl.pallas_call(
        kern,
        grid=(nstep,),
        in_specs=[
            pl.BlockSpec((tm * TOP_K,), lambda i: (i,), memory_space=pltpu.SMEM),
            pl.BlockSpec((tm * TOP_K,), lambda i: (jnp.minimum(i + 1, nstep - 1),), memory_space=pltpu.SMEM),
            pl.BlockSpec(memory_space=pl.ANY),
            pl.BlockSpec((tm, TOP_K), tok),
            pl.BlockSpec((tm, d // 2), tok),
            *xspecs,
            *_mod_specs(tm, d, npt, s // tm, bp),
            _const_spec((d, ds_)),
            _const_spec((d, ds_)),
            _const_spec((ds_, d)),
        ],
        out_specs=xspecs,
        out_shape=[jax.ShapeDtypeStruct((n_p, d), F32), jax.ShapeDtypeStruct((n_s, d), F32)],
        scratch_shapes=[pltpu.VMEM((2, TOP_K, tm, d // 2), U32), pltpu.SemaphoreType.DMA((2,))],
        compiler_params=_cparams(("arbitrary",)),
        name="moe_combine",
    )(dest_flat, dest_flat, ys, gates, hp, *x, *gt, s1, s3, s2)


def _moe(lg_t, hp, x, gt, rb, w1, w3, w2, s1, s3, s2, tri, bp, s):
    n = hp.shape[0]
    idx_t, gate_t, rank_t, cnt = _route(lg_t, rb.reshape(N_EXPERTS, 1).astype(F32), tri)
    counts = cnt[:, 0].astype(I32)
    padded = (counts + BM - 1) // BM * BM
    pend = jnp.cumsum(padded)
    pstart = pend - padded
    nb = (n * TOP_K + N_EXPERTS * (BM - 1) + BM - 1) // BM
    n_used = pend[-1] // BM
    blk_e = jnp.minimum(jnp.sum(pend[None, :] <= (jnp.arange(nb, dtype=I32) * BM)[:, None], axis=1),
                        N_EXPERTS - 1).astype(I32)
    dest_t = jnp.sum(jnp.where(idx_t[None] == jnp.arange(N_EXPERTS, dtype=I32)[:, None, None],
                               pstart[:, None, None], 0), axis=0) + rank_t
    dest_flat = dest_t.T.reshape(-1)
    fill = jnp.concatenate([pstart + counts, pend[-1:]]).astype(I32)
    xs = _dispatch(fill, dest_flat, hp, nb * BM)
    ys = _experts(blk_e, n_used.reshape(1).astype(I32), xs, w1, w3, w2)
    return _combine(dest_flat, ys, gate_t.T, hp, x, gt, s1, s3, s2, bp, s)


def _mlstm_pre_kernel(xp_ref, xs_ref, g_ref, shp_ref, shs_ref, scp_ref, scs_ref, wxo_ref, wgh_ref, wgl_ref, bif_ref, cw_ref, cb_ref,
                      wq_ref, wk_ref, wv_ref, halo_ref,
                      q_ref, k_ref, v_ref, o_ref, xcv_ref, gates_ref, tail_ref, xcs_ref,
                      xpad_s, *, npt, tps, t_s):
    i = pl.program_id(0)
    tm, d = xp_ref.shape

    def body(sample):
        h = (_rms(_pick(sample, xp_ref, xs_ref), g_ref[...]) * (1.0 + _pick(sample, scp_ref, scs_ref))
             + _pick(sample, shp_ref, shs_ref))
        h_hi, h_lo = _split2(h)
        y = _dot(h_hi, wxo_ref[...])
        xc = y[:, :d]
        o_ref[...] = y[:, d:].astype(BF16)
        wgh = wgh_ref[...]
        gp = _dot(h_hi, wgh) + _dot(h_hi, wgl_ref[...]) + _dot(h_lo, wgh) + bif_ref[...]
        lane = lax.broadcasted_iota(I32, gp.shape, 1)
        log_sig = jnp.minimum(gp, 0.0) - jnp.log(1.0 + jnp.exp(-jnp.abs(gp)))
        gates_ref[...] = jnp.where(lane >= C_HEADS, log_sig, gp)

        if sample:
            xpad_s[0:8, :] = jnp.zeros((8, d), F32)
        else:
            @pl.when(i % tps == 0)
            def _():
                xpad_s[0:8, :] = jnp.zeros((8, d), F32)

        xpad_s[8:, :] = xc
        acc = xc * cw_ref[C_CONV - 1:C_CONV, :] + cb_ref[...]
        for j in range(1, C_CONV):
            prev = xpad_s[8 - j:8 - j + tm, :]
            if sample:
                row = lax.broadcasted_iota(I32, (tm, 1), 0) & (t_s - 1)
                prev = jnp.where(row < j, halo_ref[j - 1], prev)
            acc = acc + prev * cw_ref[C_CONV - 1 - j:C_CONV - j, :]
        xconv = acc * jax.nn.sigmoid(acc)
        xcv16 = xconv.astype(BF16)
        xc16 = xc.astype(BF16)
        xcv_ref[...] = xcv16
        for hh in range(C_HEADS):
            cs = slice(hh * C_HEAD_DIM, (hh + 1) * C_HEAD_DIM)
            q_ref[:, cs] = _dot(xcv16[:, cs], wq_ref[hh]).astype(BF16)
            k_ref[:, cs] = (_dot(xcv16[:, cs], wk_ref[hh]) * C_HEAD_DIM ** -0.5).astype(BF16)
            v_ref[:, cs] = _dot(xc16[:, cs], wv_ref[hh]).astype(BF16)
        if sample:
            xcs_ref[...] = xc
        else:
            xpad_s[0:8, :] = xc[tm - 8:, :]
            tail_ref[...] = xc[tm - 8:, :]

    _by_kind(i, npt, body)


def _mlstm_pre(x, g, sh, sc, wxo, wgh, wgl, bif, cw, cb, wq, wk, wv, halo, bp, s, t_s):
    d = x[0].shape[1]
    n = x[0].shape[0] + x[1].shape[0]
    npt = bp * s // TM
    tps = s // TM
    nst = n // TM - npt
    tok = lambda i: (i, 0)
    mod = _mod_specs(TM, d, npt, tps, bp)
    kern = functools.partial(_mlstm_pre_kernel, npt=npt, tps=tps, t_s=t_s)
    b16 = jax.ShapeDtypeStruct((n, d), BF16)
    return pl.pallas_call(
        kern,
        grid=(n // TM,),
        in_specs=[
            *_tok_specs(TM, d, npt),
            _const_spec((1, d)),
            *mod,
            *mod,
            _const_spec((d, 2 * d)),
            _const_spec((d, LANES)),
            _const_spec((d, LANES)),
            _const_spec((1, LANES)),
            _const_spec((C_CONV, d)),
            _const_spec((1, d)),
            _const_spec((C_HEADS, C_HEAD_DIM, C_HEAD_DIM)),
            _const_spec((C_HEADS, C_HEAD_DIM, C_HEAD_DIM)),
            _const_spec((C_HEADS, C_HEAD_DIM, C_HEAD_DIM)),
            pl.BlockSpec((None, C_CONV - 1, TM, d), lambda i: (jnp.maximum(i - npt, 0), 0, 0, 0),
                         pipeline_mode=pl.Buffered(1)),
        ],
        out_specs=[pl.BlockSpec((TM, d), tok)] * 5 + [
            pl.BlockSpec((TM, LANES), tok),
            pl.BlockSpec((None, 8, d), lambda i: (jnp.minimum(i // tps, bp - 1), 0, 0)),
            pl.BlockSpec((TM, d), lambda i: (jnp.maximum(i - npt, 0), 0)),
        ],
        out_shape=[b16] * 5 + [
            jax.ShapeDtypeStruct((n, LANES), F32),
            jax.ShapeDtypeStruct((bp, 8, d), F32),
            jax.ShapeDtypeStruct((nst * TM, d), F32),
        ],
        scratch_shapes=[pltpu.VMEM((TM + 8, d), F32)],
        compiler_params=_cparams(("arbitrary",)),
        name="mlstm_pre",
    )(*x, g, *sh, *sc, wxo, wgh, wgl, bif, cw, cb, wq, wk, wv, halo)


def _mlstm_scan_kernel(q_ref, k_ref, v_ref, o_ref, xcv_ref, gc_ref, gr_ref, tri_ref, trit_ref, hg_ref, sk_ref,
                       mem0_ref, nrm0_ref, mx0_ref,
                       a_ref, memo_ref, nrmo_ref, mxo_ref,
                       mem_s, nrm_s, mx_s, *, nc):
    c = pl.program_id(1)
    ln = q_ref.shape[0]

    @pl.when(c == 0)
    def _():
        mem_s[...] = mem0_ref[...]
        nrm_s[...] = nrm0_ref[...]
        mx_s[...] = mx0_ref[...]

    gc = gc_ref[...]
    gr = gr_ref[...]
    tri = tri_ref[...]
    trit = trit_ref[...]
    bc = functools.reduce(lambda a, b: a + b, [_dot(tri, p) for p in _split3(gc)])
    br = functools.reduce(lambda a, b: a + b, [_dot(p, trit) for p in _split3(gr)])
    causal = lax.broadcasted_iota(I32, (ln, ln), 1) <= lax.broadcasted_iota(I32, (ln, ln), 0)
    for h in range(C_HEADS):
        cs = slice(h * C_HEAD_DIM, (h + 1) * C_HEAD_DIM)
        b_col = bc[:, C_HEADS + h:C_HEADS + h + 1]
        ig_col = gc[:, h:h + 1]
        b_row = br[C_HEADS + h:C_HEADS + h + 1, :]
        ig_row = gr[h:h + 1, :]
        b_last = b_row[:, ln - 1:ln]
        mx = mx_s[h:h + 1, 0:1]
        logw = jnp.where(causal, b_col - b_row + ig_row, NEG)
        g = b_col + mx
        m_t = jnp.maximum(g, jnp.max(logw, axis=1, keepdims=True))
        w = jnp.exp(logw - m_t)
        inter = jnp.exp(g - m_t)
        qh = q_ref[:, cs]
        kh = k_ref[:, cs]
        vh = v_ref[:, cs]
        a = w * _dot_nt(qh, kh)
        mem = mem_s[h]
        nrm = nrm_s[h:h + 1, :]
        num = _dot(a.astype(BF16), vh) + inter * _dot(qh, mem.astype(BF16))
        den = jnp.sum(a, axis=1, keepdims=True) + inter * jnp.sum(qh.astype(F32) * nrm, axis=1, keepdims=True)
        hout = num / jnp.maximum(jnp.abs(den), jnp.exp(-m_t))
        logs = b_last - b_col + ig_col
        m_new = jnp.maximum(b_last + mx, jnp.max(logs, axis=0, keepdims=True))
        decay = jnp.exp(b_last + mx - m_new)
        kw = kh.astype(F32) * jnp.exp(logs - m_new)
        mem_s[h] = decay * mem + _dot_tn(kw.astype(BF16), vh)
        nrm_s[h:h + 1, :] = decay * nrm + jnp.sum(kw, axis=0, keepdims=True)
        mx_s[h:h + 1, :] = jnp.broadcast_to(m_new, (1, mx_s.shape[1]))
        hh = hout * jax.nn.sigmoid(o_ref[:, cs].astype(F32))
        a_ref[:, cs] = (_rms(hh, hg_ref[:, cs]) + sk_ref[:, cs] * xcv_ref[:, cs].astype(F32)).astype(BF16)

    @pl.when(c == nc - 1)
    def _():
        memo_ref[...] = mem_s[...]
        nrmo_ref[...] = nrm_s[...]
        mxo_ref[...] = mx_s[...]


def _mlstm_scan(q, k, v, o, xcv, gates, hg, sk, mem0, nrm0, mx0, row0, nb, nc, ln):
    d = q.shape[1]
    nrow = nb * nc * ln
    gsl = lax.slice_in_dim(gates, row0 * ln, row0 * ln + nrow, axis=0)[:, :16]
    gr = jnp.transpose(gsl.reshape(nb * nc, ln, 16), (0, 2, 1))
    r = jnp.arange(ln)
    tri = (r[None, :] <= r[:, None]).astype(BF16)
    chunk = lambda b, c: (row0 + b * nc + c, 0)
    seq4 = lambda b, c: (b, 0, 0, 0)
    seq3 = lambda b, c: (b, 0, 0)
    kern = functools.partial(_mlstm_scan_kernel, nc=nc)
    return pl.pallas_call(
        kern,
        grid=(nb, nc),
        in_specs=[pl.BlockSpec((ln, d), chunk)] * 5 + [
            pl.BlockSpec((ln, LANES), chunk),
            pl.BlockSpec((None, 16, ln), lambda b, c: (b * nc + c, 0, 0)),
            _const_spec((ln, ln)),
            _const_spec((ln, ln)),
            _const_spec((1, d)),
            _const_spec((1, d)),
            pl.BlockSpec((None, C_HEADS, C_HEAD_DIM, C_HEAD_DIM), seq4),
            pl.BlockSpec((None, 8, C_HEAD_DIM), seq3),
            pl.BlockSpec((None, 8, LANES), seq3),
        ],
        out_specs=[
            pl.BlockSpec((ln, d), lambda b, c: (b * nc + c, 0)),
            pl.BlockSpec((None, C_HEADS, C_HEAD_DIM, C_HEAD_DIM), seq4),
            pl.BlockSpec((None, 8, C_HEAD_DIM), seq3),
            pl.BlockSpec((None, 8, LANES), seq3),
        ],
        out_shape=[
            jax.ShapeDtypeStruct((nrow, d), BF16),
            jax.ShapeDtypeStruct((nb, C_HEADS, C_HEAD_DIM, C_HEAD_DIM), F32),
            jax.ShapeDtypeStruct((nb, 8, C_HEAD_DIM), F32),
            jax.ShapeDtypeStruct((nb, 8, LANES), F32),
        ],
        scratch_shapes=[
            pltpu.VMEM((C_HEADS, C_HEAD_DIM, C_HEAD_DIM), F32),
            pltpu.VMEM((8, C_HEAD_DIM), F32),
            pltpu.VMEM((8, LANES), F32),
        ],
        compiler_params=_cparams(("parallel", "arbitrary")),
        name="mlstm_scan",
    )(q, k, v, o, xcv, gates, gr, tri, tri.T, hg, sk, mem0, nrm0, mx0)


def _pad_heads(a, width):
    nb = a.shape[0]
    if a.ndim == 2:
        a = jnp.broadcast_to(a[:, :, None], (nb, C_HEADS, width))
    return jnp.concatenate([a.astype(F32), jnp.zeros((nb, 8 - C_HEADS, width), F32)], axis=1)


def kernel(x_prompt, x_sample, c_prompt, c_sample, cache_a_k, cache_a_v, cache_b_k, cache_b_v, state_c_mem, state_c_norm, state_c_max, state_c_conv, norm_g, w_mod, b_mod, t5_bias, ab_w_in, ab_qk_g, ab_lambda, ab_head_g, ab_rel_bias, ab_w_out, c_w_in, c_b_if, c_conv_w, c_conv_b, c_w_qkv, c_head_g, c_skip, c_w_out, router_w, router_b, exp_w1, exp_w3, exp_w2, sh_w1, sh_w3, sh_w2):
    bp, s, d = x_prompt.shape
    bs, t = x_sample.shape[:2]
    depth = norm_g.shape[0]
    past = cache_a_k.shape[2]
    lb = cache_b_k.shape[2]
    n_p = bp * s
    n_s = bs * t
    n_all = n_p + n_s
    assert s % TM == 0 and n_s % TM == 0 and TM % t == 0 and t & (t - 1) == 0
    assert s % TQ_DIFF == 0 and s % ML_CHUNK == 0 and s >= BAND_PAST + TQ_BAND and TQ_DIFF >= T5_MAX_DIST
    assert past % CHUNK == 0 and lb == BAND_PAST and t <= CHUNK and t >= C_CONV - 1 and TM == BAND_PAST

    def per_token(vec):
        return vec[:bp].reshape(bp, 1, d), jnp.repeat(vec[bp:], t, axis=0)

    x = (x_prompt.reshape(n_p, d), x_sample.reshape(n_s, d))
    c_all = jnp.concatenate([c_prompt, c_sample], axis=0)
    mods = _modulation(c_all, w_mod, b_mod)

    r = jnp.arange(TM)
    tri_route = (r[:, None] < r[None, :]).astype(BF16)
    hd = jnp.arange(A_HEADS * 2 * HEAD_DIM) // HEAD_DIM
    bd = (hd[:, None] == hd[None, :]).astype(BF16)

    leaves = {}
    for l in range(depth):
        m6 = [mods[l][:, j * d:(j + 1) * d] for j in range(6)]
        sh1, sc1, gt1, sh2, sc2, gt2 = [per_token(v) for v in m6]
        i = l // 2
        if l % 2 == 0:
            lam_init = 0.8 - 0.6 * math.exp(-0.3 * l)
            lp = ab_lambda[i].astype(F32)
            lam = (jnp.exp(jnp.sum(lp[0] * lp[1])) - jnp.exp(jnp.sum(lp[2] * lp[3])) + lam_init).reshape(1)
            qkg_t = jnp.tile(ab_qk_g[i].astype(F32), (1, A_HEADS * 2))
            p16, ka_p, ka_s, va_p, va_s, kb_p, kb_s, vb_p, vb_s = _ab_in_proj(
                x, norm_g[l, 0].reshape(1, d), sh1, sc1, ab_w_in[i].astype(BF16), qkg_t, bd, bp, s)
            hg = ab_head_g[i].reshape(1, 2 * HEAD_DIM).astype(F32)
            out_scale = 1.0 - lam_init
            tiles, far = _diff_bias_prompt(t5_bias, TQ_DIFF)
            oa_p = _diff_attn_prompt(p16, tiles, far, lam, hg.reshape(2 * HEAD_DIM, 1), bp, s, out_scale)
            ob_p = _band_attn_prompt(p16, _band_bias_prompt(ab_rel_bias[i], TQ_BAND), bp, s)
            dbc, dbn = _diff_bias_sample(t5_bias, past, t)
            oa_s = _sample_attn(p16, cache_a_k[i].reshape(bs, past, -1), cache_a_v[i].reshape(bs, past, -1),
                                dbc, dbn, lam, hg, n_p, bs, t, True, out_scale)
            bbc, bbn = _band_bias_sample(ab_rel_bias[i], past, lb, t)
            ob_s = _sample_attn(p16, cache_b_k[i].reshape(bs, lb, -1), cache_b_v[i].reshape(bs, lb, -1),
                                bbc, bbn, lam, hg, n_p, bs, t, False, out_scale)
            mix_in = ((oa_p, oa_s.reshape(n_s, -1)), 0, (ob_p, ob_s.reshape(n_s, -1)), 0)
            w_out16 = ab_w_out[i].astype(BF16)
            leaves.setdefault('akp', []).append(ka_p.reshape(bp, s, A_HEADS, 2, HEAD_DIM))
            leaves.setdefault('avp', []).append(va_p.reshape(bp, s, A_HEADS, 2 * HEAD_DIM))
            leaves.setdefault('aks', []).append(ka_s.reshape(bs, t, A_HEADS, 2, HEAD_DIM))
            leaves.setdefault('avs', []).append(va_s.reshape(bs, t, A_HEADS, 2 * HEAD_DIM))
            leaves.setdefault('bkp', []).append(kb_p.reshape(bp, TM, B_HEADS, HEAD_DIM))
            leaves.setdefault('bvp', []).append(vb_p.reshape(bp, TM, B_HEADS, HEAD_DIM))
            leaves.setdefault('bks', []).append(kb_s.reshape(bs, t, B_HEADS, HEAD_DIM))
            leaves.setdefault('bvs', []).append(vb_s.reshape(bs, t, B_HEADS, HEAD_DIM))
        else:
            w_in = c_w_in[i]
            wg = jnp.pad(w_in[:, 2 * d:].astype(F32), ((0, 0), (0, LANES - 2 * C_HEADS)))
            wgh = wg.astype(BF16)
            wgl = (wg - wgh.astype(F32)).astype(BF16)
            bif = jnp.pad(c_b_if[i].astype(F32).reshape(1, 2 * C_HEADS), ((0, 0), (0, LANES - 2 * C_HEADS)))
            cprev = state_c_conv[i].astype(F32)
            planes = []
            for j in range(1, C_CONV):
                rows = jnp.concatenate([cprev[:, C_CONV - 1 - j:, :], jnp.zeros((bs, t - j, d), F32)], axis=1)
                planes.append(rows.reshape(n_s // TM, TM, d))
            halo = jnp.stack(planes, axis=1)
            wqkv = c_w_qkv[i].astype(BF16)
            q, k, v, o, xcv, gates, tail, xcs = _mlstm_pre(
                x, norm_g[l, 0].reshape(1, d), sh1, sc1, w_in[:, :2 * d].astype(BF16), wgh, wgl, bif,
                c_conv_w[i].astype(F32), c_conv_b[i].reshape(1, d).astype(F32), wqkv[0], wqkv[1], wqkv[2],
                halo, bp, s, t)
            hg = c_head_g[i].reshape(1, d).astype(F32)
            sk = c_skip[i].reshape(1, d).astype(F32)
            zm = jnp.zeros((bp, C_HEADS, C_HEAD_DIM, C_HEAD_DIM), F32)
            a_p, mem_p, nrm_p, mx_p = _mlstm_scan(
                q, k, v, o, xcv, gates, hg, sk, zm, jnp.zeros((bp, 8, C_HEAD_DIM), F32),
                jnp.zeros((bp, 8, LANES), F32), 0, bp, s // ML_CHUNK, ML_CHUNK)
            a_s, mem_s, nrm_s, mx_s = _mlstm_scan(
                q, k, v, o, xcv, gates, hg, sk, state_c_mem[i].astype(F32),
                _pad_heads(state_c_norm[i], C_HEAD_DIM), _pad_heads(state_c_max[i], LANES),
                n_p // t, bs, 1, t)
            mix_in = ((a_p, a_s), 0, (a_p, a_s), 1)
            w_out16 = c_w_out[i].astype(BF16)
            leaves.setdefault('memp', []).append(mem_p)
            leaves.setdefault('normp', []).append(nrm_p[:, :C_HEADS])
            leaves.setdefault('maxp', []).append(mx_p[:, :C_HEADS, 0])
            leaves.setdefault('convp', []).append(tail[:, 8 - (C_CONV - 1):])
            leaves.setdefault('mems', []).append(mem_s)
            leaves.setdefault('norms', []).append(nrm_s[:, :C_HEADS])
            leaves.setdefault('maxs', []).append(mx_s[:, :C_HEADS, 0])
            leaves.setdefault('convs', []).append(xcs.reshape(bs, t, d)[:, t - (C_CONV - 1):])
        rw_t = router_w[l].astype(F32).T
        rw_hi = rw_t.astype(BF16)
        rw_lo = (rw_t - rw_hi.astype(F32)).astype(BF16)
        xp, xs, hp, lg_t = _out_proj(mix_in[0], mix_in[1], mix_in[2], mix_in[3], w_out16, x, gt1,
                                     norm_g[l, 1].reshape(1, d), sh2, sc2, rw_hi, rw_lo, bp, s)
        x = _moe(lg_t, hp, (xp, xs), gt2, router_b[l], exp_w1[l].astype(BF16), exp_w3[l].astype(BF16),
                     exp_w2[l].astype(BF16), sh_w1[l].astype(BF16), sh_w3[l].astype(BF16),
                     sh_w2[l].astype(BF16), tri_route, bp, s)

    order = ['akp', 'avp', 'aks', 'avs', 'bkp', 'bvp', 'bks', 'bvs',
             'memp', 'normp', 'maxp', 'convp', 'mems', 'norms', 'maxs', 'convs']
    return (x[0].reshape(bp, s, d), x[1].reshape(bs, t, d)) + tuple(
        jnp.stack(leaves[name]) for name in order)
```

```python
import functools
import math

import jax
import jax.numpy as jnp
from jax import lax
from jax.experimental import pallas as pl
from jax.experimental.pallas import tpu as pltpu

F32 = jnp.float32
BF16 = jnp.bfloat16
I32 = jnp.int32
U32 = jnp.uint32

EPS = 1e-6
NEG = -1e30
CHUNK = 64
HEAD_DIM = 64
A_HEADS = 4
B_HEADS = 8
BAND_CHUNKS = 8
BAND_PAST = BAND_CHUNKS * CHUNK
REL_CLIP = 128
T5_BUCKETS = 32
T5_MAX_DIST = 128
C_HEADS = 4
C_HEAD_DIM = 256
C_CONV = 4
N_EXPERTS = 64
TOP_K = 8
ROUTE_SCALE = 2.5

LANES = 128
TM = 512
TQ_DIFF = 512
TQ_BAND = 128
ML_CHUNK = 256
BM = 512
TM_MOVE = 256
VMEM_LIMIT = 56 * 1024 * 1024


def _cparams(sem):
    return pltpu.CompilerParams(dimension_semantics=sem, vmem_limit_bytes=VMEM_LIMIT)


def _dot(a, b):
    return jnp.dot(a, b, preferred_element_type=F32)


def _dot_nt(a, b):
    return lax.dot_general(a, b, (((1,), (1,)), ((), ())), preferred_element_type=F32)


def _dot_tn(a, b):
    return lax.dot_general(a, b, (((0,), (0,)), ((), ())), preferred_element_type=F32)


def _split2(x):
    hi = x.astype(BF16)
    lo = (x - hi.astype(F32)).astype(BF16)
    return hi, lo


def _split3(x):
    p0 = x.astype(BF16)
    r1 = x - p0.astype(F32)
    p1 = r1.astype(BF16)
    p2 = (r1 - p1.astype(F32)).astype(BF16)
    return p0, p1, p2


def _pack_rows(x):
    half = x.shape[1] // 2
    bits = lax.bitcast_convert_type(x.astype(BF16).astype(F32), U32)
    return bits[:, :half] | (bits[:, half:] >> 16)


def _unpack_rows(p):
    return (lax.bitcast_convert_type(p & jnp.uint32(0xFFFF0000), F32),
            lax.bitcast_convert_type(p << 16, F32))


def _rms(x, g):
    return x * lax.rsqrt(jnp.mean(x * x, axis=-1, keepdims=True) + EPS) * g


def _mod_specs(tile, d, npt, tps, bp):
    return [pl.BlockSpec((None, 1, d), lambda i: (jnp.minimum(i // tps, bp - 1), 0, 0)),
            pl.BlockSpec((tile, d), lambda i: (jnp.maximum(i - npt, 0), 0), pipeline_mode=pl.Buffered(1))]


def _tok_specs(tile, width, npt, col=0):
    return [pl.BlockSpec((tile, width), lambda i: (jnp.minimum(i, npt - 1), col)),
            pl.BlockSpec((tile, width), lambda i: (jnp.maximum(i - npt, 0), col))]


def _by_kind(i, npt, body):
    @pl.when(i < npt)
    def _():
        body(False)

    @pl.when(i >= npt)
    def _():
        body(True)


def _pick(sample, p_ref, s_ref):
    return s_ref[...] if sample else p_ref[...]


def _store_tok(sample, p_ref, s_ref, val):
    (s_ref if sample else p_ref)[...] = val


def _const_spec(shape):
    nd = len(shape)
    return pl.BlockSpec(shape, lambda *_: (0,) * nd, pipeline_mode=pl.Buffered(1))


def _mod_kernel(c_ref, w_ref, b_ref, o_ref):
    c = c_ref[...]
    a_hi, a_lo = _split2(c * jax.nn.sigmoid(c))
    w_hi, w_lo = _split2(w_ref[...])
    o_ref[...] = _dot(a_hi, w_hi) + _dot(a_hi, w_lo) + _dot(a_lo, w_hi) + b_ref[...]


def _modulation(c_all, w_mod, b_mod):
    depth, d, n6 = w_mod.shape
    nseq = c_all.shape[0]
    tn = 512
    return pl.pallas_call(
        _mod_kernel,
        grid=(depth, n6 // tn),
        in_specs=[
            pl.BlockSpec((nseq, d), lambda l, j: (0, 0)),
            pl.BlockSpec((None, d, tn), lambda l, j: (l, 0, j)),
            pl.BlockSpec((None, 1, tn), lambda l, j: (l, 0, j)),
        ],
        out_specs=pl.BlockSpec((None, nseq, tn), lambda l, j: (l, 0, j)),
        out_shape=jax.ShapeDtypeStruct((depth, nseq, n6), F32),
        compiler_params=_cparams(("parallel", "parallel")),
        name="modulation",
    )(c_all, w_mod, b_mod.reshape(depth, 1, n6))


def _ab_in_kernel(xp_ref, xs_ref, g_ref, shp_ref, shs_ref, scp_ref, scs_ref, w_ref, qkg_ref, bd_ref,
                  p16_ref, kap_ref, kas_ref, vap_ref, vas_ref, kbp_ref, kbs_ref, vbp_ref, vbs_ref, *, npt, tps):
    i = pl.program_id(0)
    wa = A_HEADS * 2 * HEAD_DIM

    def group_norm(seg, gi):
        hi, lo = _split2(seg * seg)
        bd = bd_ref[...]
        ss = _dot(hi, bd) + _dot(lo, bd)
        return seg * lax.rsqrt(ss * (1.0 / HEAD_DIM) + EPS) * qkg_ref[gi:gi + 1, :]

    def body(sample):
        h = (_rms(_pick(sample, xp_ref, xs_ref), g_ref[...]) * (1.0 + _pick(sample, scp_ref, scs_ref))
             + _pick(sample, shp_ref, shs_ref))
        y = _dot(h.astype(BF16), w_ref[...])
        qa = group_norm(y[:, 0 * wa:1 * wa], 0)
        ka = group_norm(y[:, 1 * wa:2 * wa], 1)
        va = y[:, 2 * wa:3 * wa]
        qb = group_norm(y[:, 3 * wa:4 * wa], 2)
        kb = group_norm(y[:, 4 * wa:5 * wa], 3)
        vb = y[:, 5 * wa:6 * wa]
        scale = HEAD_DIM ** -0.5
        p16_ref[:, 0 * wa:1 * wa] = (qa * scale).astype(BF16)
        p16_ref[:, 1 * wa:2 * wa] = ka.astype(BF16)
        p16_ref[:, 2 * wa:3 * wa] = va.astype(BF16)
        p16_ref[:, 3 * wa:4 * wa] = (qb * scale).astype(BF16)
        p16_ref[:, 4 * wa:5 * wa] = kb.astype(BF16)
        p16_ref[:, 5 * wa:6 * wa] = vb.astype(BF16)
        _store_tok(sample, kap_ref, kas_ref, ka)
        _store_tok(sample, vap_ref, vas_ref, va)
        if sample:
            kbs_ref[...] = kb
            vbs_ref[...] = vb
        else:
            @pl.when(i % tps == tps - 1)
            def _():
                kbp_ref[...] = kb
                vbp_ref[...] = vb

    _by_kind(i, npt, body)


def _ab_in_proj(x, g, sh, sc, w16, qkg_t, bd, bp, s):
    n_p, d = x[0].shape
    n_s = x[1].shape[0]
    n = n_p + n_s
    wa = A_HEADS * 2 * HEAD_DIM
    n_in = w16.shape[1]
    tok = lambda i: (i, 0)
    npt = n_p // TM
    tps = s // TM
    mod = _mod_specs(TM, d, npt, tps, bp)
    leaf = _tok_specs(TM, wa, npt)
    band = [pl.BlockSpec((None, TM, wa), lambda i: (jnp.minimum(i // tps, bp - 1), 0, 0)), leaf[1]]
    f32 = lambda rows: jax.ShapeDtypeStruct((rows, wa), F32)
    band_shape = [jax.ShapeDtypeStruct((bp, TM, wa), F32), f32(n_s)]
    return pl.pallas_call(
        functools.partial(_ab_in_kernel, npt=npt, tps=tps),
        grid=(n // TM,),
        in_specs=[
            *_tok_specs(TM, d, npt),
            _const_spec((1, d)),
            *mod,
            *mod,
            _const_spec((d, n_in)),
            _const_spec((4, wa)),
            _const_spec((wa, wa)),
        ],
        out_specs=[pl.BlockSpec((TM, n_in), tok)] + leaf + leaf + band + band,
        out_shape=[jax.ShapeDtypeStruct((n, n_in), BF16), f32(n_p), f32(n_s), f32(n_p), f32(n_s)]
        + band_shape + band_shape,
        compiler_params=_cparams(("arbitrary",)),
        name="ab_in_proj",
    )(*x, g, *sh, *sc, w16, qkg_t, bd)


def _t5_bucket(rel):
    half = T5_BUCKETS // 2
    exact = half // 2
    n = jnp.abs(rel)
    large = exact + (jnp.log(jnp.maximum(n, 1).astype(F32) / exact)
                     / math.log(T5_MAX_DIST / exact) * (half - exact)).astype(I32)
    large = jnp.minimum(large, half - 1)
    return jnp.where(rel > 0, half, 0) + jnp.where(n < exact, n, large)


def _lookup(table, idx):
    onehot = (idx[..., None] == jnp.arange(table.shape[0], dtype=I32)).astype(F32)
    return jnp.einsum('...n,nh->...h', onehot, table.astype(F32), precision=lax.Precision.HIGHEST)


def _diff_bias_prompt(t5_bias, tq):
    i = jnp.arange(tq)[None, :]
    j = jnp.arange(tq)[:, None]
    diag = jnp.where(((j // CHUNK) <= (i // CHUNK))[..., None], _lookup(t5_bias, _t5_bucket(j - i)), NEG)
    prev = _lookup(t5_bias, _t5_bucket(j - i - tq))
    first = jnp.concatenate([diag, jnp.full_like(diag, NEG)], axis=0)
    later = jnp.concatenate([prev, diag], axis=0)
    tiles = jnp.transpose(jnp.stack([first, later]), (3, 0, 1, 2))
    tiles = jnp.concatenate([tiles, tiles], axis=3)
    far = _lookup(t5_bias, _t5_bucket(jnp.full((1,), -T5_MAX_DIST, I32)))[0]
    return tiles, far


def _diff_bias_sample(t5_bias, past, t):
    qpos = past + jnp.arange(t)
    kpos = jnp.arange(past + t)
    rel = kpos[None, :] - qpos[:, None]
    vis = (kpos[None, :] // CHUNK) <= (qpos[:, None] // CHUNK)
    b = jnp.where(vis[..., None], _lookup(t5_bias, _t5_bucket(rel)), NEG)
    b = jnp.transpose(b, (2, 0, 1))
    b = jnp.concatenate([b, b], axis=1)
    return b[:, :, :past], b[:, :, past:]


def _band_bias_prompt(rel_bias, tq):
    nvar = BAND_PAST // tq + 1
    win = BAND_PAST + tq
    u = jnp.arange(nvar)[:, None, None]
    i = jnp.arange(tq)[None, :, None]
    j = jnp.arange(win)[None, None, :]
    qp = u * tq + i
    qc = qp // CHUNK
    kc = j // CHUNK
    valid = (kc <= qc) & (kc >= qc - BAND_CHUNKS)
    b = _lookup(rel_bias, jnp.clip(j - qp, -REL_CLIP, REL_CLIP) + REL_CLIP)
    b = jnp.where(valid[..., None], b, NEG)
    b = jnp.transpose(b, (0, 3, 1, 2))
    return b.reshape(nvar, B_HEADS // 2, 2 * tq, win)


def _band_bias_sample(rel_bias, past, lb, t):
    qpos = past + jnp.arange(t)
    kpos = past - lb + jnp.arange(lb + t)
    band_lo = (past // CHUNK - BAND_CHUNKS) * CHUNK
    rel = jnp.clip(kpos[None, :] - qpos[:, None], -REL_CLIP, REL_CLIP) + REL_CLIP
    b = jnp.where((kpos >= band_lo)[None, :, None], _lookup(rel_bias, rel), NEG)
    b = jnp.transpose(b, (2, 0, 1)).reshape(B_HEADS // 2, 2 * t, lb + t)
    return b[:, :, :lb], b[:, :, lb:]


def _stack_halves(q):
    lane = lax.broadcasted_iota(I32, q.shape, 1)
    zero = jnp.zeros_like(q)
    return jnp.concatenate([jnp.where(lane < HEAD_DIM, q, zero), jnp.where(lane >= HEAD_DIM, q, zero)], axis=0)


def _diff_finish(o1, o2, lam, hg, out_scale):
    o = o1 - lam * o2
    return (_rms(o, hg) * out_scale).astype(BF16)


def _band_finish(o):
    tq = o.shape[0] // 2
    lane = lax.broadcasted_iota(I32, (tq, o.shape[1]), 1)
    return jnp.where(lane < HEAD_DIM, o[:tq], o[tq:]).astype(BF16)


def _diff_prompt_kernel(far_ref, lam_ref, q_ref, k_ref, v_ref, bias_ref, hg_ref, o_ref,
                        q2t_s, vt_s, m_s, l_s, acc_s, *, tq, out_scale):
    h = pl.program_id(1)
    i = pl.program_id(2)

    @pl.when(i == 0)
    def _():
        for jj in range(vt_s.shape[0]):
            vt_s[jj] = v_ref[jj * tq:(jj + 1) * tq, :].astype(F32).T.astype(BF16)

    qt = q_ref[...].astype(F32).T.astype(BF16)
    row = lax.broadcasted_iota(I32, qt.shape, 0)
    zero = jnp.zeros_like(qt)
    q2t_s[:, 0:tq] = jnp.where(row < HEAD_DIM, qt, zero)
    q2t_s[:, tq:2 * tq] = jnp.where(row >= HEAD_DIM, qt, zero)
    m_s[...] = jnp.full(m_s.shape, NEG, F32)
    l_s[...] = jnp.zeros(l_s.shape, F32)
    acc_s[...] = jnp.zeros(acc_s.shape, F32)

    def step(jb, nblk, bias, shift):
        kb = k_ref[pl.ds(pl.multiple_of(jb * tq, tq), nblk * tq), :]
        s = _dot(kb, q2t_s[...])
        if bias is not None:
            s = s + bias
        cmax = jnp.max(s, axis=0, keepdims=True)
        if shift is not None:
            cmax = cmax + shift
        m_prev = m_s[...]
        m_new = jnp.maximum(m_prev, cmax)
        alpha = jnp.exp(m_prev - m_new)
        p = jnp.exp(s - (m_new if shift is None else m_new - shift))
        l_s[...] = alpha * l_s[...] + jnp.sum(p, axis=0, keepdims=True)
        pb = p.astype(BF16)
        pv = _dot(vt_s[jb], pb[0:tq])
        for u in range(1, nblk):
            pv = pv + _dot(vt_s[jb + u], pb[u * tq:(u + 1) * tq])
        acc_s[...] = alpha * acc_s[...] + pv
        m_s[...] = m_new

    far = far_ref[h]
    nfar = jnp.maximum(i - 1, 0)

    def far_pair(j, carry):
        step(2 * j, 2, None, far)
        return carry

    lax.fori_loop(0, nfar // 2, far_pair, 0)

    @pl.when(nfar % 2 == 1)
    def _():
        step(nfar - 1, 1, None, far)

    step(nfar, 2, bias_ref[...], None)
    o = acc_s[...] / l_s[...]
    od = o[:, 0:tq] - lam_ref[0] * o[:, tq:2 * tq]
    on = od * lax.rsqrt(jnp.mean(od * od, axis=0, keepdims=True) + EPS) * hg_ref[...] * out_scale
    o_ref[...] = on.T.astype(BF16)


def _diff_attn_prompt(p16, tiles, far, lam, hg_col, bp, s, out_scale):
    tq = TQ_DIFF
    nq = s // tq
    wa = A_HEADS * LANES
    kern = functools.partial(_diff_prompt_kernel, tq=tq, out_scale=out_scale)
    return pl.pallas_call(
        kern,
        grid=(bp, A_HEADS, nq),
        in_specs=[
            pl.BlockSpec(memory_space=pltpu.SMEM),
            pl.BlockSpec(memory_space=pltpu.SMEM),
            pl.BlockSpec((tq, LANES), lambda b, h, i: (b * nq + i, h)),
            pl.BlockSpec((s, LANES), lambda b, h, i: (b, A_HEADS + h)),
            pl.BlockSpec((s, LANES), lambda b, h, i: (b, 2 * A_HEADS + h)),
            pl.BlockSpec((None, None, 2 * tq, 2 * tq), lambda b, h, i: (h, jnp.minimum(i, 1), 0, 0)),
            _const_spec((LANES, 1)),
        ],
        out_specs=pl.BlockSpec((tq, LANES), lambda b, h, i: (b * nq + i, h)),
        out_shape=jax.ShapeDtypeStruct((bp * s, wa), BF16),
        scratch_shapes=[
            pltpu.VMEM((LANES, 2 * tq), BF16),
            pltpu.VMEM((nq, LANES, tq), BF16),
            pltpu.VMEM((1, 2 * tq), F32),
            pltpu.VMEM((1, 2 * tq), F32),
            pltpu.VMEM((LANES, 2 * tq), F32),
        ],
        compiler_params=_cparams(("parallel", "parallel", "arbitrary")),
        name="diff_attn_prompt",
    )(far, lam, p16, p16, p16, tiles, hg_col)


def _band_prompt_kernel(q_ref, k_ref, v_ref, bias_ref, o_ref, *, tq, win):
    t = pl.program_id(2)
    start = pl.multiple_of(jnp.maximum(t * tq - BAND_PAST, 0), tq)
    for c in range(q_ref.shape[1] // LANES):
        cs = slice(c * LANES, (c + 1) * LANES)
        kb = k_ref[pl.ds(start, win), cs]
        vb = v_ref[pl.ds(start, win), cs]
        s = _dot_nt(_stack_halves(q_ref[:, cs]), kb) + bias_ref[c]
        m = jnp.max(s, axis=-1, keepdims=True)
        p = jnp.exp(s - m)
        l = jnp.sum(p, axis=-1, keepdims=True)
        o_ref[:, cs] = _band_finish(_dot(p.astype(BF16), vb) / l)


def _band_attn_prompt(p16, bias, bp, s):
    tq = TQ_BAND
    nq = s // tq
    win = BAND_PAST + tq
    nvar = bias.shape[0]
    npair = B_HEADS // 2
    pps = 4
    wide = pps * LANES
    c0 = 3 * A_HEADS // pps
    ng = npair // pps
    kern = functools.partial(_band_prompt_kernel, tq=tq, win=win)
    return pl.pallas_call(
        kern,
        grid=(bp, ng, nq),
        in_specs=[
            pl.BlockSpec((tq, wide), lambda b, p, t: (b * nq + t, c0 + p)),
            pl.BlockSpec((s, wide), lambda b, p, t: (b, c0 + ng + p)),
            pl.BlockSpec((s, wide), lambda b, p, t: (b, c0 + 2 * ng + p)),
            pl.BlockSpec((None, pps, 2 * tq, win), lambda b, p, t: (jnp.minimum(t, nvar - 1), p, 0, 0)),
        ],
        out_specs=pl.BlockSpec((tq, wide), lambda b, p, t: (b * nq + t, p)),
        out_shape=jax.ShapeDtypeStruct((bp * s, npair * LANES), BF16),
        compiler_params=_cparams(("parallel", "parallel", "parallel")),
        name="band_attn_prompt",
    )(p16, p16, p16, bias)


def _sample_attn_kernel(lam_ref, q_ref, kc_ref, vc_ref, kn_ref, vn_ref, bc_ref, bn_ref, hg_ref, o_ref,
                        *, diff, out_scale):
    t = q_ref.shape[0]
    q2 = _stack_halves(q_ref[...])
    sc = _dot_nt(q2, kc_ref[...].astype(BF16)) + bc_ref[...]
    sn = _dot_nt(q2, kn_ref[...]) + bn_ref[...]
    m = jnp.maximum(jnp.max(sc, axis=-1, keepdims=True), jnp.max(sn, axis=-1, keepdims=True))
    pc = jnp.exp(sc - m)
    pn = jnp.exp(sn - m)
    l = jnp.sum(pc, axis=-1, keepdims=True) + jnp.sum(pn, axis=-1, keepdims=True)
    o = (_dot(pc.astype(BF16), vc_ref[...].astype(BF16)) + _dot(pn.astype(BF16), vn_ref[...])) / l
    if diff:
        o_ref[...] = _diff_finish(o[:t], o[t:], lam_ref[0], hg_ref[...], out_scale)
    else:
        o_ref[...] = _band_finish(o)


def _sample_attn(p16, cache_k, cache_v, bias_c, bias_n, lam, hg, np_rows, bs, t, diff, out_scale):
    past = cache_k.shape[1]
    ncol = cache_k.shape[2] // LANES
    row0 = np_rows // t
    if diff:
        qc, kc, vc = 0, A_HEADS, 2 * A_HEADS
    else:
        qc, kc, vc = 3 * A_HEADS, 3 * A_HEADS + ncol, 3 * A_HEADS + 2 * ncol
    kern = functools.partial(_sample_attn_kernel, diff=diff, out_scale=out_scale)
    return pl.pallas_call(
        kern,
        grid=(bs, ncol),
        in_specs=[
            pl.BlockSpec(memory_space=pltpu.SMEM),
            pl.BlockSpec((t, LANES), lambda b, h: (row0 + b, qc + h)),
            pl.BlockSpec((None, past, LANES), lambda b, h: (b, 0, h)),
            pl.BlockSpec((None, past, LANES), lambda b, h: (b, 0, h)),
            pl.BlockSpec((t, LANES), lambda b, h: (row0 + b, kc + h)),
            pl.BlockSpec((t, LANES), lambda b, h: (row0 + b, vc + h)),
            pl.BlockSpec((None, 2 * t, past), lambda b, h: (h, 0, 0)),
            pl.BlockSpec((None, 2 * t, t), lambda b, h: (h, 0, 0)),
            _const_spec((1, LANES)),
        ],
        out_specs=pl.BlockSpec((None, t, LANES), lambda b, h: (b, 0, h)),
        out_shape=jax.ShapeDtypeStruct((bs, t, ncol * LANES), BF16),
        compiler_params=_cparams(("parallel", "parallel")),
        name="diff_attn_sample" if diff else "band_attn_sample",
    )(lam, p16, cache_k, cache_v, p16, p16, bias_c, bias_n, hg)


def _out_proj_kernel(a0p_ref, a0s_ref, a1p_ref, a1s_ref, w_ref, xp_ref, xs_ref, gtp_ref, gts_ref, g2_ref,
                     shp_ref, shs_ref, scp_ref, scs_ref, rwh_ref, rwl_ref, xop_ref, xos_ref, hp_ref, lg_ref,
                     *, npt):
    half = a0p_ref.shape[1]

    def body(sample):
        mix = (_dot(_pick(sample, a0p_ref, a0s_ref), w_ref[0:half, :])
               + _dot(_pick(sample, a1p_ref, a1s_ref), w_ref[half:2 * half, :]))
        x = _pick(sample, xp_ref, xs_ref) + _pick(sample, gtp_ref, gts_ref) * mix
        _store_tok(sample, xop_ref, xos_ref, x)
        h2 = _rms(x, g2_ref[...]) * (1.0 + _pick(sample, scp_ref, scs_ref)) + _pick(sample, shp_ref, shs_ref)
        hp_ref[...] = _pack_rows(h2)
        h_hi, h_lo = _split2(h2)
        rw_hi = rwh_ref[...]
        lg_ref[...] = _dot_nt(rw_hi, h_hi) + _dot_nt(rw_hi, h_lo) + _dot_nt(rwl_ref[...], h_hi)

    _by_kind(pl.program_id(0), npt, body)


def _out_proj(a0, c0, a1, c1, w16, x, gt, g2, sh, sc, rw_hi, rw_lo, bp, s):
    n_p, d = x[0].shape
    n_s = x[1].shape[0]
    n = n_p + n_s
    half = d // 2
    ne = rw_hi.shape[0]
    tok = lambda i: (i, 0)
    npt = n_p // TM
    mod = _mod_specs(TM, d, npt, s // TM, bp)
    xspecs = _tok_specs(TM, d, npt)
    return pl.pallas_call(
        functools.partial(_out_proj_kernel, npt=npt),
        grid=(n // TM,),
        in_specs=[
            *_tok_specs(TM, half, npt, c0),
            *_tok_specs(TM, half, npt, c1),
            _const_spec((d, d)),
            *xspecs,
            *mod,
            _const_spec((1, d)),
            *mod,
            *mod,
            _const_spec((ne, d)),
            _const_spec((ne, d)),
        ],
        out_specs=xspecs + [pl.BlockSpec((TM, half), tok), pl.BlockSpec((ne, TM), lambda i: (0, i))],
        out_shape=[jax.ShapeDtypeStruct((n_p, d), F32), jax.ShapeDtypeStruct((n_s, d), F32),
                   jax.ShapeDtypeStruct((n, half), U32), jax.ShapeDtypeStruct((ne, n), F32)],
        compiler_params=_cparams(("arbitrary",)),
        name="out_proj",
    )(*a0, *a1, w16, *x, *gt, g2, *sh, *sc, rw_hi, rw_lo)


def _route_kernel(lg_ref, rb_ref, tri_ref, idx_ref, gate_ref, rank_ref, cnt_ref, carry_s):
    @pl.when(pl.program_id(0) == 0)
    def _():
        carry_s[...] = jnp.zeros(carry_s.shape, F32)

    s = jax.nn.sigmoid(lg_ref[...])
    sb = s + rb_ref[...]
    row = lax.broadcasted_iota(I32, s.shape, 0).astype(F32)
    picks = []
    sel = jnp.zeros(s.shape, F32)
    for _ in range(TOP_K):
        m = jnp.max(sb, axis=0, keepdims=True)
        ik = jnp.min(jnp.where(sb == m, row, float(N_EXPERTS)), axis=0, keepdims=True)
        oh = row == ik
        picks.append((ik, oh, jnp.sum(jnp.where(oh, s, 0.0), axis=0, keepdims=True)))
        sel = sel + oh.astype(F32)
        sb = jnp.where(oh, -jnp.inf, sb)
    before = _dot(sel.astype(BF16), tri_ref[...]) + carry_s[...]
    gsum = functools.reduce(lambda a, b: a + b, [g for _, _, g in picks])
    for k, (ik, oh, g) in enumerate(picks):
        idx_ref[k:k + 1, :] = ik.astype(I32)
        gate_ref[k:k + 1, :] = g / gsum * ROUTE_SCALE
        rank_ref[k:k + 1, :] = jnp.sum(jnp.where(oh, before, 0.0), axis=0, keepdims=True).astype(I32)
    carry_s[...] = carry_s[...] + jnp.sum(sel, axis=1, keepdims=True)
    cnt_ref[...] = carry_s[...]


def _route(lg_t, rb, tri):
    ne, n = lg_t.shape
    tm = tri.shape[0]
    tokk = lambda i: (0, i)
    return pl.pallas_call(
        _route_kernel,
        grid=(n // tm,),
        in_specs=[pl.BlockSpec((ne, tm), tokk), _const_spec((ne, 1)), _const_spec((tm, tm))],
        out_specs=[pl.BlockSpec((TOP_K, tm), tokk)] * 3 + [_const_spec((ne, 1))],
        out_shape=[jax.ShapeDtypeStruct((TOP_K, n), I32), jax.ShapeDtypeStruct((TOP_K, n), F32),
                   jax.ShapeDtypeStruct((TOP_K, n), I32), jax.ShapeDtypeStruct((ne, 1), F32)],
        scratch_shapes=[pltpu.VMEM((ne, 1), F32)],
        compiler_params=_cparams(("arbitrary",)),
        name="moe_route",
    )(lg_t, rb, tri)


def _dispatch_kernel(pad_ref, dest_ref, h_ref, xs_ref, zero_s, sem, *, tm):
    i = pl.program_id(0)
    nrow = zero_s.shape[0]

    @pl.when(i == 0)
    def _():
        zero_s[...] = jnp.zeros(zero_s.shape, U32)

        def fill(start):
            cp = pltpu.make_async_copy(zero_s, xs_ref.at[pl.ds(pl.multiple_of(start, 8), nrow), :], sem)
            cp.start()
            cp.wait()

        def fill_pad(e, c):
            fill(pad_ref[e] // 8 * 8)
            return c

        lax.fori_loop(0, N_EXPERTS, fill_pad, 0)
        total = xs_ref.shape[0]
        tail = pad_ref[N_EXPERTS]

        def fill_tail(j, c):
            fill(jnp.minimum(tail + j * nrow, total - nrow))
            return c

        lax.fori_loop(0, (total - tail + nrow - 1) // nrow, fill_tail, 0)

    def issue_row(r, c):
        for k in range(TOP_K):
            d = dest_ref[r * TOP_K + k]
            pltpu.make_async_copy(h_ref.at[pl.ds(r, 1), :], xs_ref.at[pl.ds(d, 1), :], sem).start(priority=k % 2)
        return c

    lax.fori_loop(0, tm, issue_row, 0)
    pltpu.make_async_copy(xs_ref.at[pl.ds(0, tm * TOP_K), :], xs_ref.at[pl.ds(0, tm * TOP_K), :], sem).wait()


def _dispatch(pad_start, dest_flat, hp, cap):
    n, d = hp.shape
    tm = TM_MOVE
    kern = functools.partial(_dispatch_kernel, tm=tm)
    return pl.pallas_call(
        kern,
        grid_spec=pltpu.PrefetchScalarGridSpec(
            num_scalar_prefetch=1,
            grid=(n // tm,),
            in_specs=[
                pl.BlockSpec((tm * TOP_K,), lambda i, ps: (i,), memory_space=pltpu.SMEM),
                pl.BlockSpec((tm, d), lambda i, ps: (i, 0)),
            ],
            out_specs=pl.BlockSpec(memory_space=pl.ANY),
            scratch_shapes=[pltpu.VMEM((BM + 8, d), U32), pltpu.SemaphoreType.DMA(())],
        ),
        out_shape=jax.ShapeDtypeStruct((cap + BM + 8, d), U32),
        compiler_params=_cparams(("arbitrary",)),
        name="moe_dispatch",
    )(pad_start, dest_flat, hp)


def _experts_kernel(exp_ref, nused_ref, x_ref, w1_ref, w3_ref, w2_ref, y_ref, w1_s, w3_s, w2_s):
    i = pl.program_id(0)
    used = i < nused_ref[0]

    @pl.when(jnp.logical_or(i == 0, exp_ref[i] != exp_ref[jnp.maximum(i - 1, 0)]))
    def _():
        w1_s[...] = w1_ref[...].astype(BF16)
        w3_s[...] = w3_ref[...].astype(BF16)
        w2_s[...] = w2_ref[...].astype(BF16)

    @pl.when(used)
    def _():
        y_ref[...] = _pack_rows(_swiglu_packed(x_ref[...], w1_s, w3_s, w2_s))

    @pl.when(jnp.logical_not(used))
    def _():
        y_ref[...] = jnp.zeros(y_ref.shape, U32)


def _swiglu_packed(xp, w1_ref, w3_ref, w2_ref):
    half = xp.shape[1]
    hi, lo = [v.astype(BF16) for v in _unpack_rows(xp)]
    a = _dot(hi, w1_ref[0:half, :]) + _dot(lo, w1_ref[half:2 * half, :])
    b = _dot(hi, w3_ref[0:half, :]) + _dot(lo, w3_ref[half:2 * half, :])
    return _dot((a * jax.nn.sigmoid(a) * b).astype(BF16), w2_ref[...])


def _experts(blk_e, n_used, xs, w1, w3, w2, layer):
    d = w1.shape[2]
    de = w1.shape[3]
    nb = blk_e.shape[0]
    return pl.pallas_call(
        _experts_kernel,
        grid_spec=pltpu.PrefetchScalarGridSpec(
            num_scalar_prefetch=2,
            grid=(nb,),
            in_specs=[
                pl.BlockSpec((BM, d // 2), lambda i, e, u: (i, 0)),
                pl.BlockSpec((None, None, d, de), lambda i, e, u: (layer, e[i], 0, 0)),
                pl.BlockSpec((None, None, d, de), lambda i, e, u: (layer, e[i], 0, 0)),
                pl.BlockSpec((None, None, de, d), lambda i, e, u: (layer, e[i], 0, 0)),
            ],
            out_specs=pl.BlockSpec((BM, d // 2), lambda i, e, u: (i, 0)),
            scratch_shapes=[pltpu.VMEM((d, de), BF16), pltpu.VMEM((d, de), BF16), pltpu.VMEM((de, d), BF16)],
        ),
        out_shape=jax.ShapeDtypeStruct((nb * BM, d // 2), U32),
        compiler_params=_cparams(("arbitrary",)),
        name="moe_experts",
    )(blk_e, n_used, xs, w1, w3, w2)


def _combine_kernel(dcur_ref, dnxt_ref, ys_ref, g_ref, hp_ref, xp_ref, xs_ref, gtp_ref, gts_ref, s1_ref, s3_ref,
                    s2_ref, xop_ref, xos_ref, buf_s, routed_s, sem, *, tm, npt):
    i = pl.program_id(0)
    slot = i % 2
    last = i + 1 >= pl.num_programs(0)
    half = buf_s.shape[3]

    def issue_row(dest_ref, sl, r):
        for k in range(TOP_K):
            d = dest_ref[r * TOP_K + k]
            pltpu.make_async_copy(ys_ref.at[pl.ds(d, 1), :], buf_s.at[sl, k, pl.ds(r, 1), :],
                                  sem.at[sl]).start(priority=k % 2)

    @pl.when(i == 0)
    def _():
        def first(r, c):
            issue_row(dcur_ref, 0, r)
            return c

        lax.fori_loop(0, tm, first, 0)

    pltpu.make_async_copy(ys_ref.at[pl.ds(0, tm * TOP_K), :], ys_ref.at[pl.ds(0, tm * TOP_K), :],
                          sem.at[slot]).wait()

    def sweep(cur, nxt):
        def group(gi, c):
            r0 = pl.multiple_of(gi * 8, 8)
            if nxt is not None:
                for u in range(8):
                    issue_row(dnxt_ref, nxt, r0 + u)
            g = g_ref[pl.ds(r0, 8), :]
            r_hi = r_lo = None
            for k in range(TOP_K):
                hi, lo = _unpack_rows(buf_s[cur, k, pl.ds(r0, 8), :])
                r_hi = g[:, k:k + 1] * hi if k == 0 else r_hi + g[:, k:k + 1] * hi
                r_lo = g[:, k:k + 1] * lo if k == 0 else r_lo + g[:, k:k + 1] * lo
            routed_s[pl.ds(r0, 8), 0:half] = r_hi
            routed_s[pl.ds(r0, 8), half:2 * half] = r_lo
            return c

        lax.fori_loop(0, tm // 8, group, 0)

    for cur in range(2):
        @pl.when(jnp.logical_and(jnp.logical_not(last), slot == cur))
        def _():
            sweep(cur, 1 - cur)

    @pl.when(last)
    def _():
        sweep(slot, None)

    ffn = routed_s[...] + _swiglu_packed(hp_ref[...], s1_ref, s3_ref, s2_ref)
    _by_kind(i, npt, lambda sample: _store_tok(
        sample, xop_ref, xos_ref, _pick(sample, xp_ref, xs_ref) + _pick(sample, gtp_ref, gts_ref) * ffn))


def _combine(dest_flat, ys, gates, hp, x, gt, s1, s3, s2, bp, s):
    n_p, d = x[0].shape
    n_s = x[1].shape[0]
    n = n_p + n_s
    ds_ = s1.shape[1]
    tm = TM_MOVE
    npt = n_p // tm
    nstep = n // tm
    xspecs = _tok_specs(tm, d, npt)
    kern = functools.partial(_combine_kernel, tm=tm, npt=npt)
    tok = lambda i: (i, 0)
    return pl.pallas_call(
        kern,
        grid=(nstep,),
        in_specs=[
            pl.BlockSpec((tm * TOP_K,), lambda i: (i,), memory_space=pltpu.SMEM),
            pl.BlockSpec((tm * TOP_K,), lambda i: (jnp.minimum(i + 1, nstep - 1),), memory_space=pltpu.SMEM),
            pl.BlockSpec(memory_space=pl.ANY),
            pl.BlockSpec((tm, TOP_K), tok),
            pl.BlockSpec((tm, d // 2), tok),
            *xspecs,
            *_mod_specs(tm, d, npt, s // tm, bp),
            _const_spec((d, ds_)),
            _const_spec((d, ds_)),
            _const_spec((ds_, d)),
        ],
        out_specs=xspecs,
        out_shape=[jax.ShapeDtypeStruct((n_p, d), F32), jax.ShapeDtypeStruct((n_s, d), F32)],
        scratch_shapes=[pltpu.VMEM((2, TOP_K, tm, d // 2), U32), pltpu.VMEM((tm, d), F32),
                        pltpu.SemaphoreType.DMA((2,))],
        compiler_params=_cparams(("arbitrary",)),
        name="moe_combine",
    )(dest_flat, dest_flat, ys, gates, hp, *x, *gt, s1, s3, s2)


def _moe(lg_t, hp, x, gt, rb, w1, w3, w2, layer, s1, s3, s2, tri, bp, s):
    n = hp.shape[0]
    idx_t, gate_t, rank_t, cnt = _route(lg_t, rb.reshape(N_EXPERTS, 1).astype(F32), tri)
    counts = cnt[:, 0].astype(I32)
    padded = (counts + BM - 1) // BM * BM
    pend = jnp.cumsum(padded)
    pstart = pend - padded
    nb = (n * TOP_K + N_EXPERTS * (BM - 1) + BM - 1) // BM
    n_used = pend[-1] // BM
    blk_e = jnp.minimum(jnp.sum(pend[None, :] <= (jnp.arange(nb, dtype=I32) * BM)[:, None], axis=1),
                        N_EXPERTS - 1).astype(I32)
    dest_t = jnp.sum(jnp.where(idx_t[None] == jnp.arange(N_EXPERTS, dtype=I32)[:, None, None],
                               pstart[:, None, None], 0), axis=0) + rank_t
    dest_flat = dest_t.T.reshape(-1)
    fill = jnp.concatenate([pstart + counts, pend[-1:]]).astype(I32)
    xs = _dispatch(fill, dest_flat, hp, nb * BM)
    ys = _experts(blk_e, n_used.reshape(1).astype(I32), xs, w1, w3, w2, layer)
    return _combine(dest_flat, ys, gate_t.T, hp, x, gt, s1, s3, s2, bp, s)


def _mlstm_pre_kernel(xp_ref, xs_ref, g_ref, shp_ref, shs_ref, scp_ref, scs_ref, wxo_ref, wgh_ref, wgl_ref, bif_ref, cw_ref, cb_ref,
                      wq_ref, wk_ref, wv_ref, halo_ref,
                      q_ref, k_ref, v_ref, o_ref, xcv_ref, gates_ref, tail_ref, xcs_ref,
                      xpad_s, *, npt, tps, t_s):
    i = pl.program_id(0)
    tm, d = xp_ref.shape

    def body(sample):
        h = (_rms(_pick(sample, xp_ref, xs_ref), g_ref[...]) * (1.0 + _pick(sample, scp_ref, scs_ref))
             + _pick(sample, shp_ref, shs_ref))
        h_hi, h_lo = _split2(h)
        y = _dot(h_hi, wxo_ref[...])
        xc = y[:, :d]
        o_ref[...] = y[:, d:].astype(BF16)
        wgh = wgh_ref[...]
        gp = _dot(h_hi, wgh) + _dot(h_hi, wgl_ref[...]) + _dot(h_lo, wgh) + bif_ref[...]
        lane = lax.broadcasted_iota(I32, gp.shape, 1)
        log_sig = jnp.minimum(gp, 0.0) - jnp.log(1.0 + jnp.exp(-jnp.abs(gp)))
        gates_ref[...] = jnp.where(lane >= C_HEADS, log_sig, gp)

        if sample:
            xpad_s[0:8, :] = jnp.zeros((8, d), F32)
        else:
            @pl.when(i % tps == 0)
            def _():
                xpad_s[0:8, :] = jnp.zeros((8, d), F32)

        xpad_s[8:, :] = xc
        acc = xc * cw_ref[C_CONV - 1:C_CONV, :] + cb_ref[...]
        for j in range(1, C_CONV):
            prev = xpad_s[8 - j:8 - j + tm, :]
            if sample:
                row = lax.broadcasted_iota(I32, (tm, 1), 0) & (t_s - 1)
                prev = jnp.where(row < j, halo_ref[j - 1], prev)
            acc = acc + prev * cw_ref[C_CONV - 1 - j:C_CONV - j, :]
        xconv = acc * jax.nn.sigmoid(acc)
        xcv16 = xconv.astype(BF16)
        xc16 = xc.astype(BF16)
        xcv_ref[...] = xcv16
        for hh in range(C_HEADS):
            cs = slice(hh * C_HEAD_DIM, (hh + 1) * C_HEAD_DIM)
            q_ref[:, cs] = _dot(xcv16[:, cs], wq_ref[hh]).astype(BF16)
            k_ref[:, cs] = (_dot(xcv16[:, cs], wk_ref[hh]) * C_HEAD_DIM ** -0.5).astype(BF16)
            v_ref[:, cs] = _dot(xc16[:, cs], wv_ref[hh]).astype(BF16)
        if sample:
            xcs_ref[...] = xc
        else:
            xpad_s[0:8, :] = xc[tm - 8:, :]
            tail_ref[...] = xc[tm - 8:, :]

    _by_kind(i, npt, body)


def _mlstm_pre(x, g, sh, sc, wxo, wgh, wgl, bif, cw, cb, wq, wk, wv, halo, bp, s, t_s):
    d = x[0].shape[1]
    n = x[0].shape[0] + x[1].shape[0]
    npt = bp * s // TM
    tps = s // TM
    nst = n // TM - npt
    tok = lambda i: (i, 0)
    mod = _mod_specs(TM, d, npt, tps, bp)
    kern = functools.partial(_mlstm_pre_kernel, npt=npt, tps=tps, t_s=t_s)
    b16 = jax.ShapeDtypeStruct((n, d), BF16)
    return pl.pallas_call(
        kern,
        grid=(n // TM,),
        in_specs=[
            *_tok_specs(TM, d, npt),
            _const_spec((1, d)),
            *mod,
            *mod,
            _const_spec((d, 2 * d)),
            _const_spec((d, LANES)),
            _const_spec((d, LANES)),
            _const_spec((1, LANES)),
            _const_spec((C_CONV, d)),
            _const_spec((1, d)),
            _const_spec((C_HEADS, C_HEAD_DIM, C_HEAD_DIM)),
            _const_spec((C_HEADS, C_HEAD_DIM, C_HEAD_DIM)),
            _const_spec((C_HEADS, C_HEAD_DIM, C_HEAD_DIM)),
            pl.BlockSpec((None, C_CONV - 1, TM, d), lambda i: (jnp.maximum(i - npt, 0), 0, 0, 0),
                         pipeline_mode=pl.Buffered(1)),
        ],
        out_specs=[pl.BlockSpec((TM, d), tok)] * 5 + [
            pl.BlockSpec((TM, LANES), tok),
            pl.BlockSpec((None, 8, d), lambda i: (jnp.minimum(i // tps, bp - 1), 0, 0)),
            pl.BlockSpec((TM, d), lambda i: (jnp.maximum(i - npt, 0), 0)),
        ],
        out_shape=[b16] * 5 + [
            jax.ShapeDtypeStruct((n, LANES), F32),
            jax.ShapeDtypeStruct((bp, 8, d), F32),
            jax.ShapeDtypeStruct((nst * TM, d), F32),
        ],
        scratch_shapes=[pltpu.VMEM((TM + 8, d), F32)],
        compiler_params=_cparams(("arbitrary",)),
        name="mlstm_pre",
    )(*x, g, *sh, *sc, wxo, wgh, wgl, bif, cw, cb, wq, wk, wv, halo)


def _mlstm_scan_kernel(q_ref, k_ref, v_ref, o_ref, xcv_ref, gc_ref, gr_ref, tri_ref, trit_ref, hg_ref, sk_ref,
                       mem0_ref, nrm0_ref, mx0_ref,
                       a_ref, memo_ref, nrmo_ref, mxo_ref,
                       mem_s, nrm_s, mx_s, *, nc):
    c = pl.program_id(1)
    ln = q_ref.shape[0]

    @pl.when(c == 0)
    def _():
        mem_s[...] = mem0_ref[...]
        nrm_s[...] = nrm0_ref[...]
        mx_s[...] = mx0_ref[...]

    gc = gc_ref[...]
    gr = gr_ref[...]
    tri = tri_ref[...]
    trit = trit_ref[...]
    bc = functools.reduce(lambda a, b: a + b, [_dot(tri, p) for p in _split3(gc)])
    br = functools.reduce(lambda a, b: a + b, [_dot(p, trit) for p in _split3(gr)])
    causal = lax.broadcasted_iota(I32, (ln, ln), 1) <= lax.broadcasted_iota(I32, (ln, ln), 0)
    for h in range(C_HEADS):
        cs = slice(h * C_HEAD_DIM, (h + 1) * C_HEAD_DIM)
        b_col = bc[:, C_HEADS + h:C_HEADS + h + 1]
        ig_col = gc[:, h:h + 1]
        b_row = br[C_HEADS + h:C_HEADS + h + 1, :]
        ig_row = gr[h:h + 1, :]
        b_last = b_row[:, ln - 1:ln]
        mx = mx_s[h:h + 1, 0:1]
        logw = jnp.where(causal, b_col - b_row + ig_row, NEG)
        g = b_col + mx
        m_t = jnp.maximum(g, jnp.max(logw, axis=1, keepdims=True))
        w = jnp.exp(logw - m_t)
        inter = jnp.exp(g - m_t)
        qh = q_ref[:, cs]
        kh = k_ref[:, cs]
        vh = v_ref[:, cs]
        a = w * _dot_nt(qh, kh)
        mem = mem_s[h]
        nrm = nrm_s[h:h + 1, :]
        num = _dot(a.astype(BF16), vh) + inter * _dot(qh, mem.astype(BF16))
        den = jnp.sum(a, axis=1, keepdims=True) + inter * jnp.sum(qh.astype(F32) * nrm, axis=1, keepdims=True)
        hout = num / jnp.maximum(jnp.abs(den), jnp.exp(-m_t))
        logs = b_last - b_col + ig_col
        m_new = jnp.maximum(b_last + mx, jnp.max(logs, axis=0, keepdims=True))
        decay = jnp.exp(b_last + mx - m_new)
        kw = kh.astype(F32) * jnp.exp(logs - m_new)
        mem_s[h] = decay * mem + _dot_tn(kw.astype(BF16), vh)
        nrm_s[h:h + 1, :] = decay * nrm + jnp.sum(kw, axis=0, keepdims=True)
        mx_s[h:h + 1, :] = jnp.broadcast_to(m_new, (1, mx_s.shape[1]))
        hh = hout * jax.nn.sigmoid(o_ref[:, cs].astype(F32))
        a_ref[:, cs] = (_rms(hh, hg_ref[:, cs]) + sk_ref[:, cs] * xcv_ref[:, cs].astype(F32)).astype(BF16)

    @pl.when(c == nc - 1)
    def _():
        memo_ref[...] = mem_s[...]
        nrmo_ref[...] = nrm_s[...]
        mxo_ref[...] = mx_s[...]


def _mlstm_scan(q, k, v, o, xcv, gates, hg, sk, mem0, nrm0, mx0, row0, nb, nc, ln):
    d = q.shape[1]
    nrow = nb * nc * ln
    gsl = lax.slice_in_dim(gates, row0 * ln, row0 * ln + nrow, axis=0)[:, :16]
    gr = jnp.transpose(gsl.reshape(nb * nc, ln, 16), (0, 2, 1))
    r = jnp.arange(ln)
    tri = (r[None, :] <= r[:, None]).astype(BF16)
    chunk = lambda b, c: (row0 + b * nc + c, 0)
    seq4 = lambda b, c: (b, 0, 0, 0)
    seq3 = lambda b, c: (b, 0, 0)
    kern = functools.partial(_mlstm_scan_kernel, nc=nc)
    return pl.pallas_call(
        kern,
        grid=(nb, nc),
        in_specs=[pl.BlockSpec((ln, d), chunk)] * 5 + [
            pl.BlockSpec((ln, LANES), chunk),
            pl.BlockSpec((None, 16, ln), lambda b, c: (b * nc + c, 0, 0)),
            _const_spec((ln, ln)),
            _const_spec((ln, ln)),
            _const_spec((1, d)),
            _const_spec((1, d)),
            pl.BlockSpec((None, C_HEADS, C_HEAD_DIM, C_HEAD_DIM), seq4),
            pl.BlockSpec((None, 8, C_HEAD_DIM), seq3),
            pl.BlockSpec((None, 8, LANES), seq3),
        ],
        out_specs=[
            pl.BlockSpec((ln, d), lambda b, c: (b * nc + c, 0)),
            pl.BlockSpec((None, C_HEADS, C_HEAD_DIM, C_HEAD_DIM), seq4),
            pl.BlockSpec((None, 8, C_HEAD_DIM), seq3),
            pl.BlockSpec((None, 8, LANES), seq3),
        ],
        out_shape=[
            jax.ShapeDtypeStruct((nrow, d), BF16),
            jax.ShapeDtypeStruct((nb, C_HEADS, C_HEAD_DIM, C_HEAD_DIM), F32),
            jax.ShapeDtypeStruct((nb, 8, C_HEAD_DIM), F32),
            jax.ShapeDtypeStruct((nb, 8, LANES), F32),
        ],
        scratch_shapes=[
            pltpu.VMEM((C_HEADS, C_HEAD_DIM, C_HEAD_DIM), F32),
            pltpu.VMEM((8, C_HEAD_DIM), F32),
            pltpu.VMEM((8, LANES), F32),
        ],
        compiler_params=_cparams(("parallel", "arbitrary")),
        name="mlstm_scan",
    )(q, k, v, o, xcv, gates, gr, tri, tri.T, hg, sk, mem0, nrm0, mx0)


def _pad_heads(a, width):
    nb = a.shape[0]
    if a.ndim == 2:
        a = jnp.broadcast_to(a[:, :, None], (nb, C_HEADS, width))
    return jnp.concatenate([a.astype(F32), jnp.zeros((nb, 8 - C_HEADS, width), F32)], axis=1)


def kernel(x_prompt, x_sample, c_prompt, c_sample, cache_a_k, cache_a_v, cache_b_k, cache_b_v, state_c_mem, state_c_norm, state_c_max, state_c_conv, norm_g, w_mod, b_mod, t5_bias, ab_w_in, ab_qk_g, ab_lambda, ab_head_g, ab_rel_bias, ab_w_out, c_w_in, c_b_if, c_conv_w, c_conv_b, c_w_qkv, c_head_g, c_skip, c_w_out, router_w, router_b, exp_w1, exp_w3, exp_w2, sh_w1, sh_w3, sh_w2):
    bp, s, d = x_prompt.shape
    bs, t = x_sample.shape[:2]
    depth = norm_g.shape[0]
    past = cache_a_k.shape[2]
    lb = cache_b_k.shape[2]
    n_p = bp * s
    n_s = bs * t
    n_all = n_p + n_s
    assert s % TM == 0 and n_s % TM == 0 and TM % t == 0 and t & (t - 1) == 0
    assert s % TQ_DIFF == 0 and s % ML_CHUNK == 0 and s >= BAND_PAST + TQ_BAND and TQ_DIFF >= T5_MAX_DIST
    assert past % CHUNK == 0 and lb == BAND_PAST and t <= CHUNK and t >= C_CONV - 1 and TM == BAND_PAST

    def per_token(vec):
        return vec[:bp].reshape(bp, 1, d), jnp.repeat(vec[bp:], t, axis=0)

    x = (x_prompt.reshape(n_p, d), x_sample.reshape(n_s, d))
    c_all = jnp.concatenate([c_prompt, c_sample], axis=0)
    mods = _modulation(c_all, w_mod, b_mod)

    r = jnp.arange(TM)
    tri_route = (r[:, None] < r[None, :]).astype(BF16)
    hd = jnp.arange(A_HEADS * 2 * HEAD_DIM) // HEAD_DIM
    bd = (hd[:, None] == hd[None, :]).astype(BF16)

    leaves = {}
    for l in range(depth):
        m6 = [mods[l][:, j * d:(j + 1) * d] for j in range(6)]
        sh1, sc1, gt1, sh2, sc2, gt2 = [per_token(v) for v in m6]
        i = l // 2
        if l % 2 == 0:
            lam_init = 0.8 - 0.6 * math.exp(-0.3 * l)
            lp = ab_lambda[i].astype(F32)
            lam = (jnp.exp(jnp.sum(lp[0] * lp[1])) - jnp.exp(jnp.sum(lp[2] * lp[3])) + lam_init).reshape(1)
            qkg_t = jnp.tile(ab_qk_g[i].astype(F32), (1, A_HEADS * 2))
            p16, ka_p, ka_s, va_p, va_s, kb_p, kb_s, vb_p, vb_s = _ab_in_proj(
                x, norm_g[l, 0].reshape(1, d), sh1, sc1, ab_w_in[i].astype(BF16), qkg_t, bd, bp, s)
            hg = ab_head_g[i].reshape(1, 2 * HEAD_DIM).astype(F32)
            out_scale = 1.0 - lam_init
            tiles, far = _diff_bias_prompt(t5_bias, TQ_DIFF)
            oa_p = _diff_attn_prompt(p16, tiles, far, lam, hg.reshape(2 * HEAD_DIM, 1), bp, s, out_scale)
            ob_p = _band_attn_prompt(p16, _band_bias_prompt(ab_rel_bias[i], TQ_BAND), bp, s)
            dbc, dbn = _diff_bias_sample(t5_bias, past, t)
            oa_s = _sample_attn(p16, cache_a_k[i].reshape(bs, past, -1), cache_a_v[i].reshape(bs, past, -1),
                                dbc, dbn, lam, hg, n_p, bs, t, True, out_scale)
            bbc, bbn = _band_bias_sample(ab_rel_bias[i], past, lb, t)
            ob_s = _sample_attn(p16, cache_b_k[i].reshape(bs, lb, -1), cache_b_v[i].reshape(bs, lb, -1),
                                bbc, bbn, lam, hg, n_p, bs, t, False, out_scale)
            mix_in = ((oa_p, oa_s.reshape(n_s, -1)), 0, (ob_p, ob_s.reshape(n_s, -1)), 0)
            w_out16 = ab_w_out[i].astype(BF16)
            leaves.setdefault('akp', []).append(ka_p.reshape(bp, s, A_HEADS, 2, HEAD_DIM))
            leaves.setdefault('avp', []).append(va_p.reshape(bp, s, A_HEADS, 2 * HEAD_DIM))
            leaves.setdefault('aks', []).append(ka_s.reshape(bs, t, A_HEADS, 2, HEAD_DIM))
            leaves.setdefault('avs', []).append(va_s.reshape(bs, t, A_HEADS, 2 * HEAD_DIM))
            leaves.setdefault('bkp', []).append(kb_p.reshape(bp, TM, B_HEADS, HEAD_DIM))
            leaves.setdefault('bvp', []).append(vb_p.reshape(bp, TM, B_HEADS, HEAD_DIM))
            leaves.setdefault('bks', []).append(kb_s.reshape(bs, t, B_HEADS, HEAD_DIM))
            leaves.setdefault('bvs', []).append(vb_s.reshape(bs, t, B_HEADS, HEAD_DIM))
        else:
            w_in = c_w_in[i]
            wg = jnp.pad(w_in[:, 2 * d:].astype(F32), ((0, 0), (0, LANES - 2 * C_HEADS)))
            wgh = wg.astype(BF16)
            wgl = (wg - wgh.astype(F32)).astype(BF16)
            bif = jnp.pad(c_b_if[i].astype(F32).reshape(1, 2 * C_HEADS), ((0, 0), (0, LANES - 2 * C_HEADS)))
            cprev = state_c_conv[i].astype(F32)
            planes = []
            for j in range(1, C_CONV):
                rows = jnp.concatenate([cprev[:, C_CONV - 1 - j:, :], jnp.zeros((bs, t - j, d), F32)], axis=1)
                planes.append(rows.reshape(n_s // TM, TM, d))
            halo = jnp.stack(planes, axis=1)
            wqkv = c_w_qkv[i].astype(BF16)
            q, k, v, o, xcv, gates, tail, xcs = _mlstm_pre(
                x, norm_g[l, 0].reshape(1, d), sh1, sc1, w_in[:, :2 * d].astype(BF16), wgh, wgl, bif,
                c_conv_w[i].astype(F32), c_conv_b[i].reshape(1, d).astype(F32), wqkv[0], wqkv[1], wqkv[2],
                halo, bp, s, t)
            hg = c_head_g[i].reshape(1, d).astype(F32)
            sk = c_skip[i].reshape(1, d).astype(F32)
            zm = jnp.zeros((bp, C_HEADS, C_HEAD_DIM, C_HEAD_DIM), F32)
            a_p, mem_p, nrm_p, mx_p = _mlstm_scan(
                q, k, v, o, xcv, gates, hg, sk, zm, jnp.zeros((bp, 8, C_HEAD_DIM), F32),
                jnp.zeros((bp, 8, LANES), F32), 0, bp, s // ML_CHUNK, ML_CHUNK)
            a_s, mem_s, nrm_s, mx_s = _mlstm_scan(
                q, k, v, o, xcv, gates, hg, sk, state_c_mem[i].astype(F32),
                _pad_heads(state_c_norm[i], C_HEAD_DIM), _pad_heads(state_c_max[i], LANES),
                n_p // t, bs, 1, t)
            mix_in = ((a_p, a_s), 0, (a_p, a_s), 1)
            w_out16 = c_w_out[i].astype(BF16)
            leaves.setdefault('memp', []).append(mem_p)
            leaves.setdefault('normp', []).append(nrm_p[:, :C_HEADS])
            leaves.setdefault('maxp', []).append(mx_p[:, :C_HEADS, 0])
            leaves.setdefault('convp', []).append(tail[:, 8 - (C_CONV - 1):])
            leaves.setdefault('mems', []).append(mem_s)
            leaves.setdefault('norms', []).append(nrm_s[:, :C_HEADS])
            leaves.setdefault('maxs', []).append(mx_s[:, :C_HEADS, 0])
            leaves.setdefault('convs', []).append(xcs.reshape(bs, t, d)[:, t - (C_CONV - 1):])
        rw_t = router_w[l].astype(F32).T
        rw_hi = rw_t.astype(BF16)
        rw_lo = (rw_t - rw_hi.astype(F32)).astype(BF16)
        xp, xs, hp, lg_t = _out_proj(mix_in[0], mix_in[1], mix_in[2], mix_in[3], w_out16, x, gt1,
                                     norm_g[l, 1].reshape(1, d), sh2, sc2, rw_hi, rw_lo, bp, s)
        x = _moe(lg_t, hp, (xp, xs), gt2, router_b[l], exp_w1, exp_w3, exp_w2, l,
                 sh_w1[l].astype(BF16), sh_w3[l].astype(BF16), sh_w2[l].astype(BF16), tri_route, bp, s)

    order = ['akp', 'avp', 'aks', 'avs', 'bkp', 'bvp', 'bks', 'bvs',
             'memp', 'normp', 'maxp', 'convp', 'mems', 'norms', 'maxs', 'convs']
    return (x[0].reshape(bp, s, d), x[1].reshape(bs, t, d)) + tuple(
        jnp.stack(leaves[name]) for name in order)
```

```python
import functools
import math

import jax
import jax.numpy as jnp
from jax import lax
from jax.experimental import pallas as pl
from jax.experimental.pallas import tpu as pltpu

F32 = jnp.float32
BF16 = jnp.bfloat16
I32 = jnp.int32
U32 = jnp.uint32

EPS = 1e-6
NEG = -1e30
CHUNK = 64
HEAD_DIM = 64
A_HEADS = 4
B_HEADS = 8
BAND_CHUNKS = 8
BAND_PAST = BAND_CHUNKS * CHUNK
REL_CLIP = 128
T5_BUCKETS = 32
T5_MAX_DIST = 128
C_HEADS = 4
C_HEAD_DIM = 256
C_CONV = 4
N_EXPERTS = 64
TOP_K = 8
ROUTE_SCALE = 2.5

LANES = 128
TM = 512
TQ_DIFF = 512
TQ_BAND = 128
ML_CHUNK = 256
BM = 512
TM_MOVE = 256
VMEM_LIMIT = 56 * 1024 * 1024


def _cparams(sem):
    return pltpu.CompilerParams(dimension_semantics=sem, vmem_limit_bytes=VMEM_LIMIT)


def _dot(a, b):
    return jnp.dot(a, b, preferred_element_type=F32)


def _dot_nt(a, b):
    return lax.dot_general(a, b, (((1,), (1,)), ((), ())), preferred_element_type=F32)


def _dot_tn(a, b):
    return lax.dot_general(a, b, (((0,), (0,)), ((), ())), preferred_element_type=F32)


def _split2(x):
    hi = x.astype(BF16)
    lo = (x - hi.astype(F32)).astype(BF16)
    return hi, lo


def _split3(x):
    p0 = x.astype(BF16)
    r1 = x - p0.astype(F32)
    p1 = r1.astype(BF16)
    p2 = (r1 - p1.astype(F32)).astype(BF16)
    return p0, p1, p2


def _pack_rows(x):
    half = x.shape[1] // 2
    bits = lax.bitcast_convert_type(x.astype(BF16).astype(F32), U32)
    return bits[:, :half] | (bits[:, half:] >> 16)


def _unpack_rows(p):
    return (lax.bitcast_convert_type(p & jnp.uint32(0xFFFF0000), F32),
            lax.bitcast_convert_type(p << 16, F32))


def _rms(x, g):
    return x * lax.rsqrt(jnp.mean(x * x, axis=-1, keepdims=True) + EPS) * g


def _mod_specs(tile, d, npt, tps, bp):
    return [pl.BlockSpec((None, 1, d), lambda i: (jnp.minimum(i // tps, bp - 1), 0, 0)),
            pl.BlockSpec((tile, d), lambda i: (jnp.maximum(i - npt, 0), 0), pipeline_mode=pl.Buffered(1))]


def _tok_specs(tile, width, npt, col=0):
    return [pl.BlockSpec((tile, width), lambda i: (jnp.minimum(i, npt - 1), col)),
            pl.BlockSpec((tile, width), lambda i: (jnp.maximum(i - npt, 0), col))]


def _by_kind(i, npt, body):
    @pl.when(i < npt)
    def _():
        body(False)

    @pl.when(i >= npt)
    def _():
        body(True)


def _pick(sample, p_ref, s_ref):
    return s_ref[...] if sample else p_ref[...]


def _store_tok(sample, p_ref, s_ref, val):
    (s_ref if sample else p_ref)[...] = val


def _const_spec(shape):
    nd = len(shape)
    return pl.BlockSpec(shape, lambda *_: (0,) * nd, pipeline_mode=pl.Buffered(1))


def _mod_kernel(c_ref, w_ref, b_ref, o_ref):
    c = c_ref[...]
    a_hi, a_lo = _split2(c * jax.nn.sigmoid(c))
    w_hi, w_lo = _split2(w_ref[...])
    o_ref[...] = _dot(a_hi, w_hi) + _dot(a_hi, w_lo) + _dot(a_lo, w_hi) + b_ref[...]


def _modulation(c_all, w_mod, b_mod):
    depth, d, n6 = w_mod.shape
    nseq = c_all.shape[0]
    tn = 512
    return pl.pallas_call(
        _mod_kernel,
        grid=(depth, n6 // tn),
        in_specs=[
            pl.BlockSpec((nseq, d), lambda l, j: (0, 0)),
            pl.BlockSpec((None, d, tn), lambda l, j: (l, 0, j)),
            pl.BlockSpec((None, 1, tn), lambda l, j: (l, 0, j)),
        ],
        out_specs=pl.BlockSpec((None, nseq, tn), lambda l, j: (l, 0, j)),
        out_shape=jax.ShapeDtypeStruct((depth, nseq, n6), F32),
        compiler_params=_cparams(("parallel", "parallel")),
        name="modulation",
    )(c_all, w_mod, b_mod.reshape(depth, 1, n6))


def _ab_in_kernel(xp_ref, xs_ref, g_ref, shp_ref, shs_ref, scp_ref, scs_ref, w_ref, qkg_ref, bd_ref,
                  p16_ref, kap_ref, kas_ref, vap_ref, vas_ref, kbp_ref, kbs_ref, vbp_ref, vbs_ref, *, npt, tps):
    i = pl.program_id(0)
    wa = A_HEADS * 2 * HEAD_DIM

    def group_norm(seg, gi):
        hi, lo = _split2(seg * seg)
        bd = bd_ref[...]
        ss = _dot(hi, bd) + _dot(lo, bd)
        return seg * lax.rsqrt(ss * (1.0 / HEAD_DIM) + EPS) * qkg_ref[gi:gi + 1, :]

    def body(sample):
        h = (_rms(_pick(sample, xp_ref, xs_ref), g_ref[...]) * (1.0 + _pick(sample, scp_ref, scs_ref))
             + _pick(sample, shp_ref, shs_ref))
        y = _dot(h.astype(BF16), w_ref[...])
        qa = group_norm(y[:, 0 * wa:1 * wa], 0)
        ka = group_norm(y[:, 1 * wa:2 * wa], 1)
        va = y[:, 2 * wa:3 * wa]
        qb = group_norm(y[:, 3 * wa:4 * wa], 2)
        kb = group_norm(y[:, 4 * wa:5 * wa], 3)
        vb = y[:, 5 * wa:6 * wa]
        scale = HEAD_DIM ** -0.5
        p16_ref[:, 0 * wa:1 * wa] = (qa * scale).astype(BF16)
        p16_ref[:, 1 * wa:2 * wa] = ka.astype(BF16)
        p16_ref[:, 2 * wa:3 * wa] = va.astype(BF16)
        p16_ref[:, 3 * wa:4 * wa] = (qb * scale).astype(BF16)
        p16_ref[:, 4 * wa:5 * wa] = kb.astype(BF16)
        p16_ref[:, 5 * wa:6 * wa] = vb.astype(BF16)
        _store_tok(sample, kap_ref, kas_ref, ka)
        _store_tok(sample, vap_ref, vas_ref, va)
        if sample:
            kbs_ref[...] = kb
            vbs_ref[...] = vb
        else:
            @pl.when(i % tps == tps - 1)
            def _():
                kbp_ref[...] = kb
                vbp_ref[...] = vb

    _by_kind(i, npt, body)


def _ab_in_proj(x, g, sh, sc, w16, qkg_t, bd, bp, s):
    n_p, d = x[0].shape
    n_s = x[1].shape[0]
    n = n_p + n_s
    wa = A_HEADS * 2 * HEAD_DIM
    n_in = w16.shape[1]
    tok = lambda i: (i, 0)
    npt = n_p // TM
    tps = s // TM
    mod = _mod_specs(TM, d, npt, tps, bp)
    leaf = _tok_specs(TM, wa, npt)
    band = [pl.BlockSpec((None, TM, wa), lambda i: (jnp.minimum(i // tps, bp - 1), 0, 0)), leaf[1]]
    f32 = lambda rows: jax.ShapeDtypeStruct((rows, wa), F32)
    band_shape = [jax.ShapeDtypeStruct((bp, TM, wa), F32), f32(n_s)]
    return pl.pallas_call(
        functools.partial(_ab_in_kernel, npt=npt, tps=tps),
        grid=(n // TM,),
        in_specs=[
            *_tok_specs(TM, d, npt),
            _const_spec((1, d)),
            *mod,
            *mod,
            _const_spec((d, n_in)),
            _const_spec((4, wa)),
            _const_spec((wa, wa)),
        ],
        out_specs=[pl.BlockSpec((TM, n_in), tok)] + leaf + leaf + band + band,
        out_shape=[jax.ShapeDtypeStruct((n, n_in), BF16), f32(n_p), f32(n_s), f32(n_p), f32(n_s)]
        + band_shape + band_shape,
        compiler_params=_cparams(("arbitrary",)),
        name="ab_in_proj",
    )(*x, g, *sh, *sc, w16, qkg_t, bd)


def _t5_bucket(rel):
    half = T5_BUCKETS // 2
    exact = half // 2
    n = jnp.abs(rel)
    large = exact + (jnp.log(jnp.maximum(n, 1).astype(F32) / exact)
                     / math.log(T5_MAX_DIST / exact) * (half - exact)).astype(I32)
    large = jnp.minimum(large, half - 1)
    return jnp.where(rel > 0, half, 0) + jnp.where(n < exact, n, large)


def _lookup(table, idx):
    onehot = (idx[..., None] == jnp.arange(table.shape[0], dtype=I32)).astype(F32)
    return jnp.einsum('...n,nh->...h', onehot, table.astype(F32), precision=lax.Precision.HIGHEST)


def _diff_bias_prompt(t5_bias, tq):
    i = jnp.arange(tq)[None, :]
    j = jnp.arange(tq)[:, None]
    diag = jnp.where(((j // CHUNK) <= (i // CHUNK))[..., None], _lookup(t5_bias, _t5_bucket(j - i)), NEG)
    prev = _lookup(t5_bias, _t5_bucket(j - i - tq))
    first = jnp.concatenate([diag, jnp.full_like(diag, NEG)], axis=0)
    later = jnp.concatenate([prev, diag], axis=0)
    tiles = jnp.transpose(jnp.stack([first, later]), (3, 0, 1, 2))
    tiles = jnp.concatenate([tiles, tiles], axis=3)
    far = _lookup(t5_bias, _t5_bucket(jnp.full((1,), -T5_MAX_DIST, I32)))[0]
    return tiles, far


def _diff_bias_sample(t5_bias, past, t):
    qpos = past + jnp.arange(t)
    kpos = jnp.arange(past + t)
    rel = kpos[None, :] - qpos[:, None]
    vis = (kpos[None, :] // CHUNK) <= (qpos[:, None] // CHUNK)
    b = jnp.where(vis[..., None], _lookup(t5_bias, _t5_bucket(rel)), NEG)
    b = jnp.transpose(b, (2, 0, 1))
    b = jnp.concatenate([b, b], axis=1)
    return b[:, :, :past], b[:, :, past:]


def _band_bias_prompt(rel_bias, tq):
    nvar = BAND_PAST // tq + 1
    win = BAND_PAST + tq
    u = jnp.arange(nvar)[:, None, None]
    i = jnp.arange(tq)[None, :, None]
    j = jnp.arange(win)[None, None, :]
    qp = u * tq + i
    qc = qp // CHUNK
    kc = j // CHUNK
    valid = (kc <= qc) & (kc >= qc - BAND_CHUNKS)
    b = _lookup(rel_bias, jnp.clip(j - qp, -REL_CLIP, REL_CLIP) + REL_CLIP)
    b = jnp.where(valid[..., None], b, NEG)
    b = jnp.transpose(b, (0, 3, 1, 2))
    return b.reshape(nvar, B_HEADS // 2, 2 * tq, win)


def _band_bias_sample(rel_bias, past, lb, t):
    qpos = past + jnp.arange(t)
    kpos = past - lb + jnp.arange(lb + t)
    band_lo = (past // CHUNK - BAND_CHUNKS) * CHUNK
    rel = jnp.clip(kpos[None, :] - qpos[:, None], -REL_CLIP, REL_CLIP) + REL_CLIP
    b = jnp.where((kpos >= band_lo)[None, :, None], _lookup(rel_bias, rel), NEG)
    b = jnp.transpose(b, (2, 0, 1)).reshape(B_HEADS // 2, 2 * t, lb + t)
    return b[:, :, :lb], b[:, :, lb:]


def _stack_halves(q):
    lane = lax.broadcasted_iota(I32, q.shape, 1)
    zero = jnp.zeros_like(q)
    return jnp.concatenate([jnp.where(lane < HEAD_DIM, q, zero), jnp.where(lane >= HEAD_DIM, q, zero)], axis=0)


def _diff_finish(o1, o2, lam, hg, out_scale):
    o = o1 - lam * o2
    return (_rms(o, hg) * out_scale).astype(BF16)


def _band_finish(o):
    tq = o.shape[0] // 2
    lane = lax.broadcasted_iota(I32, (tq, o.shape[1]), 1)
    return jnp.where(lane < HEAD_DIM, o[:tq], o[tq:]).astype(BF16)


def _diff_prompt_kernel(far_ref, lam_ref, q_ref, k_ref, v_ref, bias_ref, hg_ref, o_ref,
                        q2t_s, vt_s, m_s, l_s, acc_s, *, tq, out_scale):
    h = pl.program_id(1)
    i = pl.program_id(2)

    @pl.when(i == 0)
    def _():
        for jj in range(vt_s.shape[0]):
            vt_s[jj] = v_ref[jj * tq:(jj + 1) * tq, :].astype(F32).T.astype(BF16)

    qt = q_ref[...].astype(F32).T.astype(BF16)
    row = lax.broadcasted_iota(I32, qt.shape, 0)
    zero = jnp.zeros_like(qt)
    q2t_s[:, 0:tq] = jnp.where(row < HEAD_DIM, qt, zero)
    q2t_s[:, tq:2 * tq] = jnp.where(row >= HEAD_DIM, qt, zero)
    m_s[...] = jnp.full(m_s.shape, NEG, F32)
    l_s[...] = jnp.zeros(l_s.shape, F32)
    acc_s[...] = jnp.zeros(acc_s.shape, F32)

    def step(jb, nblk, bias, shift):
        kb = k_ref[pl.ds(pl.multiple_of(jb * tq, tq), nblk * tq), :]
        s = _dot(kb, q2t_s[...])
        if bias is not None:
            s = s + bias
        cmax = jnp.max(s, axis=0, keepdims=True)
        if shift is not None:
            cmax = cmax + shift
        m_prev = m_s[...]
        m_new = jnp.maximum(m_prev, cmax)
        alpha = jnp.exp(m_prev - m_new)
        p = jnp.exp(s - (m_new if shift is None else m_new - shift))
        l_s[...] = alpha * l_s[...] + jnp.sum(p, axis=0, keepdims=True)
        pb = p.astype(BF16)
        pv = _dot(vt_s[jb], pb[0:tq])
        for u in range(1, nblk):
            pv = pv + _dot(vt_s[jb + u], pb[u * tq:(u + 1) * tq])
        acc_s[...] = alpha * acc_s[...] + pv
        m_s[...] = m_new

    far = far_ref[h]
    nfar = jnp.maximum(i - 1, 0)

    def far_pair(j, carry):
        step(2 * j, 2, None, far)
        return carry

    lax.fori_loop(0, nfar // 2, far_pair, 0)

    @pl.when(nfar % 2 == 1)
    def _():
        step(nfar - 1, 1, None, far)

    step(nfar, 2, bias_ref[...], None)
    o = acc_s[...] / l_s[...]
    od = o[:, 0:tq] - lam_ref[0] * o[:, tq:2 * tq]
    on = od * lax.rsqrt(jnp.mean(od * od, axis=0, keepdims=True) + EPS) * hg_ref[...] * out_scale
    o_ref[...] = on.T.astype(BF16)


def _diff_attn_prompt(p16, tiles, far, lam, hg_col, bp, s, out_scale):
    tq = TQ_DIFF
    nq = s // tq
    wa = A_HEADS * LANES
    kern = functools.partial(_diff_prompt_kernel, tq=tq, out_scale=out_scale)
    return pl.pallas_call(
        kern,
        grid=(bp, A_HEADS, nq),
        in_specs=[
            pl.BlockSpec(memory_space=pltpu.SMEM),
            pl.BlockSpec(memory_space=pltpu.SMEM),
            pl.BlockSpec((tq, LANES), lambda b, h, i: (b * nq + i, h)),
            pl.BlockSpec((s, LANES), lambda b, h, i: (b, A_HEADS + h)),
            pl.BlockSpec((s, LANES), lambda b, h, i: (b, 2 * A_HEADS + h)),
            pl.BlockSpec((None, None, 2 * tq, 2 * tq), lambda b, h, i: (h, jnp.minimum(i, 1), 0, 0)),
            _const_spec((LANES, 1)),
        ],
        out_specs=pl.BlockSpec((tq, LANES), lambda b, h, i: (b * nq + i, h)),
        out_shape=jax.ShapeDtypeStruct((bp * s, wa), BF16),
        scratch_shapes=[
            pltpu.VMEM((LANES, 2 * tq), BF16),
            pltpu.VMEM((nq, LANES, tq), BF16),
            pltpu.VMEM((1, 2 * tq), F32),
            pltpu.VMEM((1, 2 * tq), F32),
            pltpu.VMEM((LANES, 2 * tq), F32),
        ],
        compiler_params=_cparams(("parallel", "parallel", "arbitrary")),
        name="diff_attn_prompt",
    )(far, lam, p16, p16, p16, tiles, hg_col)


def _band_prompt_kernel(q_ref, k_ref, v_ref, bias_ref, o_ref, *, tq, win):
    t = pl.program_id(2)
    start = pl.multiple_of(jnp.maximum(t * tq - BAND_PAST, 0), tq)
    for c in range(q_ref.shape[1] // LANES):
        cs = slice(c * LANES, (c + 1) * LANES)
        kb = k_ref[pl.ds(start, win), cs]
        vb = v_ref[pl.ds(start, win), cs]
        s = _dot_nt(_stack_halves(q_ref[:, cs]), kb) + bias_ref[c]
        m = jnp.max(s, axis=-1, keepdims=True)
        p = jnp.exp(s - m)
        l = jnp.sum(p, axis=-1, keepdims=True)
        o_ref[:, cs] = _band_finish(_dot(p.astype(BF16), vb) / l)


def _band_attn_prompt(p16, bias, bp, s):
    tq = TQ_BAND
    nq = s // tq
    win = BAND_PAST + tq
    nvar = bias.shape[0]
    npair = B_HEADS // 2
    pps = 4
    wide = pps * LANES
    c0 = 3 * A_HEADS // pps
    ng = npair // pps
    kern = functools.partial(_band_prompt_kernel, tq=tq, win=win)
    return pl.pallas_call(
        kern,
        grid=(bp, ng, nq),
        in_specs=[
            pl.BlockSpec((tq, wide), lambda b, p, t: (b * nq + t, c0 + p)),
            pl.BlockSpec((s, wide), lambda b, p, t: (b, c0 + ng + p)),
            pl.BlockSpec((s, wide), lambda b, p, t: (b, c0 + 2 * ng + p)),
            pl.BlockSpec((None, pps, 2 * tq, win), lambda b, p, t: (jnp.minimum(t, nvar - 1), p, 0, 0)),
        ],
        out_specs=pl.BlockSpec((tq, wide), lambda b, p, t: (b * nq + t, p)),
        out_shape=jax.ShapeDtypeStruct((bp * s, npair * LANES), BF16),
        compiler_params=_cparams(("parallel", "parallel", "parallel")),
        name="band_attn_prompt",
    )(p16, p16, p16, bias)


def _sample_attn_kernel(lam_ref, q_ref, kc_ref, vc_ref, kn_ref, vn_ref, bc_ref, bn_ref, hg_ref, o_ref,
                        *, diff, out_scale):
    t = q_ref.shape[0]
    q2 = _stack_halves(q_ref[...])
    sc = _dot_nt(q2, kc_ref[...].astype(BF16)) + bc_ref[...]
    sn = _dot_nt(q2, kn_ref[...]) + bn_ref[...]
    m = jnp.maximum(jnp.max(sc, axis=-1, keepdims=True), jnp.max(sn, axis=-1, keepdims=True))
    pc = jnp.exp(sc - m)
    pn = jnp.exp(sn - m)
    l = jnp.sum(pc, axis=-1, keepdims=True) + jnp.sum(pn, axis=-1, keepdims=True)
    o = (_dot(pc.astype(BF16), vc_ref[...].astype(BF16)) + _dot(pn.astype(BF16), vn_ref[...])) / l
    if diff:
        o_ref[...] = _diff_finish(o[:t], o[t:], lam_ref[0], hg_ref[...], out_scale)
    else:
        o_ref[...] = _band_finish(o)


def _sample_attn(p16, cache_k, cache_v, bias_c, bias_n, lam, hg, np_rows, bs, t, diff, out_scale):
    past = cache_k.shape[1]
    ncol = cache_k.shape[2] // LANES
    row0 = np_rows // t
    if diff:
        qc, kc, vc = 0, A_HEADS, 2 * A_HEADS
    else:
        qc, kc, vc = 3 * A_HEADS, 3 * A_HEADS + ncol, 3 * A_HEADS + 2 * ncol
    kern = functools.partial(_sample_attn_kernel, diff=diff, out_scale=out_scale)
    return pl.pallas_call(
        kern,
        grid=(bs, ncol),
        in_specs=[
            pl.BlockSpec(memory_space=pltpu.SMEM),
            pl.BlockSpec((t, LANES), lambda b, h: (row0 + b, qc + h)),
            pl.BlockSpec((None, past, LANES), lambda b, h: (b, 0, h)),
            pl.BlockSpec((None, past, LANES), lambda b, h: (b, 0, h)),
            pl.BlockSpec((t, LANES), lambda b, h: (row0 + b, kc + h)),
            pl.BlockSpec((t, LANES), lambda b, h: (row0 + b, vc + h)),
            pl.BlockSpec((None, 2 * t, past), lambda b, h: (h, 0, 0)),
            pl.BlockSpec((None, 2 * t, t), lambda b, h: (h, 0, 0)),
            _const_spec((1, LANES)),
        ],
        out_specs=pl.BlockSpec((None, t, LANES), lambda b, h: (b, 0, h)),
        out_shape=jax.ShapeDtypeStruct((bs, t, ncol * LANES), BF16),
        compiler_params=_cparams(("parallel", "parallel")),
        name="diff_attn_sample" if diff else "band_attn_sample",
    )(lam, p16, cache_k, cache_v, p16, p16, bias_c, bias_n, hg)


def _out_proj_kernel(a0p_ref, a0s_ref, a1p_ref, a1s_ref, w_ref, xp_ref, xs_ref, gtp_ref, gts_ref, g2_ref,
                     shp_ref, shs_ref, scp_ref, scs_ref, rwh_ref, rwl_ref, xop_ref, xos_ref, hp_ref, lg_ref,
                     *, npt):
    half = a0p_ref.shape[1]

    def body(sample):
        mix = (_dot(_pick(sample, a0p_ref, a0s_ref), w_ref[0:half, :])
               + _dot(_pick(sample, a1p_ref, a1s_ref), w_ref[half:2 * half, :]))
        x = _pick(sample, xp_ref, xs_ref) + _pick(sample, gtp_ref, gts_ref) * mix
        _store_tok(sample, xop_ref, xos_ref, x)
        h2 = _rms(x, g2_ref[...]) * (1.0 + _pick(sample, scp_ref, scs_ref)) + _pick(sample, shp_ref, shs_ref)
        hp_ref[...] = _pack_rows(h2)
        h_hi, h_lo = _split2(h2)
        rw_hi = rwh_ref[...]
        lg_ref[...] = _dot_nt(rw_hi, h_hi) + _dot_nt(rw_hi, h_lo) + _dot_nt(rwl_ref[...], h_hi)

    _by_kind(pl.program_id(0), npt, body)


def _out_proj(a0, c0, a1, c1, w16, x, gt, g2, sh, sc, rw_hi, rw_lo, bp, s):
    n_p, d = x[0].shape
    n_s = x[1].shape[0]
    n = n_p + n_s
    half = d // 2
    ne = rw_hi.shape[0]
    tok = lambda i: (i, 0)
    npt = n_p // TM
    mod = _mod_specs(TM, d, npt, s // TM, bp)
    xspecs = _tok_specs(TM, d, npt)
    return pl.pallas_call(
        functools.partial(_out_proj_kernel, npt=npt),
        grid=(n // TM,),
        in_specs=[
            *_tok_specs(TM, half, npt, c0),
            *_tok_specs(TM, half, npt, c1),
            _const_spec((d, d)),
            *xspecs,
            *mod,
            _const_spec((1, d)),
            *mod,
            *mod,
            _const_spec((ne, d)),
            _const_spec((ne, d)),
        ],
        out_specs=xspecs + [pl.BlockSpec((TM, half), tok), pl.BlockSpec((ne, TM), lambda i: (0, i))],
        out_shape=[jax.ShapeDtypeStruct((n_p, d), F32), jax.ShapeDtypeStruct((n_s, d), F32),
                   jax.ShapeDtypeStruct((n, half), U32), jax.ShapeDtypeStruct((ne, n), F32)],
        compiler_params=_cparams(("arbitrary",)),
        name="out_proj",
    )(*a0, *a1, w16, *x, *gt, g2, *sh, *sc, rw_hi, rw_lo)


def _route_kernel(lg_ref, rb_ref, tri_ref, idx_ref, gate_ref, rank_ref, cnt_ref, carry_s):
    @pl.when(pl.program_id(0) == 0)
    def _():
        carry_s[...] = jnp.zeros(carry_s.shape, F32)

    s = jax.nn.sigmoid(lg_ref[...])
    sb = s + rb_ref[...]
    row = lax.broadcasted_iota(I32, s.shape, 0).astype(F32)
    picks = []
    sel = jnp.zeros(s.shape, F32)
    for _ in range(TOP_K):
        m = jnp.max(sb, axis=0, keepdims=True)
        ik = jnp.min(jnp.where(sb == m, row, float(N_EXPERTS)), axis=0, keepdims=True)
        oh = row == ik
        picks.append((ik, oh, jnp.sum(jnp.where(oh, s, 0.0), axis=0, keepdims=True)))
        sel = sel + oh.astype(F32)
        sb = jnp.where(oh, -jnp.inf, sb)
    before = _dot(sel.astype(BF16), tri_ref[...]) + carry_s[...]
    gsum = functools.reduce(lambda a, b: a + b, [g for _, _, g in picks])
    for k, (ik, oh, g) in enumerate(picks):
        idx_ref[k:k + 1, :] = ik.astype(I32)
        gate_ref[k:k + 1, :] = g / gsum * ROUTE_SCALE
        rank_ref[k:k + 1, :] = jnp.sum(jnp.where(oh, before, 0.0), axis=0, keepdims=True).astype(I32)
    carry_s[...] = carry_s[...] + jnp.sum(sel, axis=1, keepdims=True)
    cnt_ref[...] = carry_s[...]


def _route(lg_t, rb, tri):
    ne, n = lg_t.shape
    tm = tri.shape[0]
    tokk = lambda i: (0, i)
    return pl.pallas_call(
        _route_kernel,
        grid=(n // tm,),
        in_specs=[pl.BlockSpec((ne, tm), tokk), _const_spec((ne, 1)), _const_spec((tm, tm))],
        out_specs=[pl.BlockSpec((TOP_K, tm), tokk)] * 3 + [_const_spec((ne, 1))],
        out_shape=[jax.ShapeDtypeStruct((TOP_K, n), I32), jax.ShapeDtypeStruct((TOP_K, n), F32),
                   jax.ShapeDtypeStruct((TOP_K, n), I32), jax.ShapeDtypeStruct((ne, 1), F32)],
        scratch_shapes=[pltpu.VMEM((ne, 1), F32)],
        compiler_params=_cparams(("arbitrary",)),
        name="moe_route",
    )(lg_t, rb, tri)


def _dispatch_kernel(pad_ref, dest_ref, h_ref, s1_ref, s3_ref, s2_ref, xs_ref, shared_ref, zero_s, sem, *, tm):
    i = pl.program_id(0)
    nrow = zero_s.shape[0]

    @pl.when(i == 0)
    def _():
        zero_s[...] = jnp.zeros(zero_s.shape, U32)

        def fill(start):
            cp = pltpu.make_async_copy(zero_s, xs_ref.at[pl.ds(pl.multiple_of(start, 8), nrow), :], sem)
            cp.start()
            cp.wait()

        def fill_pad(e, c):
            fill(pad_ref[e] // 8 * 8)
            return c

        lax.fori_loop(0, N_EXPERTS, fill_pad, 0)
        total = xs_ref.shape[0]
        tail = pad_ref[N_EXPERTS]

        def fill_tail(j, c):
            fill(jnp.minimum(tail + j * nrow, total - nrow))
            return c

        lax.fori_loop(0, (total - tail + nrow - 1) // nrow, fill_tail, 0)

    def issue_row(r, c):
        for k in range(TOP_K):
            d = dest_ref[r * TOP_K + k]
            pltpu.make_async_copy(h_ref.at[pl.ds(r, 1), :], xs_ref.at[pl.ds(d, 1), :], sem).start(priority=k % 2)
        return c

    lax.fori_loop(0, tm, issue_row, 0)
    shared_ref[...] = _swiglu_packed(h_ref[...], s1_ref, s3_ref, s2_ref)
    pltpu.make_async_copy(xs_ref.at[pl.ds(0, tm * TOP_K), :], xs_ref.at[pl.ds(0, tm * TOP_K), :], sem).wait()


def _dispatch(pad_start, dest_flat, hp, s1, s3, s2, cap):
    n, d = hp.shape
    dm, ds_ = s1.shape
    tm = TM_MOVE
    kern = functools.partial(_dispatch_kernel, tm=tm)
    return pl.pallas_call(
        kern,
        grid_spec=pltpu.PrefetchScalarGridSpec(
            num_scalar_prefetch=1,
            grid=(n // tm,),
            in_specs=[
                pl.BlockSpec((tm * TOP_K,), lambda i, ps: (i,), memory_space=pltpu.SMEM),
                pl.BlockSpec((tm, d), lambda i, ps: (i, 0)),
                _const_spec((dm, ds_)),
                _const_spec((dm, ds_)),
                _const_spec((ds_, dm)),
            ],
            out_specs=[pl.BlockSpec(memory_space=pl.ANY), pl.BlockSpec((tm, dm), lambda i, ps: (i, 0))],
            scratch_shapes=[pltpu.VMEM((BM + 8, d), U32), pltpu.SemaphoreType.DMA(())],
        ),
        out_shape=[jax.ShapeDtypeStruct((cap + BM + 8, d), U32), jax.ShapeDtypeStruct((n, dm), F32)],
        compiler_params=_cparams(("arbitrary",)),
        name="moe_dispatch",
    )(pad_start, dest_flat, hp, s1, s3, s2)


def _experts_kernel(exp_ref, nused_ref, x_ref, w1_ref, w3_ref, w2_ref, y_ref, w1_s, w3_s, w2_s):
    i = pl.program_id(0)
    used = i < nused_ref[0]

    @pl.when(jnp.logical_or(i == 0, exp_ref[i] != exp_ref[jnp.maximum(i - 1, 0)]))
    def _():
        w1_s[...] = w1_ref[...].astype(BF16)
        w3_s[...] = w3_ref[...].astype(BF16)
        w2_s[...] = w2_ref[...].astype(BF16)

    @pl.when(used)
    def _():
        y_ref[...] = _pack_rows(_swiglu_packed(x_ref[...], w1_s, w3_s, w2_s))

    @pl.when(jnp.logical_not(used))
    def _():
        y_ref[...] = jnp.zeros(y_ref.shape, U32)


def _swiglu_packed(xp, w1_ref, w3_ref, w2_ref):
    half = xp.shape[1]
    hi, lo = [v.astype(BF16) for v in _unpack_rows(xp)]
    a = _dot(hi, w1_ref[0:half, :]) + _dot(lo, w1_ref[half:2 * half, :])
    b = _dot(hi, w3_ref[0:half, :]) + _dot(lo, w3_ref[half:2 * half, :])
    return _dot((a * jax.nn.sigmoid(a) * b).astype(BF16), w2_ref[...])


def _experts(blk_e, n_used, xs, w1, w3, w2, layer):
    d = w1.shape[2]
    de = w1.shape[3]
    nb = blk_e.shape[0]
    return pl.pallas_call(
        _experts_kernel,
        grid_spec=pltpu.PrefetchScalarGridSpec(
            num_scalar_prefetch=2,
            grid=(nb,),
            in_specs=[
                pl.BlockSpec((BM, d // 2), lambda i, e, u: (i, 0)),
                pl.BlockSpec((None, None, d, de), lambda i, e, u: (layer, e[i], 0, 0)),
                pl.BlockSpec((None, None, d, de), lambda i, e, u: (layer, e[i], 0, 0)),
                pl.BlockSpec((None, None, de, d), lambda i, e, u: (layer, e[i], 0, 0)),
            ],
            out_specs=pl.BlockSpec((BM, d // 2), lambda i, e, u: (i, 0)),
            scratch_shapes=[pltpu.VMEM((d, de), BF16), pltpu.VMEM((d, de), BF16), pltpu.VMEM((de, d), BF16)],
        ),
        out_shape=jax.ShapeDtypeStruct((nb * BM, d // 2), U32),
        compiler_params=_cparams(("arbitrary",)),
        name="moe_experts",
    )(blk_e, n_used, xs, w1, w3, w2)


def _combine_kernel(dcur_ref, dnxt_ref, ys_ref, g_ref, shared_ref, xp_ref, xs_ref, gtp_ref, gts_ref,
                    xop_ref, xos_ref, buf_s, routed_s, sem, *, tm, npt):
    i = pl.program_id(0)
    slot = i % 2
    last = i + 1 >= pl.num_programs(0)
    half = buf_s.shape[3]

    def issue_row(dest_ref, sl, r):
        for k in range(TOP_K):
            d = dest_ref[r * TOP_K + k]
            pltpu.make_async_copy(ys_ref.at[pl.ds(d, 1), :], buf_s.at[sl, k, pl.ds(r, 1), :],
                                  sem.at[sl]).start(priority=k % 2)

    @pl.when(i == 0)
    def _():
        def first(r, c):
            issue_row(dcur_ref, 0, r)
            return c

        lax.fori_loop(0, tm, first, 0)

    pltpu.make_async_copy(ys_ref.at[pl.ds(0, tm * TOP_K), :], ys_ref.at[pl.ds(0, tm * TOP_K), :],
                          sem.at[slot]).wait()

    def sweep(cur, nxt):
        def group(gi, c):
            r0 = pl.multiple_of(gi * 8, 8)
            if nxt is not None:
                for u in range(8):
                    issue_row(dnxt_ref, nxt, r0 + u)
            g = g_ref[pl.ds(r0, 8), :]
            r_hi = r_lo = None
            for k in range(TOP_K):
                hi, lo = _unpack_rows(buf_s[cur, k, pl.ds(r0, 8), :])
                r_hi = g[:, k:k + 1] * hi if k == 0 else r_hi + g[:, k:k + 1] * hi
                r_lo = g[:, k:k + 1] * lo if k == 0 else r_lo + g[:, k:k + 1] * lo
            routed_s[pl.ds(r0, 8), 0:half] = r_hi
            routed_s[pl.ds(r0, 8), half:2 * half] = r_lo
            return c

        lax.fori_loop(0, tm // 8, group, 0)

    for cur in range(2):
        @pl.when(jnp.logical_and(jnp.logical_not(last), slot == cur))
        def _():
            sweep(cur, 1 - cur)

    @pl.when(last)
    def _():
        sweep(slot, None)

    ffn = routed_s[...] + shared_ref[...]
    _by_kind(i, npt, lambda sample: _store_tok(
        sample, xop_ref, xos_ref, _pick(sample, xp_ref, xs_ref) + _pick(sample, gtp_ref, gts_ref) * ffn))


def _combine(dest_flat, ys, gates, shared, x, gt, bp, s):
    n_p, d = x[0].shape
    n_s = x[1].shape[0]
    n = n_p + n_s
    tm = TM_MOVE
    npt = n_p // tm
    nstep = n // tm
    xspecs = _tok_specs(tm, d, npt)
    kern = functools.partial(_combine_kernel, tm=tm, npt=npt)
    tok = lambda i: (i, 0)
    return pl.pallas_call(
        kern,
        grid=(nstep,),
        in_specs=[
            pl.BlockSpec((tm * TOP_K,), lambda i: (i,), memory_space=pltpu.SMEM),
            pl.BlockSpec((tm * TOP_K,), lambda i: (jnp.minimum(i + 1, nstep - 1),), memory_space=pltpu.SMEM),
            pl.BlockSpec(memory_space=pl.ANY),
            pl.BlockSpec((tm, TOP_K), tok),
            pl.BlockSpec((tm, d), tok),
            *xspecs,
            *_mod_specs(tm, d, npt, s // tm, bp),
        ],
        out_specs=xspecs,
        out_shape=[jax.ShapeDtypeStruct((n_p, d), F32), jax.ShapeDtypeStruct((n_s, d), F32)],
        scratch_shapes=[pltpu.VMEM((2, TOP_K, tm, d // 2), U32), pltpu.VMEM((tm, d), F32),
                        pltpu.SemaphoreType.DMA((2,))],
        compiler_params=_cparams(("arbitrary",)),
        name="moe_combine",
    )(dest_flat, dest_flat, ys, gates, shared, *x, *gt)


def _moe(lg_t, hp, x, gt, rb, w1, w3, w2, layer, s1, s3, s2, tri, bp, s):
    n = hp.shape[0]
    idx_t, gate_t, rank_t, cnt = _route(lg_t, rb.reshape(N_EXPERTS, 1).astype(F32), tri)
    counts = cnt[:, 0].astype(I32)
    padded = (counts + BM - 1) // BM * BM
    pend = jnp.cumsum(padded)
    pstart = pend - padded
    nb = (n * TOP_K + N_EXPERTS * (BM - 1) + BM - 1) // BM
    n_used = pend[-1] // BM
    blk_e = jnp.minimum(jnp.sum(pend[None, :] <= (jnp.arange(nb, dtype=I32) * BM)[:, None], axis=1),
                        N_EXPERTS - 1).astype(I32)
    dest_t = jnp.sum(jnp.where(idx_t[None] == jnp.arange(N_EXPERTS, dtype=I32)[:, None, None],
                               pstart[:, None, None], 0), axis=0) + rank_t
    dest_flat = dest_t.T.reshape(-1)
    fill = jnp.concatenate([pstart + counts, pend[-1:]]).astype(I32)
    xs, shared = _dispatch(fill, dest_flat, hp, s1, s3, s2, nb * BM)
    ys = _experts(blk_e, n_used.reshape(1).astype(I32), xs, w1, w3, w2, layer)
    return _combine(dest_flat, ys, gate_t.T, shared, x, gt, bp, s)


def _mlstm_pre_kernel(xp_ref, xs_ref, g_ref, shp_ref, shs_ref, scp_ref, scs_ref, wxo_ref, wgh_ref, wgl_ref, bif_ref, cw_ref, cb_ref,
                      wq_ref, wk_ref, wv_ref, halo_ref,
                      q_ref, k_ref, v_ref, o_ref, xcv_ref, gates_ref, tail_ref, xcs_ref,
                      xpad_s, *, npt, tps, t_s):
    i = pl.program_id(0)
    tm, d = xp_ref.shape

    def body(sample):
        h = (_rms(_pick(sample, xp_ref, xs_ref), g_ref[...]) * (1.0 + _pick(sample, scp_ref, scs_ref))
             + _pick(sample, shp_ref, shs_ref))
        h_hi, h_lo = _split2(h)
        y = _dot(h_hi, wxo_ref[...])
        xc = y[:, :d]
        o_ref[...] = y[:, d:].astype(BF16)
        wgh = wgh_ref[...]
        gp = _dot(h_hi, wgh) + _dot(h_hi, wgl_ref[...]) + _dot(h_lo, wgh) + bif_ref[...]
        lane = lax.broadcasted_iota(I32, gp.shape, 1)
        log_sig = jnp.minimum(gp, 0.0) - jnp.log(1.0 + jnp.exp(-jnp.abs(gp)))
        gates_ref[...] = jnp.where(lane >= C_HEADS, log_sig, gp)

        if sample:
            xpad_s[0:8, :] = jnp.zeros((8, d), F32)
        else:
            @pl.when(i % tps == 0)
            def _():
                xpad_s[0:8, :] = jnp.zeros((8, d), F32)

        xpad_s[8:, :] = xc
        acc = xc * cw_ref[C_CONV - 1:C_CONV, :] + cb_ref[...]
        for j in range(1, C_CONV):
            prev = xpad_s[8 - j:8 - j + tm, :]
            if sample:
                row = lax.broadcasted_iota(I32, (tm, 1), 0) & (t_s - 1)
                prev = jnp.where(row < j, halo_ref[j - 1], prev)
            acc = acc + prev * cw_ref[C_CONV - 1 - j:C_CONV - j, :]
        xconv = acc * jax.nn.sigmoid(acc)
        xcv16 = xconv.astype(BF16)
        xc16 = xc.astype(BF16)
        xcv_ref[...] = xcv16
        for hh in range(C_HEADS):
            cs = slice(hh * C_HEAD_DIM, (hh + 1) * C_HEAD_DIM)
            q_ref[:, cs] = _dot(xcv16[:, cs], wq_ref[hh]).astype(BF16)
            k_ref[:, cs] = (_dot(xcv16[:, cs], wk_ref[hh]) * C_HEAD_DIM ** -0.5).astype(BF16)
            v_ref[:, cs] = _dot(xc16[:, cs], wv_ref[hh]).astype(BF16)
        if sample:
            xcs_ref[...] = xc
        else:
            xpad_s[0:8, :] = xc[tm - 8:, :]
            tail_ref[...] = xc[tm - 8:, :]

    _by_kind(i, npt, body)


def _mlstm_pre(x, g, sh, sc, wxo, wgh, wgl, bif, cw, cb, wq, wk, wv, halo, bp, s, t_s):
    d = x[0].shape[1]
    n = x[0].shape[0] + x[1].shape[0]
    npt = bp * s // TM
    tps = s // TM
    nst = n // TM - npt
    tok = lambda i: (i, 0)
    mod = _mod_specs(TM, d, npt, tps, bp)
    kern = functools.partial(_mlstm_pre_kernel, npt=npt, tps=tps, t_s=t_s)
    b16 = jax.ShapeDtypeStruct((n, d), BF16)
    return pl.pallas_call(
        kern,
        grid=(n // TM,),
        in_specs=[
            *_tok_specs(TM, d, npt),
            _const_spec((1, d)),
            *mod,
            *mod,
            _const_spec((d, 2 * d)),
            _const_spec((d, LANES)),
            _const_spec((d, LANES)),
            _const_spec((1, LANES)),
            _const_spec((C_CONV, d)),
            _const_spec((1, d)),
            _const_spec((C_HEADS, C_HEAD_DIM, C_HEAD_DIM)),
            _const_spec((C_HEADS, C_HEAD_DIM, C_HEAD_DIM)),
            _const_spec((C_HEADS, C_HEAD_DIM, C_HEAD_DIM)),
            pl.BlockSpec((None, C_CONV - 1, TM, d), lambda i: (jnp.maximum(i - npt, 0), 0, 0, 0),
                         pipeline_mode=pl.Buffered(1)),
        ],
        out_specs=[pl.BlockSpec((TM, d), tok)] * 5 + [
            pl.BlockSpec((TM, LANES), tok),
            pl.BlockSpec((None, 8, d), lambda i: (jnp.minimum(i // tps, bp - 1), 0, 0)),
            pl.BlockSpec((TM, d), lambda i: (jnp.maximum(i - npt, 0), 0)),
        ],
        out_shape=[b16] * 5 + [
            jax.ShapeDtypeStruct((n, LANES), F32),
            jax.ShapeDtypeStruct((bp, 8, d), F32),
            jax.ShapeDtypeStruct((nst * TM, d), F32),
        ],
        scratch_shapes=[pltpu.VMEM((TM + 8, d), F32)],
        compiler_params=_cparams(("arbitrary",)),
        name="mlstm_pre",
    )(*x, g, *sh, *sc, wxo, wgh, wgl, bif, cw, cb, wq, wk, wv, halo)


def _mlstm_scan_kernel(q_ref, k_ref, v_ref, o_ref, xcv_ref, gc_ref, gr_ref, tri_ref, trit_ref, hg_ref, sk_ref,
                       mem0_ref, nrm0_ref, mx0_ref,
                       a_ref, memo_ref, nrmo_ref, mxo_ref,
                       mem_s, nrm_s, mx_s, *, nc):
    c = pl.program_id(1)
    ln = q_ref.shape[0]

    @pl.when(c == 0)
    def _():
        mem_s[...] = mem0_ref[...]
        nrm_s[...] = nrm0_ref[...]
        mx_s[...] = mx0_ref[...]

    gc = gc_ref[...]
    gr = gr_ref[...]
    tri = tri_ref[...]
    trit = trit_ref[...]
    bc = functools.reduce(lambda a, b: a + b, [_dot(tri, p) for p in _split3(gc)])
    br = functools.reduce(lambda a, b: a + b, [_dot(p, trit) for p in _split3(gr)])
    causal = lax.broadcasted_iota(I32, (ln, ln), 1) <= lax.broadcasted_iota(I32, (ln, ln), 0)
    for h in range(C_HEADS):
        cs = slice(h * C_HEAD_DIM, (h + 1) * C_HEAD_DIM)
        b_col = bc[:, C_HEADS + h:C_HEADS + h + 1]
        ig_col = gc[:, h:h + 1]
        b_row = br[C_HEADS + h:C_HEADS + h + 1, :]
        ig_row = gr[h:h + 1, :]
        b_last = b_row[:, ln - 1:ln]
        mx = mx_s[h:h + 1, 0:1]
        logw = jnp.where(causal, b_col - b_row + ig_row, NEG)
        g = b_col + mx
        m_t = jnp.maximum(g, jnp.max(logw, axis=1, keepdims=True))
        w = jnp.exp(logw - m_t)
        inter = jnp.exp(g - m_t)
        qh = q_ref[:, cs]
        kh = k_ref[:, cs]
        vh = v_ref[:, cs]
        a = w * _dot_nt(qh, kh)
        mem = mem_s[h]
        nrm = nrm_s[h:h + 1, :]
        num = _dot(a.astype(BF16), vh) + inter * _dot(qh, mem.astype(BF16))
        den = jnp.sum(a, axis=1, keepdims=True) + inter * jnp.sum(qh.astype(F32) * nrm, axis=1, keepdims=True)
        hout = num / jnp.maximum(jnp.abs(den), jnp.exp(-m_t))
        logs = b_last - b_col + ig_col
        m_new = jnp.maximum(b_last + mx, jnp.max(logs, axis=0, keepdims=True))
        decay = jnp.exp(b_last + mx - m_new)
        kw = kh.astype(F32) * jnp.exp(logs - m_new)
        mem_s[h] = decay * mem + _dot_tn(kw.astype(BF16), vh)
        nrm_s[h:h + 1, :] = decay * nrm + jnp.sum(kw, axis=0, keepdims=True)
        mx_s[h:h + 1, :] = jnp.broadcast_to(m_new, (1, mx_s.shape[1]))
        hh = hout * jax.nn.sigmoid(o_ref[:, cs].astype(F32))
        a_ref[:, cs] = (_rms(hh, hg_ref[:, cs]) + sk_ref[:, cs] * xcv_ref[:, cs].astype(F32)).astype(BF16)

    @pl.when(c == nc - 1)
    def _():
        memo_ref[...] = mem_s[...]
        nrmo_ref[...] = nrm_s[...]
        mxo_ref[...] = mx_s[...]


def _mlstm_scan(q, k, v, o, xcv, gates, hg, sk, mem0, nrm0, mx0, row0, nb, nc, ln):
    d = q.shape[1]
    nrow = nb * nc * ln
    gsl = lax.slice_in_dim(gates, row0 * ln, row0 * ln + nrow, axis=0)[:, :16]
    gr = jnp.transpose(gsl.reshape(nb * nc, ln, 16), (0, 2, 1))
    r = jnp.arange(ln)
    tri = (r[None, :] <= r[:, None]).astype(BF16)
    chunk = lambda b, c: (row0 + b * nc + c, 0)
    seq4 = lambda b, c: (b, 0, 0, 0)
    seq3 = lambda b, c: (b, 0, 0)
    kern = functools.partial(_mlstm_scan_kernel, nc=nc)
    return pl.pallas_call(
        kern,
        grid=(nb, nc),
        in_specs=[pl.BlockSpec((ln, d), chunk)] * 5 + [
            pl.BlockSpec((ln, LANES), chunk),
            pl.BlockSpec((None, 16, ln), lambda b, c: (b * nc + c, 0, 0)),
            _const_spec((ln, ln)),
            _const_spec((ln, ln)),
            _const_spec((1, d)),
            _const_spec((1, d)),
            pl.BlockSpec((None, C_HEADS, C_HEAD_DIM, C_HEAD_DIM), seq4),
            pl.BlockSpec((None, 8, C_HEAD_DIM), seq3),
            pl.BlockSpec((None, 8, LANES), seq3),
        ],
        out_specs=[
            pl.BlockSpec((ln, d), lambda b, c: (b * nc + c, 0)),
            pl.BlockSpec((None, C_HEADS, C_HEAD_DIM, C_HEAD_DIM), seq4),
            pl.BlockSpec((None, 8, C_HEAD_DIM), seq3),
            pl.BlockSpec((None, 8, LANES), seq3),
        ],
        out_shape=[
            jax.ShapeDtypeStruct((nrow, d), BF16),
            jax.ShapeDtypeStruct((nb, C_HEADS, C_HEAD_DIM, C_HEAD_DIM), F32),
            jax.ShapeDtypeStruct((nb, 8, C_HEAD_DIM), F32),
            jax.ShapeDtypeStruct((nb, 8, LANES), F32),
        ],
        scratch_shapes=[
            pltpu.VMEM((C_HEADS, C_HEAD_DIM, C_HEAD_DIM), F32),
            pltpu.VMEM((8, C_HEAD_DIM), F32),
            pltpu.VMEM((8, LANES), F32),
        ],
        compiler_params=_cparams(("parallel", "arbitrary")),
        name="mlstm_scan",
    )(q, k, v, o, xcv, gates, gr, tri, tri.T, hg, sk, mem0, nrm0, mx0)


def _pad_heads(a, width):
    nb = a.shape[0]
    if a.ndim == 2:
        a = jnp.broadcast_to(a[:, :, None], (nb, C_HEADS, width))
    return jnp.concatenate([a.astype(F32), jnp.zeros((nb, 8 - C_HEADS, width), F32)], axis=1)


def kernel(x_prompt, x_sample, c_prompt, c_sample, cache_a_k, cache_a_v, cache_b_k, cache_b_v, state_c_mem, state_c_norm, state_c_max, state_c_conv, norm_g, w_mod, b_mod, t5_bias, ab_w_in, ab_qk_g, ab_lambda, ab_head_g, ab_rel_bias, ab_w_out, c_w_in, c_b_if, c_conv_w, c_conv_b, c_w_qkv, c_head_g, c_skip, c_w_out, router_w, router_b, exp_w1, exp_w3, exp_w2, sh_w1, sh_w3, sh_w2):
    bp, s, d = x_prompt.shape
    bs, t = x_sample.shape[:2]
    depth = norm_g.shape[0]
    past = cache_a_k.shape[2]
    lb = cache_b_k.shape[2]
    n_p = bp * s
    n_s = bs * t
    n_all = n_p + n_s
    assert s % TM == 0 and n_s % TM == 0 and TM % t == 0 and t & (t - 1) == 0
    assert s % TQ_DIFF == 0 and s % ML_CHUNK == 0 and s >= BAND_PAST + TQ_BAND and TQ_DIFF >= T5_MAX_DIST
    assert past % CHUNK == 0 and lb == BAND_PAST and t <= CHUNK and t >= C_CONV - 1 and TM == BAND_PAST

    def per_token(vec):
        return vec[:bp].reshape(bp, 1, d), jnp.repeat(vec[bp:], t, axis=0)

    x = (x_prompt.reshape(n_p, d), x_sample.reshape(n_s, d))
    c_all = jnp.concatenate([c_prompt, c_sample], axis=0)
    mods = _modulation(c_all, w_mod, b_mod)

    r = jnp.arange(TM)
    tri_route = (r[:, None] < r[None, :]).astype(BF16)
    hd = jnp.arange(A_HEADS * 2 * HEAD_DIM) // HEAD_DIM
    bd = (hd[:, None] == hd[None, :]).astype(BF16)

    leaves = {}
    for l in range(depth):
        m6 = [mods[l][:, j * d:(j + 1) * d] for j in range(6)]
        sh1, sc1, gt1, sh2, sc2, gt2 = [per_token(v) for v in m6]
        i = l // 2
        if l % 2 == 0:
            lam_init = 0.8 - 0.6 * math.exp(-0.3 * l)
            lp = ab_lambda[i].astype(F32)
            lam = (jnp.exp(jnp.sum(lp[0] * lp[1])) - jnp.exp(jnp.sum(lp[2] * lp[3])) + lam_init).reshape(1)
            qkg_t = jnp.tile(ab_qk_g[i].astype(F32), (1, A_HEADS * 2))
            p16, ka_p, ka_s, va_p, va_s, kb_p, kb_s, vb_p, vb_s = _ab_in_proj(
                x, norm_g[l, 0].reshape(1, d), sh1, sc1, ab_w_in[i].astype(BF16), qkg_t, bd, bp, s)
            hg = ab_head_g[i].reshape(1, 2 * HEAD_DIM).astype(F32)
            out_scale = 1.0 - lam_init
            tiles, far = _diff_bias_prompt(t5_bias, TQ_DIFF)
            oa_p = _diff_attn_prompt(p16, tiles, far, lam, hg.reshape(2 * HEAD_DIM, 1), bp, s, out_scale)
            ob_p = _band_attn_prompt(p16, _band_bias_prompt(ab_rel_bias[i], TQ_BAND), bp, s)
            dbc, dbn = _diff_bias_sample(t5_bias, past, t)
            oa_s = _sample_attn(p16, cache_a_k[i].reshape(bs, past, -1), cache_a_v[i].reshape(bs, past, -1),
                                dbc, dbn, lam, hg, n_p, bs, t, True, out_scale)
            bbc, bbn = _band_bias_sample(ab_rel_bias[i], past, lb, t)
            ob_s = _sample_attn(p16, cache_b_k[i].reshape(bs, lb, -1), cache_b_v[i].reshape(bs, lb, -1),
                                bbc, bbn, lam, hg, n_p, bs, t, False, out_scale)
            mix_in = ((oa_p, oa_s.reshape(n_s, -1)), 0, (ob_p, ob_s.reshape(n_s, -1)), 0)
            w_out16 = ab_w_out[i].astype(BF16)
            leaves.setdefault('akp', []).append(ka_p.reshape(bp, s, A_HEADS, 2, HEAD_DIM))
            leaves.setdefault('avp', []).append(va_p.reshape(bp, s, A_HEADS, 2 * HEAD_DIM))
            leaves.setdefault('aks', []).append(ka_s.reshape(bs, t, A_HEADS, 2, HEAD_DIM))
            leaves.setdefault('avs', []).append(va_s.reshape(bs, t, A_HEADS, 2 * HEAD_DIM))
            leaves.setdefault('bkp', []).append(kb_p.reshape(bp, TM, B_HEADS, HEAD_DIM))
            leaves.setdefault('bvp', []).append(vb_p.reshape(bp, TM, B_HEADS, HEAD_DIM))
            leaves.setdefault('bks', []).append(kb_s.reshape(bs, t, B_HEADS, HEAD_DIM))
            leaves.setdefault('bvs', []).append(vb_s.reshape(bs, t, B_HEADS, HEAD_DIM))
        else:
            w_in = c_w_in[i]
            wg = jnp.pad(w_in[:, 2 * d:].astype(F32), ((0, 0), (0, LANES - 2 * C_HEADS)))
            wgh = wg.astype(BF16)
            wgl = (wg - wgh.astype(F32)).astype(BF16)
            bif = jnp.pad(c_b_if[i].astype(F32).reshape(1, 2 * C_HEADS), ((0, 0), (0, LANES - 2 * C_HEADS)))
            cprev = state_c_conv[i].astype(F32)
            planes = []
            for j in range(1, C_CONV):
                rows = jnp.concatenate([cprev[:, C_CONV - 1 - j:, :], jnp.zeros((bs, t - j, d), F32)], axis=1)
                planes.append(rows.reshape(n_s // TM, TM, d))
            halo = jnp.stack(planes, axis=1)
            wqkv = c_w_qkv[i].astype(BF16)
            q, k, v, o, xcv, gates, tail, xcs = _mlstm_pre(
                x, norm_g[l, 0].reshape(1, d), sh1, sc1, w_in[:, :2 * d].astype(BF16), wgh, wgl, bif,
                c_conv_w[i].astype(F32), c_conv_b[i].reshape(1, d).astype(F32), wqkv[0], wqkv[1], wqkv[2],
                halo, bp, s, t)
            hg = c_head_g[i].reshape(1, d).astype(F32)
            sk = c_skip[i].reshape(1, d).astype(F32)
            zm = jnp.zeros((bp, C_HEADS, C_HEAD_DIM, C_HEAD_DIM), F32)
            a_p, mem_p, nrm_p, mx_p = _mlstm_scan(
                q, k, v, o, xcv, gates, hg, sk, zm, jnp.zeros((bp, 8, C_HEAD_DIM), F32),
                jnp.zeros((bp, 8, LANES), F32), 0, bp, s // ML_CHUNK, ML_CHUNK)
            a_s, mem_s, nrm_s, mx_s = _mlstm_scan(
                q, k, v, o, xcv, gates, hg, sk, state_c_mem[i].astype(F32),
                _pad_heads(state_c_norm[i], C_HEAD_DIM), _pad_heads(state_c_max[i], LANES),
                n_p // t, bs, 1, t)
            mix_in = ((a_p, a_s), 0, (a_p, a_s), 1)
            w_out16 = c_w_out[i].astype(BF16)
            leaves.setdefault('memp', []).append(mem_p)
            leaves.setdefault('normp', []).append(nrm_p[:, :C_HEADS])
            leaves.setdefault('maxp', []).append(mx_p[:, :C_HEADS, 0])
            leaves.setdefault('convp', []).append(tail[:, 8 - (C_CONV - 1):])
            leaves.setdefault('mems', []).append(mem_s)
            leaves.setdefault('norms', []).append(nrm_s[:, :C_HEADS])
            leaves.setdefault('maxs', []).append(mx_s[:, :C_HEADS, 0])
            leaves.setdefault('convs', []).append(xcs.reshape(bs, t, d)[:, t - (C_CONV - 1):])
        rw_t = router_w[l].astype(F32).T
        rw_hi = rw_t.astype(BF16)
        rw_lo = (rw_t - rw_hi.astype(F32)).astype(BF16)
        xp, xs, hp, lg_t = _out_proj(mix_in[0], mix_in[1], mix_in[2], mix_in[3], w_out16, x, gt1,
                                     norm_g[l, 1].reshape(1, d), sh2, sc2, rw_hi, rw_lo, bp, s)
        x = _moe(lg_t, hp, (xp, xs), gt2, router_b[l], exp_w1, exp_w3, exp_w2, l,
                 sh_w1[l].astype(BF16), sh_w3[l].astype(BF16), sh_w2[l].astype(BF16), tri_route, bp, s)

    order = ['akp', 'avp', 'aks', 'avs', 'bkp', 'bvp', 'bks', 'bvs',
             'memp', 'normp', 'maxp', 'convp', 'mems', 'norms', 'maxs', 'convs']
    return (x[0].reshape(bp, s, d), x[1].reshape(bs, t, d)) + tuple(
        jnp.stack(leaves[name]) for name in order)
```

```python
import functools
import math

import jax
import jax.numpy as jnp
from jax import lax
from jax.experimental import pallas as pl
from jax.experimental.pallas import tpu as pltpu

F32 = jnp.float32
BF16 = jnp.bfloat16
I32 = jnp.int32
U32 = jnp.uint32

EPS = 1e-6
NEG = -1e30
CHUNK = 64
HEAD_DIM = 64
A_HEADS = 4
B_HEADS = 8
BAND_CHUNKS = 8
BAND_PAST = BAND_CHUNKS * CHUNK
REL_CLIP = 128
T5_BUCKETS = 32
T5_MAX_DIST = 128
C_HEADS = 4
C_HEAD_DIM = 256
C_CONV = 4
N_EXPERTS = 64
TOP_K = 8
ROUTE_SCALE = 2.5

LANES = 128
TM = 512
TQ_DIFF = 512
TQ_BAND = 128
ML_CHUNK = 256
BM = 512
TM_MOVE = 256
VMEM_LIMIT = 56 * 1024 * 1024


def _cparams(sem):
    return pltpu.CompilerParams(dimension_semantics=sem, vmem_limit_bytes=VMEM_LIMIT)


def _dot(a, b):
    return jnp.dot(a, b, preferred_element_type=F32)


def _dot_nt(a, b):
    return lax.dot_general(a, b, (((1,), (1,)), ((), ())), preferred_element_type=F32)


def _dot_tn(a, b):
    return lax.dot_general(a, b, (((0,), (0,)), ((), ())), preferred_element_type=F32)


def _split2(x):
    hi = x.astype(BF16)
    lo = (x - hi.astype(F32)).astype(BF16)
    return hi, lo


def _split3(x):
    p0 = x.astype(BF16)
    r1 = x - p0.astype(F32)
    p1 = r1.astype(BF16)
    p2 = (r1 - p1.astype(F32)).astype(BF16)
    return p0, p1, p2


def _pack_rows(x):
    half = x.shape[1] // 2
    bits = lax.bitcast_convert_type(x.astype(BF16).astype(F32), U32)
    return bits[:, :half] | (bits[:, half:] >> 16)


def _unpack_rows(p):
    return (lax.bitcast_convert_type(p & jnp.uint32(0xFFFF0000), F32),
            lax.bitcast_convert_type(p << 16, F32))


def _rms(x, g):
    return x * lax.rsqrt(jnp.mean(x * x, axis=-1, keepdims=True) + EPS) * g


def _mod_specs(tile, d, npt, tps, bp):
    return [pl.BlockSpec((None, 1, d), lambda i: (jnp.minimum(i // tps, bp - 1), 0, 0)),
            pl.BlockSpec((tile, d), lambda i: (jnp.maximum(i - npt, 0), 0), pipeline_mode=pl.Buffered(1))]


def _tok_specs(tile, width, npt, col=0):
    return [pl.BlockSpec((tile, width), lambda i: (jnp.minimum(i, npt - 1), col)),
            pl.BlockSpec((tile, width), lambda i: (jnp.maximum(i - npt, 0), col))]


def _by_kind(i, npt, body):
    @pl.when(i < npt)
    def _():
        body(False)

    @pl.when(i >= npt)
    def _():
        body(True)


def _pick(sample, p_ref, s_ref):
    return s_ref[...] if sample else p_ref[...]


def _store_tok(sample, p_ref, s_ref, val):
    (s_ref if sample else p_ref)[...] = val


def _const_spec(shape):
    nd = len(shape)
    return pl.BlockSpec(shape, lambda *_: (0,) * nd, pipeline_mode=pl.Buffered(1))


def _mod_kernel(c_ref, w_ref, b_ref, o_ref):
    c = c_ref[...]
    a_hi, a_lo = _split2(c * jax.nn.sigmoid(c))
    w_hi, w_lo = _split2(w_ref[...])
    o_ref[...] = _dot(a_hi, w_hi) + _dot(a_hi, w_lo) + _dot(a_lo, w_hi) + b_ref[...]


def _modulation(c_all, w_mod, b_mod):
    depth, d, n6 = w_mod.shape
    nseq = c_all.shape[0]
    tn = 512
    return pl.pallas_call(
        _mod_kernel,
        grid=(depth, n6 // tn),
        in_specs=[
            pl.BlockSpec((nseq, d), lambda l, j: (0, 0)),
            pl.BlockSpec((None, d, tn), lambda l, j: (l, 0, j)),
            pl.BlockSpec((None, 1, tn), lambda l, j: (l, 0, j)),
        ],
        out_specs=pl.BlockSpec((None, nseq, tn), lambda l, j: (l, 0, j)),
        out_shape=jax.ShapeDtypeStruct((depth, nseq, n6), F32),
        compiler_params=_cparams(("parallel", "parallel")),
        name="modulation",
    )(c_all, w_mod, b_mod.reshape(depth, 1, n6))


def _ab_in_kernel(xp_ref, xs_ref, g_ref, shp_ref, shs_ref, scp_ref, scs_ref, w_ref, qkg_ref, bd_ref,
                  p16_ref, kap_ref, kas_ref, vap_ref, vas_ref, kbp_ref, kbs_ref, vbp_ref, vbs_ref, *, npt, tps):
    i = pl.program_id(0)
    wa = A_HEADS * 2 * HEAD_DIM

    def group_norm(seg, gi):
        hi, lo = _split2(seg * seg)
        bd = bd_ref[...]
        ss = _dot(hi, bd) + _dot(lo, bd)
        return seg * lax.rsqrt(ss * (1.0 / HEAD_DIM) + EPS) * qkg_ref[gi:gi + 1, :]

    def body(sample):
        h = (_rms(_pick(sample, xp_ref, xs_ref), g_ref[...]) * (1.0 + _pick(sample, scp_ref, scs_ref))
             + _pick(sample, shp_ref, shs_ref))
        y = _dot(h.astype(BF16), w_ref[...])
        qa = group_norm(y[:, 0 * wa:1 * wa], 0)
        ka = group_norm(y[:, 1 * wa:2 * wa], 1)
        va = y[:, 2 * wa:3 * wa]
        qb = group_norm(y[:, 3 * wa:4 * wa], 2)
        kb = group_norm(y[:, 4 * wa:5 * wa], 3)
        vb = y[:, 5 * wa:6 * wa]
        scale = HEAD_DIM ** -0.5
        p16_ref[:, 0 * wa:1 * wa] = (qa * scale).astype(BF16)
        p16_ref[:, 1 * wa:2 * wa] = ka.astype(BF16)
        p16_ref[:, 2 * wa:3 * wa] = va.astype(BF16)
        p16_ref[:, 3 * wa:4 * wa] = (qb * scale).astype(BF16)
        p16_ref[:, 4 * wa:5 * wa] = kb.astype(BF16)
        p16_ref[:, 5 * wa:6 * wa] = vb.astype(BF16)
        _store_tok(sample, kap_ref, kas_ref, ka)
        _store_tok(sample, vap_ref, vas_ref, va)
        if sample:
            kbs_ref[...] = kb
            vbs_ref[...] = vb
        else:
            @pl.when(i % tps == tps - 1)
            def _():
                kbp_ref[...] = kb
                vbp_ref[...] = vb

    _by_kind(i, npt, body)


def _ab_in_proj(x, g, sh, sc, w16, qkg_t, bd, bp, s):
    n_p, d = x[0].shape
    n_s = x[1].shape[0]
    n = n_p + n_s
    wa = A_HEADS * 2 * HEAD_DIM
    n_in = w16.shape[1]
    tok = lambda i: (i, 0)
    npt = n_p // TM
    tps = s // TM
    mod = _mod_specs(TM, d, npt, tps, bp)
    leaf = _tok_specs(TM, wa, npt)
    band = [pl.BlockSpec((None, TM, wa), lambda i: (jnp.minimum(i // tps, bp - 1), 0, 0)), leaf[1]]
    f32 = lambda rows: jax.ShapeDtypeStruct((rows, wa), F32)
    band_shape = [jax.ShapeDtypeStruct((bp, TM, wa), F32), f32(n_s)]
    return pl.pallas_call(
        functools.partial(_ab_in_kernel, npt=npt, tps=tps),
        grid=(n // TM,),
        in_specs=[
            *_tok_specs(TM, d, npt),
            _const_spec((1, d)),
            *mod,
            *mod,
            _const_spec((d, n_in)),
            _const_spec((4, wa)),
            _const_spec((wa, wa)),
        ],
        out_specs=[pl.BlockSpec((TM, n_in), tok)] + leaf + leaf + band + band,
        out_shape=[jax.ShapeDtypeStruct((n, n_in), BF16), f32(n_p), f32(n_s), f32(n_p), f32(n_s)]
        + band_shape + band_shape,
        compiler_params=_cparams(("arbitrary",)),
        name="ab_in_proj",
    )(*x, g, *sh, *sc, w16, qkg_t, bd)


def _t5_bucket(rel):
    half = T5_BUCKETS // 2
    exact = half // 2
    n = jnp.abs(rel)
    large = exact + (jnp.log(jnp.maximum(n, 1).astype(F32) / exact)
                     / math.log(T5_MAX_DIST / exact) * (half - exact)).astype(I32)
    large = jnp.minimum(large, half - 1)
    return jnp.where(rel > 0, half, 0) + jnp.where(n < exact, n, large)


def _lookup(table, idx):
    onehot = (idx[..., None] == jnp.arange(table.shape[0], dtype=I32)).astype(F32)
    return jnp.einsum('...n,nh->...h', onehot, table.astype(F32), precision=lax.Precision.HIGHEST)


def _diff_bias_prompt(t5_bias, tq):
    i = jnp.arange(tq)[None, :]
    j = jnp.arange(tq)[:, None]
    diag = jnp.where(((j // CHUNK) <= (i // CHUNK))[..., None], _lookup(t5_bias, _t5_bucket(j - i)), NEG)
    prev = _lookup(t5_bias, _t5_bucket(j - i - tq))
    first = jnp.concatenate([diag, jnp.full_like(diag, NEG)], axis=0)
    later = jnp.concatenate([prev, diag], axis=0)
    tiles = jnp.transpose(jnp.stack([first, later]), (3, 0, 1, 2))
    tiles = jnp.concatenate([tiles, tiles], axis=3)
    far = _lookup(t5_bias, _t5_bucket(jnp.full((1,), -T5_MAX_DIST, I32)))[0]
    return tiles, far


def _diff_bias_sample(t5_bias, past, t):
    qpos = past + jnp.arange(t)
    kpos = jnp.arange(past + t)
    rel = kpos[None, :] - qpos[:, None]
    vis = (kpos[None, :] // CHUNK) <= (qpos[:, None] // CHUNK)
    b = jnp.where(vis[..., None], _lookup(t5_bias, _t5_bucket(rel)), NEG)
    b = jnp.transpose(b, (2, 0, 1))
    b = jnp.concatenate([b, b], axis=1)
    return b[:, :, :past], b[:, :, past:]


def _band_bias_prompt(rel_bias, tq):
    nvar = BAND_PAST // tq + 1
    win = BAND_PAST + tq
    u = jnp.arange(nvar)[:, None, None]
    i = jnp.arange(tq)[None, :, None]
    j = jnp.arange(win)[None, None, :]
    qp = u * tq + i
    qc = qp // CHUNK
    kc = j // CHUNK
    valid = (kc <= qc) & (kc >= qc - BAND_CHUNKS)
    b = _lookup(rel_bias, jnp.clip(j - qp, -REL_CLIP, REL_CLIP) + REL_CLIP)
    b = jnp.where(valid[..., None], b, NEG)
    b = jnp.transpose(b, (0, 3, 1, 2))
    return b.reshape(nvar, B_HEADS // 2, 2 * tq, win)


def _band_bias_sample(rel_bias, past, lb, t):
    qpos = past + jnp.arange(t)
    kpos = past - lb + jnp.arange(lb + t)
    band_lo = (past // CHUNK - BAND_CHUNKS) * CHUNK
    rel = jnp.clip(kpos[None, :] - qpos[:, None], -REL_CLIP, REL_CLIP) + REL_CLIP
    b = jnp.where((kpos >= band_lo)[None, :, None], _lookup(rel_bias, rel), NEG)
    b = jnp.transpose(b, (2, 0, 1)).reshape(B_HEADS // 2, 2 * t, lb + t)
    return b[:, :, :lb], b[:, :, lb:]


def _stack_halves(q):
    lane = lax.broadcasted_iota(I32, q.shape, 1)
    zero = jnp.zeros_like(q)
    return jnp.concatenate([jnp.where(lane < HEAD_DIM, q, zero), jnp.where(lane >= HEAD_DIM, q, zero)], axis=0)


def _diff_finish(o1, o2, lam, hg, out_scale):
    o = o1 - lam * o2
    return (_rms(o, hg) * out_scale).astype(BF16)


def _band_finish(o):
    tq = o.shape[0] // 2
    lane = lax.broadcasted_iota(I32, (tq, o.shape[1]), 1)
    return jnp.where(lane < HEAD_DIM, o[:tq], o[tq:]).astype(BF16)


def _diff_prompt_kernel(far_ref, lam_ref, q_ref, k_ref, v_ref, bias_ref, hg_ref, o_ref,
                        q2t_s, vt_s, m_s, l_s, acc_s, *, tq, out_scale):
    h = pl.program_id(1)
    i = pl.program_id(2)

    @pl.when(i == 0)
    def _():
        for jj in range(vt_s.shape[0]):
            vt_s[jj] = v_ref[jj * tq:(jj + 1) * tq, :].astype(F32).T.astype(BF16)

    qt = q_ref[...].astype(F32).T.astype(BF16)
    row = lax.broadcasted_iota(I32, qt.shape, 0)
    zero = jnp.zeros_like(qt)
    q2t_s[:, 0:tq] = jnp.where(row < HEAD_DIM, qt, zero)
    q2t_s[:, tq:2 * tq] = jnp.where(row >= HEAD_DIM, qt, zero)
    m_s[...] = jnp.full(m_s.shape, NEG, F32)
    l_s[...] = jnp.zeros(l_s.shape, F32)
    acc_s[...] = jnp.zeros(acc_s.shape, F32)

    def step(jb, nblk, bias, shift):
        kb = k_ref[pl.ds(pl.multiple_of(jb * tq, tq), nblk * tq), :]
        s = _dot(kb, q2t_s[...])
        if bias is not None:
            s = s + bias
        cmax = jnp.max(s, axis=0, keepdims=True)
        if shift is not None:
            cmax = cmax + shift
        m_prev = m_s[...]
        m_new = jnp.maximum(m_prev, cmax)
        alpha = jnp.exp(m_prev - m_new)
        p = jnp.exp(s - (m_new if shift is None else m_new - shift))
        l_s[...] = alpha * l_s[...] + jnp.sum(p, axis=0, keepdims=True)
        pb = p.astype(BF16)
        pv = _dot(vt_s[jb], pb[0:tq])
        for u in range(1, nblk):
            pv = pv + _dot(vt_s[jb + u], pb[u * tq:(u + 1) * tq])
        acc_s[...] = alpha * acc_s[...] + pv
        m_s[...] = m_new

    far = far_ref[h]
    nfar = jnp.maximum(i - 1, 0)

    def far_pair(j, carry):
        step(2 * j, 2, None, far)
        return carry

    lax.fori_loop(0, nfar // 2, far_pair, 0)

    @pl.when(nfar % 2 == 1)
    def _():
        step(nfar - 1, 1, None, far)

    step(nfar, 2, bias_ref[...], None)
    o = acc_s[...] / l_s[...]
    od = o[:, 0:tq] - lam_ref[0] * o[:, tq:2 * tq]
    on = od * lax.rsqrt(jnp.mean(od * od, axis=0, keepdims=True) + EPS) * hg_ref[...] * out_scale
    o_ref[...] = on.T.astype(BF16)


def _diff_attn_prompt(p16, tiles, far, lam, hg_col, bp, s, out_scale):
    tq = TQ_DIFF
    nq = s // tq
    wa = A_HEADS * LANES
    kern = functools.partial(_diff_prompt_kernel, tq=tq, out_scale=out_scale)
    return pl.pallas_call(
        kern,
        grid=(bp, A_HEADS, nq),
        in_specs=[
            pl.BlockSpec(memory_space=pltpu.SMEM),
            pl.BlockSpec(memory_space=pltpu.SMEM),
            pl.BlockSpec((tq, LANES), lambda b, h, i: (b * nq + i, h)),
            pl.BlockSpec((s, LANES), lambda b, h, i: (b, A_HEADS + h)),
            pl.BlockSpec((s, LANES), lambda b, h, i: (b, 2 * A_HEADS + h)),
            pl.BlockSpec((None, None, 2 * tq, 2 * tq), lambda b, h, i: (h, jnp.minimum(i, 1), 0, 0)),
            _const_spec((LANES, 1)),
        ],
        out_specs=pl.BlockSpec((tq, LANES), lambda b, h, i: (b * nq + i, h)),
        out_shape=jax.ShapeDtypeStruct((bp * s, wa), BF16),
        scratch_shapes=[
            pltpu.VMEM((LANES, 2 * tq), BF16),
            pltpu.VMEM((nq, LANES, tq), BF16),
            pltpu.VMEM((1, 2 * tq), F32),
            pltpu.VMEM((1, 2 * tq), F32),
            pltpu.VMEM((LANES, 2 * tq), F32),
        ],
        compiler_params=_cparams(("parallel", "parallel", "arbitrary")),
        name="diff_attn_prompt",
    )(far, lam, p16, p16, p16, tiles, hg_col)


def _band_prompt_kernel(q_ref, k_ref, v_ref, bias_ref, o_ref, *, tq, win):
    t = pl.program_id(2)
    start = pl.multiple_of(jnp.maximum(t * tq - BAND_PAST, 0), tq)
    for c in range(q_ref.shape[1] // LANES):
        cs = slice(c * LANES, (c + 1) * LANES)
        kb = k_ref[pl.ds(start, win), cs]
        vb = v_ref[pl.ds(start, win), cs]
        s = _dot_nt(_stack_halves(q_ref[:, cs]), kb) + bias_ref[c]
        m = jnp.max(s, axis=-1, keepdims=True)
        p = jnp.exp(s - m)
        l = jnp.sum(p, axis=-1, keepdims=True)
        o_ref[:, cs] = _band_finish(_dot(p.astype(BF16), vb) / l)


def _band_attn_prompt(p16, bias, bp, s):
    tq = TQ_BAND
    nq = s // tq
    win = BAND_PAST + tq
    nvar = bias.shape[0]
    npair = B_HEADS // 2
    pps = 4
    wide = pps * LANES
    c0 = 3 * A_HEADS // pps
    ng = npair // pps
    kern = functools.partial(_band_prompt_kernel, tq=tq, win=win)
    return pl.pallas_call(
        kern,
        grid=(bp, ng, nq),
        in_specs=[
            pl.BlockSpec((tq, wide), lambda b, p, t: (b * nq + t, c0 + p)),
            pl.BlockSpec((s, wide), lambda b, p, t: (b, c0 + ng + p)),
            pl.BlockSpec((s, wide), lambda b, p, t: (b, c0 + 2 * ng + p)),
            pl.BlockSpec((None, pps, 2 * tq, win), lambda b, p, t: (jnp.minimum(t, nvar - 1), p, 0, 0)),
        ],
        out_specs=pl.BlockSpec((tq, wide), lambda b, p, t: (b * nq + t, p)),
        out_shape=jax.ShapeDtypeStruct((bp * s, npair * LANES), BF16),
        compiler_params=_cparams(("parallel", "parallel", "parallel")),
        name="band_attn_prompt",
    )(p16, p16, p16, bias)


def _sample_attn_kernel(lam_ref, q_ref, kc_ref, vc_ref, kn_ref, vn_ref, bc_ref, bn_ref, hg_ref, o_ref,
                        *, diff, out_scale):
    t = q_ref.shape[0]
    q2 = _stack_halves(q_ref[...])
    sc = _dot_nt(q2, kc_ref[...].astype(BF16)) + bc_ref[...]
    sn = _dot_nt(q2, kn_ref[...]) + bn_ref[...]
    m = jnp.maximum(jnp.max(sc, axis=-1, keepdims=True), jnp.max(sn, axis=-1, keepdims=True))
    pc = jnp.exp(sc - m)
    pn = jnp.exp(sn - m)
    l = jnp.sum(pc, axis=-1, keepdims=True) + jnp.sum(pn, axis=-1, keepdims=True)
    o = (_dot(pc.astype(BF16), vc_ref[...].astype(BF16)) + _dot(pn.astype(BF16), vn_ref[...])) / l
    if diff:
        o_ref[...] = _diff_finish(o[:t], o[t:], lam_ref[0], hg_ref[...], out_scale)
    else:
        o_ref[...] = _band_finish(o)


def _sample_attn(p16, cache_k, cache_v, bias_c, bias_n, lam, hg, np_rows, bs, t, diff, out_scale):
    past = cache_k.shape[1]
    ncol = cache_k.shape[2] // LANES
    row0 = np_rows // t
    if diff:
        qc, kc, vc = 0, A_HEADS, 2 * A_HEADS
    else:
        qc, kc, vc = 3 * A_HEADS, 3 * A_HEADS + ncol, 3 * A_HEADS + 2 * ncol
    kern = functools.partial(_sample_attn_kernel, diff=diff, out_scale=out_scale)
    return pl.pallas_call(
        kern,
        grid=(bs, ncol),
        in_specs=[
            pl.BlockSpec(memory_space=pltpu.SMEM),
            pl.BlockSpec((t, LANES), lambda b, h: (row0 + b, qc + h)),
            pl.BlockSpec((None, past, LANES), lambda b, h: (b, 0, h)),
            pl.BlockSpec((None, past, LANES), lambda b, h: (b, 0, h)),
            pl.BlockSpec((t, LANES), lambda b, h: (row0 + b, kc + h)),
            pl.BlockSpec((t, LANES), lambda b, h: (row0 + b, vc + h)),
            pl.BlockSpec((None, 2 * t, past), lambda b, h: (h, 0, 0)),
            pl.BlockSpec((None, 2 * t, t), lambda b, h: (h, 0, 0)),
            _const_spec((1, LANES)),
        ],
        out_specs=pl.BlockSpec((None, t, LANES), lambda b, h: (b, 0, h)),
        out_shape=jax.ShapeDtypeStruct((bs, t, ncol * LANES), BF16),
        compiler_params=_cparams(("parallel", "parallel")),
        name="diff_attn_sample" if diff else "band_attn_sample",
    )(lam, p16, cache_k, cache_v, p16, p16, bias_c, bias_n, hg)


def _out_proj_kernel(a0p_ref, a0s_ref, a1p_ref, a1s_ref, w_ref, xp_ref, xs_ref, gtp_ref, gts_ref, g2_ref,
                     shp_ref, shs_ref, scp_ref, scs_ref, rwh_ref, rwl_ref, xop_ref, xos_ref, hp_ref, lg_ref,
                     *, npt):
    half = a0p_ref.shape[1]

    def body(sample):
        mix = (_dot(_pick(sample, a0p_ref, a0s_ref), w_ref[0:half, :])
               + _dot(_pick(sample, a1p_ref, a1s_ref), w_ref[half:2 * half, :]))
        x = _pick(sample, xp_ref, xs_ref) + _pick(sample, gtp_ref, gts_ref) * mix
        _store_tok(sample, xop_ref, xos_ref, x)
        h2 = _rms(x, g2_ref[...]) * (1.0 + _pick(sample, scp_ref, scs_ref)) + _pick(sample, shp_ref, shs_ref)
        hp_ref[...] = _pack_rows(h2)
        h_hi, h_lo = _split2(h2)
        rw_hi = rwh_ref[...]
        lg_ref[...] = _dot_nt(rw_hi, h_hi) + _dot_nt(rw_hi, h_lo) + _dot_nt(rwl_ref[...], h_hi)

    _by_kind(pl.program_id(0), npt, body)


def _out_proj(a0, c0, a1, c1, w16, x, gt, g2, sh, sc, rw_hi, rw_lo, bp, s):
    n_p, d = x[0].shape
    n_s = x[1].shape[0]
    n = n_p + n_s
    half = d // 2
    ne = rw_hi.shape[0]
    tok = lambda i: (i, 0)
    npt = n_p // TM
    mod = _mod_specs(TM, d, npt, s // TM, bp)
    xspecs = _tok_specs(TM, d, npt)
    return pl.pallas_call(
        functools.partial(_out_proj_kernel, npt=npt),
        grid=(n // TM,),
        in_specs=[
            *_tok_specs(TM, half, npt, c0),
            *_tok_specs(TM, half, npt, c1),
            _const_spec((d, d)),
            *xspecs,
            *mod,
            _const_spec((1, d)),
            *mod,
            *mod,
            _const_spec((ne, d)),
            _const_spec((ne, d)),
        ],
        out_specs=xspecs + [pl.BlockSpec((TM, half), tok), pl.BlockSpec((ne, TM), lambda i: (0, i))],
        out_shape=[jax.ShapeDtypeStruct((n_p, d), F32), jax.ShapeDtypeStruct((n_s, d), F32),
                   jax.ShapeDtypeStruct((n, half), U32), jax.ShapeDtypeStruct((ne, n), F32)],
        compiler_params=_cparams(("arbitrary",)),
        name="out_proj",
    )(*a0, *a1, w16, *x, *gt, g2, *sh, *sc, rw_hi, rw_lo)


def _route_kernel(lg_ref, rb_ref, tri_ref, idx_ref, gate_ref, rank_ref, cnt_ref, carry_s):
    @pl.when(pl.program_id(0) == 0)
    def _():
        carry_s[...] = jnp.zeros(carry_s.shape, F32)

    s = jax.nn.sigmoid(lg_ref[...])
    sb = s + rb_ref[...]
    row = lax.broadcasted_iota(I32, s.shape, 0).astype(F32)
    picks = []
    sel = jnp.zeros(s.shape, F32)
    for _ in range(TOP_K):
        m = jnp.max(sb, axis=0, keepdims=True)
        ik = jnp.min(jnp.where(sb == m, row, float(N_EXPERTS)), axis=0, keepdims=True)
        oh = row == ik
        picks.append((ik, oh, jnp.sum(jnp.where(oh, s, 0.0), axis=0, keepdims=True)))
        sel = sel + oh.astype(F32)
        sb = jnp.where(oh, -jnp.inf, sb)
    before = _dot(sel.astype(BF16), tri_ref[...]) + carry_s[...]
    gsum = functools.reduce(lambda a, b: a + b, [g for _, _, g in picks])
    for k, (ik, oh, g) in enumerate(picks):
        idx_ref[k:k + 1, :] = ik.astype(I32)
        gate_ref[k:k + 1, :] = g / gsum * ROUTE_SCALE
        rank_ref[k:k + 1, :] = jnp.sum(jnp.where(oh, before, 0.0), axis=0, keepdims=True).astype(I32)
    carry_s[...] = carry_s[...] + jnp.sum(sel, axis=1, keepdims=True)
    cnt_ref[...] = carry_s[...]


def _route(lg_t, rb, tri):
    ne, n = lg_t.shape
    tm = tri.shape[0]
    tokk = lambda i: (0, i)
    return pl.pallas_call(
        _route_kernel,
        grid=(n // tm,),
        in_specs=[pl.BlockSpec((ne, tm), tokk), _const_spec((ne, 1)), _const_spec((tm, tm))],
        out_specs=[pl.BlockSpec((TOP_K, tm), tokk)] * 3 + [_const_spec((ne, 1))],
        out_shape=[jax.ShapeDtypeStruct((TOP_K, n), I32), jax.ShapeDtypeStruct((TOP_K, n), F32),
                   jax.ShapeDtypeStruct((TOP_K, n), I32), jax.ShapeDtypeStruct((ne, 1), F32)],
        scratch_shapes=[pltpu.VMEM((ne, 1), F32)],
        compiler_params=_cparams(("arbitrary",)),
        name="moe_route",
    )(lg_t, rb, tri)


def _dispatch_kernel(pad_ref, dest_ref, h_ref, s1_ref, s3_ref, s2_ref, xs_ref, shared_ref, zero_s, sem, *, tm):
    i = pl.program_id(0)
    nrow = zero_s.shape[0]

    @pl.when(i == 0)
    def _():
        zero_s[...] = jnp.zeros(zero_s.shape, U32)
        nblk = pad_ref.shape[0] - 1

        def fill(j, c):
            @pl.when(pad_ref[j] != 0)
            def _():
                pltpu.make_async_copy(zero_s, xs_ref.at[pl.ds(pl.multiple_of(j * nrow, nrow), nrow), :], sem).start()

            return c

        lax.fori_loop(0, nblk, fill, 0)

        def drain(j, c):
            pltpu.make_async_copy(zero_s, xs_ref.at[pl.ds(0, nrow), :], sem).wait()
            return c

        lax.fori_loop(0, pad_ref[nblk], drain, 0)

    def issue_row(r, c):
        for k in range(TOP_K):
            d = dest_ref[r * TOP_K + k]
            pltpu.make_async_copy(h_ref.at[pl.ds(r, 1), :], xs_ref.at[pl.ds(d, 1), :], sem).start(priority=k % 2)
        return c

    lax.fori_loop(0, tm, issue_row, 0)
    shared_ref[...] = _swiglu_packed(h_ref[...], s1_ref, s3_ref, s2_ref)
    pltpu.make_async_copy(xs_ref.at[pl.ds(0, tm * TOP_K), :], xs_ref.at[pl.ds(0, tm * TOP_K), :], sem).wait()


def _dispatch(pad_start, dest_flat, hp, s1, s3, s2, cap):
    n, d = hp.shape
    dm, ds_ = s1.shape
    tm = TM_MOVE
    kern = functools.partial(_dispatch_kernel, tm=tm)
    return pl.pallas_call(
        kern,
        grid_spec=pltpu.PrefetchScalarGridSpec(
            num_scalar_prefetch=1,
            grid=(n // tm,),
            in_specs=[
                pl.BlockSpec((tm * TOP_K,), lambda i, ps: (i,), memory_space=pltpu.SMEM),
                pl.BlockSpec((tm, d), lambda i, ps: (i, 0)),
                _const_spec((dm, ds_)),
                _const_spec((dm, ds_)),
                _const_spec((ds_, dm)),
            ],
            out_specs=[pl.BlockSpec(memory_space=pl.ANY), pl.BlockSpec((tm, dm), lambda i, ps: (i, 0))],
            scratch_shapes=[pltpu.VMEM((BM, d), U32), pltpu.SemaphoreType.DMA(())],
        ),
        out_shape=[jax.ShapeDtypeStruct((cap, d), U32), jax.ShapeDtypeStruct((n, dm), F32)],
        compiler_params=_cparams(("arbitrary",)),
        name="moe_dispatch",
    )(pad_start, dest_flat, hp, s1, s3, s2)


def _experts_kernel(exp_ref, nused_ref, x_ref, w1_ref, w3_ref, w2_ref, y_ref, w1_s, w3_s, w2_s):
    i = pl.program_id(0)
    used = i < nused_ref[0]

    @pl.when(jnp.logical_or(i == 0, exp_ref[i] != exp_ref[jnp.maximum(i - 1, 0)]))
    def _():
        w1_s[...] = w1_ref[...].astype(BF16)
        w3_s[...] = w3_ref[...].astype(BF16)
        w2_s[...] = w2_ref[...].astype(BF16)

    @pl.when(used)
    def _():
        y_ref[...] = _pack_rows(_swiglu_packed(x_ref[...], w1_s, w3_s, w2_s))

    @pl.when(jnp.logical_not(used))
    def _():
        y_ref[...] = jnp.zeros(y_ref.shape, U32)


def _swiglu_packed(xp, w1_ref, w3_ref, w2_ref):
    half = xp.shape[1]
    hi, lo = [v.astype(BF16) for v in _unpack_rows(xp)]
    a = _dot(hi, w1_ref[0:half, :]) + _dot(lo, w1_ref[half:2 * half, :])
    b = _dot(hi, w3_ref[0:half, :]) + _dot(lo, w3_ref[half:2 * half, :])
    return _dot((a * jax.nn.sigmoid(a) * b).astype(BF16), w2_ref[...])


def _experts(blk_e, n_used, xs, w1, w3, w2, layer):
    d = w1.shape[2]
    de = w1.shape[3]
    nb = blk_e.shape[0]
    return pl.pallas_call(
        _experts_kernel,
        grid_spec=pltpu.PrefetchScalarGridSpec(
            num_scalar_prefetch=2,
            grid=(nb,),
            in_specs=[
                pl.BlockSpec((BM, d // 2), lambda i, e, u: (i, 0)),
                pl.BlockSpec((None, None, d, de), lambda i, e, u: (layer, e[i], 0, 0)),
                pl.BlockSpec((None, None, d, de), lambda i, e, u: (layer, e[i], 0, 0)),
                pl.BlockSpec((None, None, de, d), lambda i, e, u: (layer, e[i], 0, 0)),
            ],
            out_specs=pl.BlockSpec((BM, d // 2), lambda i, e, u: (i, 0)),
            scratch_shapes=[pltpu.VMEM((d, de), BF16), pltpu.VMEM((d, de), BF16), pltpu.VMEM((de, d), BF16)],
        ),
        out_shape=jax.ShapeDtypeStruct((nb * BM, d // 2), U32),
        compiler_params=_cparams(("arbitrary",)),
        name="moe_experts",
    )(blk_e, n_used, xs, w1, w3, w2)


def _combine_kernel(dcur_ref, dnxt_ref, ys_ref, g_ref, shared_ref, xp_ref, xs_ref, gtp_ref, gts_ref,
                    xop_ref, xos_ref, buf_s, routed_s, sem, *, tm, npt):
    i = pl.program_id(0)
    slot = i % 2
    last = i + 1 >= pl.num_programs(0)
    half = buf_s.shape[3]

    def issue_row(dest_ref, sl, r):
        for k in range(TOP_K):
            d = dest_ref[r * TOP_K + k]
            pltpu.make_async_copy(ys_ref.at[pl.ds(d, 1), :], buf_s.at[sl, k, pl.ds(r, 1), :],
                                  sem.at[sl]).start(priority=k % 2)

    @pl.when(i == 0)
    def _():
        def first(r, c):
            issue_row(dcur_ref, 0, r)
            return c

        lax.fori_loop(0, tm, first, 0)

    pltpu.make_async_copy(ys_ref.at[pl.ds(0, tm * TOP_K), :], ys_ref.at[pl.ds(0, tm * TOP_K), :],
                          sem.at[slot]).wait()

    def sweep(cur, nxt):
        def group(gi, c):
            r0 = pl.multiple_of(gi * 8, 8)
            if nxt is not None:
                for u in range(8):
                    issue_row(dnxt_ref, nxt, r0 + u)
            g = g_ref[pl.ds(r0, 8), :]
            r_hi = r_lo = None
            for k in range(TOP_K):
                hi, lo = _unpack_rows(buf_s[cur, k, pl.ds(r0, 8), :])
                r_hi = g[:, k:k + 1] * hi if k == 0 else r_hi + g[:, k:k + 1] * hi
                r_lo = g[:, k:k + 1] * lo if k == 0 else r_lo + g[:, k:k + 1] * lo
            routed_s[pl.ds(r0, 8), 0:half] = r_hi
            routed_s[pl.ds(r0, 8), half:2 * half] = r_lo
            return c

        lax.fori_loop(0, tm // 8, group, 0)

    for cur in range(2):
        @pl.when(jnp.logical_and(jnp.logical_not(last), slot == cur))
        def _():
            sweep(cur, 1 - cur)

    @pl.when(last)
    def _():
        sweep(slot, None)

    ffn = routed_s[...] + shared_ref[...]
    _by_kind(i, npt, lambda sample: _store_tok(
        sample, xop_ref, xos_ref, _pick(sample, xp_ref, xs_ref) + _pick(sample, gtp_ref, gts_ref) * ffn))


def _combine(dest_flat, ys, gates, shared, x, gt, bp, s):
    n_p, d = x[0].shape
    n_s = x[1].shape[0]
    n = n_p + n_s
    tm = TM_MOVE
    npt = n_p // tm
    nstep = n // tm
    xspecs = _tok_specs(tm, d, npt)
    kern = functools.partial(_combine_kernel, tm=tm, npt=npt)
    tok = lambda i: (i, 0)
    return pl.pallas_call(
        kern,
        grid=(nstep,),
        in_specs=[
            pl.BlockSpec((tm * TOP_K,), lambda i: (i,), memory_space=pltpu.SMEM),
            pl.BlockSpec((tm * TOP_K,), lambda i: (jnp.minimum(i + 1, nstep - 1),), memory_space=pltpu.SMEM),
            pl.BlockSpec(memory_space=pl.ANY),
            pl.BlockSpec((tm, TOP_K), tok),
            pl.BlockSpec((tm, d), tok),
            *xspecs,
            *_mod_specs(tm, d, npt, s // tm, bp),
        ],
        out_specs=xspecs,
        out_shape=[jax.ShapeDtypeStruct((n_p, d), F32), jax.ShapeDtypeStruct((n_s, d), F32)],
        scratch_shapes=[pltpu.VMEM((2, TOP_K, tm, d // 2), U32), pltpu.VMEM((tm, d), F32),
                        pltpu.SemaphoreType.DMA((2,))],
        compiler_params=_cparams(("arbitrary",)),
        name="moe_combine",
    )(dest_flat, dest_flat, ys, gates, shared, *x, *gt)


def _moe(lg_t, hp, x, gt, rb, w1, w3, w2, layer, s1, s3, s2, tri, bp, s):
    n = hp.shape[0]
    idx_t, gate_t, rank_t, cnt = _route(lg_t, rb.reshape(N_EXPERTS, 1).astype(F32), tri)
    counts = cnt[:, 0].astype(I32)
    padded = (counts + BM - 1) // BM * BM
    pend = jnp.cumsum(padded)
    pstart = pend - padded
    nb = (n * TOP_K + N_EXPERTS * (BM - 1) + BM - 1) // BM
    n_used = pend[-1] // BM
    blk_e = jnp.minimum(jnp.sum(pend[None, :] <= (jnp.arange(nb, dtype=I32) * BM)[:, None], axis=1),
                        N_EXPERTS - 1).astype(I32)
    dest_t = jnp.sum(jnp.where(idx_t[None] == jnp.arange(N_EXPERTS, dtype=I32)[:, None, None],
                               pstart[:, None, None], 0), axis=0) + rank_t
    dest_flat = dest_t.T.reshape(-1)
    blocks = jnp.arange(nb, dtype=I32)
    is_last = jnp.any((blocks[:, None] == (pend // BM - 1)[None, :]) & (padded > 0)[None, :], axis=1)
    flags = ((blocks >= n_used) | is_last).astype(I32)
    fill = jnp.concatenate([flags, jnp.sum(flags, keepdims=True)]).astype(I32)
    xs, shared = _dispatch(fill, dest_flat, hp, s1, s3, s2, nb * BM)
    ys = _experts(blk_e, n_used.reshape(1).astype(I32), xs, w1, w3, w2, layer)
    return _combine(dest_flat, ys, gate_t.T, shared, x, gt, bp, s)


def _mlstm_pre_kernel(xp_ref, xs_ref, g_ref, shp_ref, shs_ref, scp_ref, scs_ref, wxo_ref, wgh_ref, wgl_ref, bif_ref, cw_ref, cb_ref,
                      wq_ref, wk_ref, wv_ref, halo_ref,
                      q_ref, k_ref, v_ref, o_ref, xcv_ref, gates_ref, tail_ref, xcs_ref,
                      xpad_s, *, npt, tps, t_s):
    i = pl.program_id(0)
    tm, d = xp_ref.shape

    def body(sample):
        h = (_rms(_pick(sample, xp_ref, xs_ref), g_ref[...]) * (1.0 + _pick(sample, scp_ref, scs_ref))
             + _pick(sample, shp_ref, shs_ref))
        h_hi, h_lo = _split2(h)
        y = _dot(h_hi, wxo_ref[...])
        xc = y[:, :d]
        o_ref[...] = y[:, d:].astype(BF16)
        wgh = wgh_ref[...]
        gp = _dot(h_hi, wgh) + _dot(h_hi, wgl_ref[...]) + _dot(h_lo, wgh) + bif_ref[...]
        lane = lax.broadcasted_iota(I32, gp.shape, 1)
        log_sig = jnp.minimum(gp, 0.0) - jnp.log(1.0 + jnp.exp(-jnp.abs(gp)))
        gates_ref[...] = jnp.where(lane >= C_HEADS, log_sig, gp)

        if sample:
            xpad_s[0:8, :] = jnp.zeros((8, d), F32)
        else:
            @pl.when(i % tps == 0)
            def _():
                xpad_s[0:8, :] = jnp.zeros((8, d), F32)

        xpad_s[8:, :] = xc
        acc = xc * cw_ref[C_CONV - 1:C_CONV, :] + cb_ref[...]
        for j in range(1, C_CONV):
            prev = xpad_s[8 - j:8 - j + tm, :]
            if sample:
                row = lax.broadcasted_iota(I32, (tm, 1), 0) & (t_s - 1)
                prev = jnp.where(row < j, halo_ref[j - 1], prev)
            acc = acc + prev * cw_ref[C_CONV - 1 - j:C_CONV - j, :]
        xconv = acc * jax.nn.sigmoid(acc)
        xcv16 = xconv.astype(BF16)
        xc16 = xc.astype(BF16)
        xcv_ref[...] = xcv16
        for hh in range(C_HEADS):
            cs = slice(hh * C_HEAD_DIM, (hh + 1) * C_HEAD_DIM)
            q_ref[:, cs] = _dot(xcv16[:, cs], wq_ref[hh]).astype(BF16)
            k_ref[:, cs] = (_dot(xcv16[:, cs], wk_ref[hh]) * C_HEAD_DIM ** -0.5).astype(BF16)
            v_ref[:, cs] = _dot(xc16[:, cs], wv_ref[hh]).astype(BF16)
        if sample:
            xcs_ref[...] = xc
        else:
            xpad_s[0:8, :] = xc[tm - 8:, :]
            tail_ref[...] = xc[tm - 8:, :]

    _by_kind(i, npt, body)


def _mlstm_pre(x, g, sh, sc, wxo, wgh, wgl, bif, cw, cb, wq, wk, wv, halo, bp, s, t_s):
    d = x[0].shape[1]
    n = x[0].shape[0] + x[1].shape[0]
    npt = bp * s // TM
    tps = s // TM
    nst = n // TM - npt
    tok = lambda i: (i, 0)
    mod = _mod_specs(TM, d, npt, tps, bp)
    kern = functools.partial(_mlstm_pre_kernel, npt=npt, tps=tps, t_s=t_s)
    b16 = jax.ShapeDtypeStruct((n, d), BF16)
    return pl.pallas_call(
        kern,
        grid=(n // TM,),
        in_specs=[
            *_tok_specs(TM, d, npt),
            _const_spec((1, d)),
            *mod,
            *mod,
            _const_spec((d, 2 * d)),
            _const_spec((d, LANES)),
            _const_spec((d, LANES)),
            _const_spec((1, LANES)),
            _const_spec((C_CONV, d)),
            _const_spec((1, d)),
            _const_spec((C_HEADS, C_HEAD_DIM, C_HEAD_DIM)),
            _const_spec((C_HEADS, C_HEAD_DIM, C_HEAD_DIM)),
            _const_spec((C_HEADS, C_HEAD_DIM, C_HEAD_DIM)),
            pl.BlockSpec((None, C_CONV - 1, TM, d), lambda i: (jnp.maximum(i - npt, 0), 0, 0, 0),
                         pipeline_mode=pl.Buffered(1)),
        ],
        out_specs=[pl.BlockSpec((TM, d), tok)] * 5 + [
            pl.BlockSpec((TM, LANES), tok),
            pl.BlockSpec((None, 8, d), lambda i: (jnp.minimum(i // tps, bp - 1), 0, 0)),
            pl.BlockSpec((TM, d), lambda i: (jnp.maximum(i - npt, 0), 0)),
        ],
        out_shape=[b16] * 5 + [
            jax.ShapeDtypeStruct((n, LANES), F32),
            jax.ShapeDtypeStruct((bp, 8, d), F32),
            jax.ShapeDtypeStruct((nst * TM, d), F32),
        ],
        scratch_shapes=[pltpu.VMEM((TM + 8, d), F32)],
        compiler_params=_cparams(("arbitrary",)),
        name="mlstm_pre",
    )(*x, g, *sh, *sc, wxo, wgh, wgl, bif, cw, cb, wq, wk, wv, halo)


def _mlstm_scan_kernel(q_ref, k_ref, v_ref, o_ref, xcv_ref, gc_ref, gr_ref, tri_ref, trit_ref, hg_ref, sk_ref,
                       mem0_ref, nrm0_ref, mx0_ref,
                       a_ref, memo_ref, nrmo_ref, mxo_ref,
                       mem_s, nrm_s, mx_s, *, nc):
    c = pl.program_id(1)
    ln = q_ref.shape[0]

    @pl.when(c == 0)
    def _():
        mem_s[...] = mem0_ref[...]
        nrm_s[...] = nrm0_ref[...]
        mx_s[...] = mx0_ref[...]

    gc = gc_ref[...]
    gr = gr_ref[...]
    tri = tri_ref[...]
    trit = trit_ref[...]
    bc = functools.reduce(lambda a, b: a + b, [_dot(tri, p) for p in _split3(gc)])
    br = functools.reduce(lambda a, b: a + b, [_dot(p, trit) for p in _split3(gr)])
    causal = lax.broadcasted_iota(I32, (ln, ln), 1) <= lax.broadcasted_iota(I32, (ln, ln), 0)
    for h in range(C_HEADS):
        cs = slice(h * C_HEAD_DIM, (h + 1) * C_HEAD_DIM)
        b_col = bc[:, C_HEADS + h:C_HEADS + h + 1]
        ig_col = gc[:, h:h + 1]
        b_row = br[C_HEADS + h:C_HEADS + h + 1, :]
        ig_row = gr[h:h + 1, :]
        b_last = b_row[:, ln - 1:ln]
        mx = mx_s[h:h + 1, 0:1]
        logw = jnp.where(causal, b_col - b_row + ig_row, NEG)
        g = b_col + mx
        m_t = jnp.maximum(g, jnp.max(logw, axis=1, keepdims=True))
        w = jnp.exp(logw - m_t)
        inter = jnp.exp(g - m_t)
        qh = q_ref[:, cs]
        kh = k_ref[:, cs]
        vh = v_ref[:, cs]
        a = w * _dot_nt(qh, kh)
        mem = mem_s[h]
        nrm = nrm_s[h:h + 1, :]
        num = _dot(a.astype(BF16), vh) + inter * _dot(qh, mem.astype(BF16))
        den = jnp.sum(a, axis=1, keepdims=True) + inter * jnp.sum(qh.astype(F32) * nrm, axis=1, keepdims=True)
        hout = num / jnp.maximum(jnp.abs(den), jnp.exp(-m_t))
        logs = b_last - b_col + ig_col
        m_new = jnp.maximum(b_last + mx, jnp.max(logs, axis=0, keepdims=True))
        decay = jnp.exp(b_last + mx - m_new)
        kw = kh.astype(F32) * jnp.exp(logs - m_new)
        mem_s[h] = decay * mem + _dot_tn(kw.astype(BF16), vh)
        nrm_s[h:h + 1, :] = decay * nrm + jnp.sum(kw, axis=0, keepdims=True)
        mx_s[h:h + 1, :] = jnp.broadcast_to(m_new, (1, mx_s.shape[1]))
        hh = hout * jax.nn.sigmoid(o_ref[:, cs].astype(F32))
        a_ref[:, cs] = (_rms(hh, hg_ref[:, cs]) + sk_ref[:, cs] * xcv_ref[:, cs].astype(F32)).astype(BF16)

    @pl.when(c == nc - 1)
    def _():
        memo_ref[...] = mem_s[...]
        nrmo_ref[...] = nrm_s[...]
        mxo_ref[...] = mx_s[...]


def _mlstm_scan(q, k, v, o, xcv, gates, hg, sk, mem0, nrm0, mx0, row0, nb, nc, ln):
    d = q.shape[1]
    nrow = nb * nc * ln
    gsl = lax.slice_in_dim(gates, row0 * ln, row0 * ln + nrow, axis=0)[:, :16]
    gr = jnp.transpose(gsl.reshape(nb * nc, ln, 16), (0, 2, 1))
    r = jnp.arange(ln)
    tri = (r[None, :] <= r[:, None]).astype(BF16)
    chunk = lambda b, c: (row0 + b * nc + c, 0)
    seq4 = lambda b, c: (b, 0, 0, 0)
    seq3 = lambda b, c: (b, 0, 0)
    kern = functools.partial(_mlstm_scan_kernel, nc=nc)
    return pl.pallas_call(
        kern,
        grid=(nb, nc),
        in_specs=[pl.BlockSpec((ln, d), chunk)] * 5 + [
            pl.BlockSpec((ln, LANES), chunk),
            pl.BlockSpec((None, 16, ln), lambda b, c: (b * nc + c, 0, 0)),
            _const_spec((ln, ln)),
            _const_spec((ln, ln)),
            _const_spec((1, d)),
            _const_spec((1, d)),
            pl.BlockSpec((None, C_HEADS, C_HEAD_DIM, C_HEAD_DIM), seq4),
            pl.BlockSpec((None, 8, C_HEAD_DIM), seq3),
            pl.BlockSpec((None, 8, LANES), seq3),
        ],
        out_specs=[
            pl.BlockSpec((ln, d), lambda b, c: (b * nc + c, 0)),
            pl.BlockSpec((None, C_HEADS, C_HEAD_DIM, C_HEAD_DIM), seq4),
            pl.BlockSpec((None, 8, C_HEAD_DIM), seq3),
            pl.BlockSpec((None, 8, LANES), seq3),
        ],
        out_shape=[
            jax.ShapeDtypeStruct((nrow, d), BF16),
            jax.ShapeDtypeStruct((nb, C_HEADS, C_HEAD_DIM, C_HEAD_DIM), F32),
            jax.ShapeDtypeStruct((nb, 8, C_HEAD_DIM), F32),
            jax.ShapeDtypeStruct((nb, 8, LANES), F32),
        ],
        scratch_shapes=[
            pltpu.VMEM((C_HEADS, C_HEAD_DIM, C_HEAD_DIM), F32),
            pltpu.VMEM((8, C_HEAD_DIM), F32),
            pltpu.VMEM((8, LANES), F32),
        ],
        compiler_params=_cparams(("parallel", "arbitrary")),
        name="mlstm_scan",
    )(q, k, v, o, xcv, gates, gr, tri, tri.T, hg, sk, mem0, nrm0, mx0)


def _pad_heads(a, width):
    nb = a.shape[0]
    if a.ndim == 2:
        a = jnp.broadcast_to(a[:, :, None], (nb, C_HEADS, width))
    return jnp.concatenate([a.astype(F32), jnp.zeros((nb, 8 - C_HEADS, width), F32)], axis=1)


def kernel(x_prompt, x_sample, c_prompt, c_sample, cache_a_k, cache_a_v, cache_b_k, cache_b_v, state_c_mem, state_c_norm, state_c_max, state_c_conv, norm_g, w_mod, b_mod, t5_bias, ab_w_in, ab_qk_g, ab_lambda, ab_head_g, ab_rel_bias, ab_w_out, c_w_in, c_b_if, c_conv_w, c_conv_b, c_w_qkv, c_head_g, c_skip, c_w_out, router_w, router_b, exp_w1, exp_w3, exp_w2, sh_w1, sh_w3, sh_w2):
    bp, s, d = x_prompt.shape
    bs, t = x_sample.shape[:2]
    depth = norm_g.shape[0]
    past = cache_a_k.shape[2]
    lb = cache_b_k.shape[2]
    n_p = bp * s
    n_s = bs * t
    n_all = n_p + n_s
    assert s % TM == 0 and n_s % TM == 0 and TM % t == 0 and t & (t - 1) == 0
    assert s % TQ_DIFF == 0 and s % ML_CHUNK == 0 and s >= BAND_PAST + TQ_BAND and TQ_DIFF >= T5_MAX_DIST
    assert past % CHUNK == 0 and lb == BAND_PAST and t <= CHUNK and t >= C_CONV - 1 and TM == BAND_PAST

    def per_token(vec):
        return vec[:bp].reshape(bp, 1, d), jnp.repeat(vec[bp:], t, axis=0)

    x = (x_prompt.reshape(n_p, d), x_sample.reshape(n_s, d))
    c_all = jnp.concatenate([c_prompt, c_sample], axis=0)
    mods = _modulation(c_all, w_mod, b_mod)

    r = jnp.arange(TM)
    tri_route = (r[:, None] < r[None, :]).astype(BF16)
    hd = jnp.arange(A_HEADS * 2 * HEAD_DIM) // HEAD_DIM
    bd = (hd[:, None] == hd[None, :]).astype(BF16)

    leaves = {}
    for l in range(depth):
        m6 = [mods[l][:, j * d:(j + 1) * d] for j in range(6)]
        sh1, sc1, gt1, sh2, sc2, gt2 = [per_token(v) for v in m6]
        i = l // 2
        if l % 2 == 0:
            lam_init = 0.8 - 0.6 * math.exp(-0.3 * l)
            lp = ab_lambda[i].astype(F32)
            lam = (jnp.exp(jnp.sum(lp[0] * lp[1])) - jnp.exp(jnp.sum(lp[2] * lp[3])) + lam_init).reshape(1)
            qkg_t = jnp.tile(ab_qk_g[i].astype(F32), (1, A_HEADS * 2))
            p16, ka_p, ka_s, va_p, va_s, kb_p, kb_s, vb_p, vb_s = _ab_in_proj(
                x, norm_g[l, 0].reshape(1, d), sh1, sc1, ab_w_in[i].astype(BF16), qkg_t, bd, bp, s)
            hg = ab_head_g[i].reshape(1, 2 * HEAD_DIM).astype(F32)
            out_scale = 1.0 - lam_init
            tiles, far = _diff_bias_prompt(t5_bias, TQ_DIFF)
            oa_p = _diff_attn_prompt(p16, tiles, far, lam, hg.reshape(2 * HEAD_DIM, 1), bp, s, out_scale)
            ob_p = _band_attn_prompt(p16, _band_bias_prompt(ab_rel_bias[i], TQ_BAND), bp, s)
            dbc, dbn = _diff_bias_sample(t5_bias, past, t)
            oa_s = _sample_attn(p16, cache_a_k[i].reshape(bs, past, -1), cache_a_v[i].reshape(bs, past, -1),
                                dbc, dbn, lam, hg, n_p, bs, t, True, out_scale)
            bbc, bbn = _band_bias_sample(ab_rel_bias[i], past, lb, t)
            ob_s = _sample_attn(p16, cache_b_k[i].reshape(bs, lb, -1), cache_b_v[i].reshape(bs, lb, -1),
                                bbc, bbn, lam, hg, n_p, bs, t, False, out_scale)
            mix_in = ((oa_p, oa_s.reshape(n_s, -1)), 0, (ob_p, ob_s.reshape(n_s, -1)), 0)
            w_out16 = ab_w_out[i].astype(BF16)
            leaves.setdefault('akp', []).append(ka_p.reshape(bp, s, A_HEADS, 2, HEAD_DIM))
            leaves.setdefault('avp', []).append(va_p.reshape(bp, s, A_HEADS, 2 * HEAD_DIM))
            leaves.setdefault('aks', []).append(ka_s.reshape(bs, t, A_HEADS, 2, HEAD_DIM))
            leaves.setdefault('avs', []).append(va_s.reshape(bs, t, A_HEADS, 2 * HEAD_DIM))
            leaves.setdefault('bkp', []).append(kb_p.reshape(bp, TM, B_HEADS, HEAD_DIM))
            leaves.setdefault('bvp', []).append(vb_p.reshape(bp, TM, B_HEADS, HEAD_DIM))
            leaves.setdefault('bks', []).append(kb_s.reshape(bs, t, B_HEADS, HEAD_DIM))
            leaves.setdefault('bvs', []).append(vb_s.reshape(bs, t, B_HEADS, HEAD_DIM))
        else:
            w_in = c_w_in[i]
            wg = jnp.pad(w_in[:, 2 * d:].astype(F32), ((0, 0), (0, LANES - 2 * C_HEADS)))
            wgh = wg.astype(BF16)
            wgl = (wg - wgh.astype(F32)).astype(BF16)
            bif = jnp.pad(c_b_if[i].astype(F32).reshape(1, 2 * C_HEADS), ((0, 0), (0, LANES - 2 * C_HEADS)))
            cprev = state_c_conv[i].astype(F32)
            planes = []
            for j in range(1, C_CONV):
                rows = jnp.concatenate([cprev[:, C_CONV - 1 - j:, :], jnp.zeros((bs, t - j, d), F32)], axis=1)
                planes.append(rows.reshape(n_s // TM, TM, d))
            halo = jnp.stack(planes, axis=1)
            wqkv = c_w_qkv[i].astype(BF16)
            q, k, v, o, xcv, gates, tail, xcs = _mlstm_pre(
                x, norm_g[l, 0].reshape(1, d), sh1, sc1, w_in[:, :2 * d].astype(BF16), wgh, wgl, bif,
                c_conv_w[i].astype(F32), c_conv_b[i].reshape(1, d).astype(F32), wqkv[0], wqkv[1], wqkv[2],
                halo, bp, s, t)
            hg = c_head_g[i].reshape(1, d).astype(F32)
            sk = c_skip[i].reshape(1, d).astype(F32)
            zm = jnp.zeros((bp, C_HEADS, C_HEAD_DIM, C_HEAD_DIM), F32)
            a_p, mem_p, nrm_p, mx_p = _mlstm_scan(
                q, k, v, o, xcv, gates, hg, sk, zm, jnp.zeros((bp, 8, C_HEAD_DIM), F32),
                jnp.zeros((bp, 8, LANES), F32), 0, bp, s // ML_CHUNK, ML_CHUNK)
            a_s, mem_s, nrm_s, mx_s = _mlstm_scan(
                q, k, v, o, xcv, gates, hg, sk, state_c_mem[i].astype(F32),
                _pad_heads(state_c_norm[i], C_HEAD_DIM), _pad_heads(state_c_max[i], LANES),
                n_p // t, bs, 1, t)
            mix_in = ((a_p, a_s), 0, (a_p, a_s), 1)
            w_out16 = c_w_out[i].astype(BF16)
            leaves.setdefault('memp', []).append(mem_p)
            leaves.setdefault('normp', []).append(nrm_p[:, :C_HEADS])
            leaves.setdefault('maxp', []).append(mx_p[:, :C_HEADS, 0])
            leaves.setdefault('convp', []).append(tail[:, 8 - (C_CONV - 1):])
            leaves.setdefault('mems', []).append(mem_s)
            leaves.setdefault('norms', []).append(nrm_s[:, :C_HEADS])
            leaves.setdefault('maxs', []).append(mx_s[:, :C_HEADS, 0])
            leaves.setdefault('convs', []).append(xcs.reshape(bs, t, d)[:, t - (C_CONV - 1):])
        rw_t = router_w[l].astype(F32).T
        rw_hi = rw_t.astype(BF16)
        rw_lo = (rw_t - rw_hi.astype(F32)).astype(BF16)
        xp, xs, hp, lg_t = _out_proj(mix_in[0], mix_in[1], mix_in[2], mix_in[3], w_out16, x, gt1,
                                     norm_g[l, 1].reshape(1, d), sh2, sc2, rw_hi, rw_lo, bp, s)
        x = _moe(lg_t, hp, (xp, xs), gt2, router_b[l], exp_w1, exp_w3, exp_w2, l,
                 sh_w1[l].astype(BF16), sh_w3[l].astype(BF16), sh_w2[l].astype(BF16), tri_route, bp, s)

    order = ['akp', 'avp', 'aks', 'avs', 'bkp', 'bvp', 'bks', 'bvs',
             'memp', 'normp', 'maxp', 'convp', 'mems', 'norms', 'maxs', 'convs']
    return (x[0].reshape(bp, s, d), x[1].reshape(bs, t, d)) + tuple(
        jnp.stack(leaves[name]) for name in order)
```
